```python
import math
import jax
import jax.numpy as jnp
from jax import lax
import numpy as np

D_MODEL = 1024
BATCH = 4
SEQ = 4096
DEPTH = 2
DEC_BATCH = 128
DEC_SEQ = 4
PAST_LEN = 8192
PAGE_SIZE = 128

F32 = jnp.float32
ROPE_THETA = 500000.0
NORM_EPS = 1e-6
Q_BLOCK = 128
NEG = -1e30

GDN_HEADS = 4
GDN_DK = 128
GDN_DV = 128
GDN_CONV = 4
GDN_CHUNK = 64
GDN_QK = GDN_HEADS * GDN_DK
GDN_VW = GDN_HEADS * GDN_DV
GDN_CONV_DIM = 2 * GDN_QK + GDN_VW

MLA_HEADS = 8
MLA_Q_LORA = 384
MLA_KV_LORA = 256
MLA_NOPE = 64
MLA_ROPE = 32
MLA_V = 64
MLA_SCALE = (MLA_NOPE + MLA_ROPE) ** -0.5

NSA_HEADS = 8
NSA_GROUPS = 2
NSA_HPG = NSA_HEADS // NSA_GROUPS
NSA_DH = 64
NSA_ROT = NSA_DH // 4
NSA_CMP_BLOCK = 32
NSA_SEL_BLOCK = 64
NSA_TOPK = 16
NSA_WINDOW = 512
NSA_FORCE = 1000.0
NSA_SCALE = NSA_DH ** -0.5
NSA_KV_COLS = 2 * NSA_GROUPS * NSA_DH

BRANCH_W = 512

MOE_GROUPS = 4
MOE_EPG = 8
MOE_EXPERTS = MOE_GROUPS * MOE_EPG
MOE_TOPK = 2
MOE_DFF = 512
MOE_BLOCK = 128

IN_SPLITS = (GDN_CONV_DIM, GDN_VW, GDN_HEADS, GDN_HEADS,
             MLA_Q_LORA, MLA_KV_LORA, MLA_ROPE,
             NSA_HEADS * NSA_DH, NSA_KV_COLS, NSA_KV_COLS, NSA_KV_COLS, 3 * NSA_HEADS,
             3 * D_MODEL)
IN_COLS = sum(IN_SPLITS)

kernel_name = 'hybrid_gdn_mla_nsa_hmoe_step'


def rmsnorm(x, g):
    xf = x.astype(F32)
    y = xf * lax.rsqrt(jnp.mean(xf * xf, axis=-1, keepdims=True) + NORM_EPS)
    return (y * g.astype(F32)).astype(x.dtype)


def l2norm(x):
    xf = x.astype(F32)
    return (xf * lax.rsqrt(jnp.sum(xf * xf, axis=-1, keepdims=True) + NORM_EPS)).astype(x.dtype)


def rope(x, pos, rot):
    half = rot // 2
    inv = ROPE_THETA ** (-(jnp.arange(half, dtype=F32) * 2.0 / rot))
    ang = pos.astype(F32)[:, None] * inv[None, :]
    cos = jnp.cos(ang)[:, None, :]
    sin = jnp.sin(ang)[:, None, :]
    xf = x.astype(F32)
    x1 = xf[..., :half]
    x2 = xf[..., half:rot]
    out = jnp.concatenate([x1 * cos - x2 * sin, x2 * cos + x1 * sin, xf[..., rot:]], axis=-1)
    return out.astype(x.dtype)


def masked_softmax(s, mask, axis):
    s = jnp.where(mask, s, NEG)
    m = jnp.max(s, axis=axis, keepdims=True)
    p = jnp.where(mask, jnp.exp(s - m), 0.0)
    return p / jnp.maximum(jnp.sum(p, axis=axis, keepdims=True), 1e-30)


def to_blocks(a, qb):
    n, t = a.shape[:2]
    return jnp.moveaxis(a.reshape((n, t // qb, qb) + a.shape[2:]), 1, 0)


def from_blocks(a):
    a = jnp.moveaxis(a, 0, 1)
    return a.reshape((a.shape[0], a.shape[1] * a.shape[2]) + a.shape[3:])


def gdn_chunked(q, k, v, g, beta, s0, chunk):
    n, t, h, dk = q.shape
    dv = v.shape[-1]
    nc = t // chunk

    def blk(a):
        a = a.astype(F32).reshape((n, nc, chunk, h) + a.shape[3:])
        return jnp.moveaxis(jnp.moveaxis(a, 1, 0), 2, 3)

    q, k, v, g, beta = blk(q), blk(k), blk(v), blk(g), blk(beta)
    gc = jnp.cumsum(g, axis=-1)
    incl = jnp.tril(jnp.ones((chunk, chunk), bool))
    strict = jnp.tril(jnp.ones((chunk, chunk), bool), -1)
    diff = gc[..., :, None] - gc[..., None, :]
    decay = jnp.where(incl, jnp.exp(jnp.where(incl, diff, 0.0)), 0.0)
    kb = k * beta[..., None]
    a_kk = jnp.where(strict, jnp.einsum('...td,...sd->...ts', kb, k) * decay, 0.0)
    eye = jnp.eye(chunk, dtype=F32)
    rhs = jnp.concatenate([v * beta[..., None], kb * jnp.exp(gc)[..., None]], axis=-1)
    sol = lax.linalg.triangular_solve(eye + a_kk, rhs, left_side=True, lower=True, unit_diagonal=True)
    u, w = sol[..., :dv], sol[..., dv:]
    a_qk = jnp.einsum('...td,...sd->...ts', q, k) * decay

    def step(s, inp):
        qi, ki, ui, wi, gi, ai = inp
        v_new = ui - jnp.einsum('nhck,nhkv->nhcv', wi, s)
        o = (jnp.einsum('nhck,nhkv->nhcv', qi * jnp.exp(gi)[..., None], s)
             + jnp.einsum('nhts,nhsv->nhtv', ai, v_new))
        g_last = gi[..., -1:]
        s = (s * jnp.exp(g_last)[..., None]
             + jnp.einsum('nhck,nhcv->nhkv', ki * jnp.exp(g_last - gi)[..., None], v_new))
        return s, o

    s_fin, o = lax.scan(step, s0.astype(F32), (q, k, u, w, gc, a_qk))
    o = jnp.moveaxis(jnp.moveaxis(o, 3, 2), 0, 1).reshape(n, t, h, dv)
    return o, s_fin


def gdn_mixer(qkv_raw, z, b_lin, a_lin, conv_hist, s0, conv_w, a_log, dt_bias, out_g):
    n, t, _ = qkv_raw.shape
    xh = jnp.concatenate([conv_hist, qkv_raw], axis=1)
    qkv = jax.nn.silu(sum(xh[:, i:i + t] * conv_w[i] for i in range(GDN_CONV)))
    q, k, v = jnp.split(qkv, [GDN_QK, 2 * GDN_QK], axis=-1)
    q = l2norm(q.reshape(n, t, GDN_HEADS, GDN_DK)) * (GDN_DK ** -0.5)
    k = l2norm(k.reshape(n, t, GDN_HEADS, GDN_DK))
    v = v.reshape(n, t, GDN_HEADS, GDN_DV)
    beta = jax.nn.sigmoid(b_lin.astype(F32))
    g = -jnp.exp(a_log.astype(F32)) * jax.nn.softplus(a_lin.astype(F32) + dt_bias.astype(F32))
    chunk = GDN_CHUNK if t % GDN_CHUNK == 0 else t
    o, s_new = gdn_chunked(q, k, v, g, beta, s0, chunk)
    o = rmsnorm(o, out_g).astype(F32) * jax.nn.silu(z.reshape(n, t, GDN_HEADS, GDN_DV).astype(F32))
    return o.reshape(n, t, GDN_VW).astype(qkv_raw.dtype), s_new.astype(s0.dtype), xh[:, t:]


def mla_project(cq_raw, ckv_raw, kr_raw, pos, cq_g, ckv_g, w_uq, qn_g, qr_g, kr_g, w_uk):
    n, t, _ = cq_raw.shape
    q = (rmsnorm(cq_raw, cq_g) @ w_uq).reshape(n, t, MLA_HEADS, MLA_NOPE + MLA_ROPE)
    q_nope = rmsnorm(q[..., :MLA_NOPE], qn_g)
    q_rope = rope(rmsnorm(q[..., MLA_NOPE:], qr_g), pos, MLA_ROPE)
    ckv = rmsnorm(ckv_raw, ckv_g)
    krope = rope(rmsnorm(kr_raw, kr_g)[:, :, None, :], pos, MLA_ROPE)[:, :, 0, :]
    q_lat = jnp.einsum('nthd,hcd->nthc', q_nope, w_uk)
    return q_lat, q_rope, ckv, krope


def mla_attend(q_lat, q_rope, qpos, ckv, krope, kpos, w_uv):
    n, t = q_lat.shape[:2]
    s = (jnp.einsum('nthc,nsc->nhts', q_lat, ckv)
         + jnp.einsum('nthr,nsr->nhts', q_rope, krope)).astype(F32) * MLA_SCALE
    p = masked_softmax(s, (kpos[None, :] <= qpos[:, None])[None, None], -1)
    lat = jnp.einsum('nhts,nsc->nthc', p.astype(ckv.dtype), ckv)
    return jnp.einsum('nthc,hcv->nthv', lat, w_uv).reshape(n, t, MLA_HEADS * MLA_V)


def nsa_kv_rows(raw, kn_g, pos):
    n, t, _ = raw.shape
    kv = raw.reshape(n, t, 2, NSA_GROUPS, NSA_DH)
    k = rope(rmsnorm(kv[:, :, 0], kn_g), pos, NSA_ROT)
    return jnp.stack([k, kv[:, :, 1]], axis=2)


def nsa_compress(rows, wpos, proj):
    n, length = rows.shape[:2]
    nc = length // NSA_CMP_BLOCK
    blk = rows[:, :nc * NSA_CMP_BLOCK].reshape(n, nc, NSA_CMP_BLOCK, 2, NSA_GROUPS, NSA_DH)
    pooled = jnp.einsum('ncikgd,kid->nckgd', blk, wpos)
    return jnp.einsum('nckgd,kde->nckge', pooled, proj)


def nsa_core(q, qpos, kvc, gather_sel, n_sel, kvw, wpos, gates):
    n, tq = q.shape[:2]
    qg = q.reshape(n, tq, NSA_GROUPS, NSA_HPG, NSA_DH)
    nc = kvc.shape[1]
    s_c = jnp.einsum('ntgpd,ncgd->ntgpc', qg, kvc[:, :, 0]).astype(F32) * NSA_SCALE
    c_end = jnp.arange(nc, dtype=jnp.int32) * NSA_CMP_BLOCK + (NSA_CMP_BLOCK - 1)
    p_c = masked_softmax(s_c, (c_end[None, :] <= qpos[:, None])[None, :, None, None, :], -1)
    o_cmp = jnp.einsum('ntgpc,ncgd->ntgpd', p_c.astype(kvc.dtype), kvc[:, :, 1])
    ratio = NSA_SEL_BLOCK // NSA_CMP_BLOCK
    imp = jnp.pad(p_c.sum(axis=3), ((0, 0), (0, 0), (0, 0), (0, ratio * n_sel - nc)))
    imp = imp.reshape(n, tq, NSA_GROUPS, n_sel, ratio).sum(-1)
    cur = (qpos // NSA_SEL_BLOCK)[:, None]
    bidx = jnp.arange(n_sel, dtype=jnp.int32)[None, :]
    forced = ((bidx == 0) | (bidx == cur) | (bidx == cur - 1)).astype(F32)
    score = jnp.where((bidx > cur)[None, :, None, :], -1e9, imp + NSA_FORCE * forced[None, :, None, :])
    _, idx = lax.top_k(score, min(NSA_TOPK, n_sel))
    kvb = gather_sel(idx)
    kpos = idx[..., None] * NSA_SEL_BLOCK + jnp.arange(NSA_SEL_BLOCK, dtype=jnp.int32)
    s_s = jnp.einsum('ntgpd,ntgksd->ntgpks', qg, kvb[..., 0, :]).astype(F32) * NSA_SCALE
    p_s = masked_softmax(s_s, (kpos <= qpos[None, :, None, None, None])[:, :, :, None], (-2, -1))
    o_sel = jnp.einsum('ntgpks,ntgksd->ntgpd', p_s.astype(kvb.dtype), kvb[..., 1, :])
    s_w = jnp.einsum('ntgpd,nsgd->ntgps', qg, kvw[:, :, 0]).astype(F32) * NSA_SCALE
    dpos = qpos[:, None] - wpos[None, :]
    mask_w = (dpos >= 0) & (dpos < NSA_WINDOW) & (wpos[None, :] >= 0)
    p_w = masked_softmax(s_w, mask_w[None, :, None, None, :], -1)
    o_win = jnp.einsum('ntgps,nsgd->ntgpd', p_w.astype(kvw.dtype), kvw[:, :, 1])
    gt = jax.nn.sigmoid(gates.astype(F32)).reshape(n, tq, NSA_GROUPS, NSA_HPG, 3)
    o = gt[..., 0:1] * o_cmp + gt[..., 1:2] * o_sel + gt[..., 2:3] * o_win
    return o.reshape(n, tq, NSA_HEADS * NSA_DH).astype(q.dtype)


def grouped_experts(x, eid, ew, w1, w3, w2):
    m, d = x.shape
    n_assign = m * MOE_TOPK
    fe = eid.reshape(-1).astype(jnp.int32)
    order = jnp.argsort(fe)
    se = fe[order]
    stok = (jnp.arange(n_assign, dtype=jnp.int32) // MOE_TOPK)[order]
    sw = ew.reshape(-1)[order]
    counts = jax.ops.segment_sum(jnp.ones_like(fe), fe, num_segments=MOE_EXPERTS)
    padded = (counts + MOE_BLOCK - 1) // MOE_BLOCK * MOE_BLOCK
    ends = jnp.cumsum(padded)
    dest = (ends - padded)[se] + jnp.arange(n_assign, dtype=jnp.int32) - (jnp.cumsum(counts) - counts)[se]
    n_blk = -(-(n_assign + MOE_EXPERTS * (MOE_BLOCK - 1)) // MOE_BLOCK)
    rows = n_blk * MOE_BLOCK
    row_tok = jnp.full((rows,), m, jnp.int32).at[dest].set(stok)
    row_w = jnp.zeros((rows,), F32).at[dest].set(sw)
    blk_e = jnp.minimum(jnp.searchsorted(ends, jnp.arange(n_blk, dtype=jnp.int32) * MOE_BLOCK, side='right'),
                        MOE_EXPERTS - 1)
    xb = jnp.concatenate([x, jnp.zeros((1, d), x.dtype)], axis=0)[row_tok].reshape(n_blk, MOE_BLOCK, d)

    def expert_block(a):
        xe, e = a
        return (jax.nn.silu(xe @ w1[e]) * (xe @ w3[e])) @ w2[e]

    yb = lax.map(expert_block, (xb, blk_e)).reshape(rows, d)
    y = jax.ops.segment_sum(yb.astype(F32) * row_w[:, None], row_tok, num_segments=m + 1)[:m]
    return y.astype(x.dtype)


def hier_moe(h, w_group, w_expert, w1, w3, w2):
    n, t, d = h.shape
    x = h.reshape(n * t, d)
    g_logit = (x @ w_group).astype(F32)
    _, g_idx = lax.top_k(g_logit, 1)
    g_w = jnp.take_along_axis(jax.nn.softmax(g_logit, axis=-1), g_idx, axis=-1)
    e_logit = (x @ w_expert).astype(F32).reshape(-1, MOE_GROUPS, MOE_EPG)
    e_logit = jnp.take_along_axis(e_logit, g_idx[:, :, None], axis=1)[:, 0]
    e_val, e_idx = lax.top_k(e_logit, MOE_TOPK)
    w = jax.nn.softmax(e_val, axis=-1) * g_w
    eid = g_idx * MOE_EPG + e_idx
    return grouped_experts(x, eid, w, w1, w3, w2).reshape(n, t, d)


def setup_inputs(seed: int = 0) -> dict:
    key = jax.random.key(seed)
    keys = iter(jax.random.split(key, 64))

    def nrm(shape, scale):
        return jax.random.normal(next(keys), shape, F32) * scale

    def gain(shape):
        return 1.0 + 0.05 * jax.random.normal(next(keys), shape, F32)

    n_pages = PAST_LEN // PAGE_SIZE
    n_pool = (5 * DEC_BATCH * n_pages) // 4
    win = min(NSA_WINDOW, PAST_LEN)
    page_table = jax.random.permutation(next(keys), n_pool)[:DEC_BATCH * n_pages]
    page_table = page_table.reshape(DEC_BATCH, n_pages).astype(jnp.int32)
    dt = jnp.exp(jax.random.uniform(next(keys), (DEPTH, GDN_HEADS), F32, math.log(1e-3), math.log(1e-1)))
    a_log = jnp.log(jax.random.uniform(next(keys), (DEPTH, GDN_HEADS), F32, 1.0, 16.0))
    return {
        'x_prompt': nrm((BATCH, SEQ, D_MODEL), 1.0),
        'x_sample': nrm((DEC_BATCH, DEC_SEQ, D_MODEL), 1.0),
        'cache_mla_ckv': nrm((DEPTH, n_pool, PAGE_SIZE, MLA_KV_LORA), 1.0),
        'cache_mla_krope': nrm((DEPTH, n_pool, PAGE_SIZE, MLA_ROPE), 1.0),
        'cache_nsa_cmp': nrm((DEPTH, n_pool, PAGE_SIZE, 2, NSA_GROUPS, NSA_DH), 1.0),
        'cache_nsa_sel': nrm((DEPTH, n_pool, PAGE_SIZE, 2, NSA_GROUPS, NSA_DH), 1.0),
        'cache_nsa_win': nrm((DEPTH, DEC_BATCH, win, 2, NSA_GROUPS, NSA_DH), 1.0),
        'state_gdn_ssm': nrm((DEPTH, DEC_BATCH, GDN_HEADS, GDN_DK, GDN_DV), 0.1),
        'state_gdn_conv': nrm((DEPTH, DEC_BATCH, GDN_CONV - 1, GDN_CONV_DIM), 1.0),
        'page_table': page_table,
        'norm1_g': gain((DEPTH, D_MODEL)),
        'w_in': nrm((DEPTH, D_MODEL, IN_COLS), D_MODEL ** -0.5),
        'gdn_conv_w': nrm((DEPTH, GDN_CONV, GDN_CONV_DIM), GDN_CONV ** -0.5),
        'gdn_a_log': a_log,
        'gdn_dt_bias': dt + jnp.log(-jnp.expm1(-dt)),
        'gdn_out_g': gain((DEPTH, GDN_DV)),
        'mla_cq_g': gain((DEPTH, MLA_Q_LORA)),
        'mla_ckv_g': gain((DEPTH, MLA_KV_LORA)),
        'mla_w_uq': nrm((DEPTH, MLA_Q_LORA, MLA_HEADS * (MLA_NOPE + MLA_ROPE)), MLA_Q_LORA ** -0.5),
        'mla_qn_g': gain((DEPTH, MLA_NOPE)),
        'mla_qr_g': gain((DEPTH, MLA_ROPE)),
        'mla_kr_g': gain((DEPTH, MLA_ROPE)),
        'mla_w_uk': nrm((DEPTH, MLA_HEADS, MLA_KV_LORA, MLA_NOPE), MLA_KV_LORA ** -0.5),
        'mla_w_uv': nrm((DEPTH, MLA_HEADS, MLA_KV_LORA, MLA_V), MLA_KV_LORA ** -0.5),
        'nsa_qn_g': gain((DEPTH, NSA_DH)),
        'nsa_kn_g': gain((DEPTH, 3, NSA_DH)),
        'nsa_cmp_wpos': (1.0 + 0.1 * jax.random.normal(next(keys), (DEPTH, 2, NSA_CMP_BLOCK, NSA_DH), F32)) / NSA_CMP_BLOCK,
        'nsa_cmp_proj': nrm((DEPTH, 2, NSA_DH, NSA_DH), (NSA_CMP_BLOCK / NSA_DH) ** 0.5),
        'p_gdn': nrm((DEPTH, BRANCH_W, D_MODEL), BRANCH_W ** -0.5),
        'p_mla': nrm((DEPTH, BRANCH_W, D_MODEL), BRANCH_W ** -0.5),
        'p_nsa': nrm((DEPTH, BRANCH_W, D_MODEL), BRANCH_W ** -0.5),
        'w_out': nrm((DEPTH, D_MODEL, D_MODEL), D_MODEL ** -0.5),
        'norm2_g': gain((DEPTH, D_MODEL)),
        'moe_w_group': nrm((DEPTH, D_MODEL, MOE_GROUPS), D_MODEL ** -0.5),
        'moe_w_expert': nrm((DEPTH, D_MODEL, MOE_GROUPS * MOE_EPG), D_MODEL ** -0.5),
        'moe_w1': nrm((DEPTH, MOE_EXPERTS, D_MODEL, MOE_DFF), D_MODEL ** -0.5),
        'moe_w3': nrm((DEPTH, MOE_EXPERTS, D_MODEL, MOE_DFF), D_MODEL ** -0.5),
        'moe_w2': nrm((DEPTH, MOE_EXPERTS, MOE_DFF, D_MODEL), MOE_DFF ** -0.5),
    }


def reference(x_prompt, x_sample, cache_mla_ckv, cache_mla_krope, cache_nsa_cmp, cache_nsa_sel,
              cache_nsa_win, state_gdn_ssm, state_gdn_conv, page_table,
              norm1_g, w_in, gdn_conv_w, gdn_a_log, gdn_dt_bias, gdn_out_g,
              mla_cq_g, mla_ckv_g, mla_w_uq, mla_qn_g, mla_qr_g, mla_kr_g, mla_w_uk, mla_w_uv,
              nsa_qn_g, nsa_kn_g, nsa_cmp_wpos, nsa_cmp_proj,
              p_gdn, p_mla, p_nsa, w_out, norm2_g,
              moe_w_group, moe_w_expert, moe_w1, moe_w3, moe_w2):
    b_p, t_p, _ = x_prompt.shape
    b_s, t_s, _ = x_sample.shape
    page = cache_mla_ckv.shape[2]
    past = page_table.shape[1] * page
    win_s = cache_nsa_win.shape[2]
    pos_p = jnp.arange(t_p, dtype=jnp.int32)
    pos_s = past + jnp.arange(t_s, dtype=jnp.int32)
    qb = min(Q_BLOCK, t_p)
    n_qb = t_p // qb
    offsets = [sum(IN_SPLITS[:i + 1]) for i in range(len(IN_SPLITS) - 1)]
    g_i = jnp.arange(NSA_GROUPS)[None, None, :, None]

    def shared(x, pos, l, conv_hist, s0):
        n, t, _ = x.shape
        (qkv_raw, z, b_lin, a_lin, cq_raw, ckv_raw, kr_raw, nq_raw, ncmp_raw, nsel_raw, nwin_raw,
         ngate, br) = jnp.split(rmsnorm(x, norm1_g[l]) @ w_in[l], offsets, axis=-1)
        o_a, ssm, conv = gdn_mixer(qkv_raw, z, b_lin, a_lin, conv_hist, s0, gdn_conv_w[l],
                                   gdn_a_log[l], gdn_dt_bias[l], gdn_out_g[l])
        mla = mla_project(cq_raw, ckv_raw, kr_raw, pos, mla_cq_g[l], mla_ckv_g[l], mla_w_uq[l],
                          mla_qn_g[l], mla_qr_g[l], mla_kr_g[l], mla_w_uk[l])
        q = rope(rmsnorm(nq_raw.reshape(n, t, NSA_HEADS, NSA_DH), nsa_qn_g[l]), pos, NSA_ROT)
        kv_c = nsa_kv_rows(ncmp_raw, nsa_kn_g[l, 0], pos)
        kv_s = nsa_kv_rows(nsel_raw, nsa_kn_g[l, 1], pos)
        kv_w = nsa_kv_rows(nwin_raw, nsa_kn_g[l, 2], pos)
        return o_a, ssm, conv, mla, q, kv_c, kv_s, kv_w, ngate, br

    def merge(x, o_a, o_b, o_c, br, l):
        g_a, g_b, g_c = jnp.split(jax.nn.sigmoid(br.astype(F32)), 3, axis=-1)
        u = g_a * (o_a @ p_gdn[l]) + g_b * (o_b @ p_mla[l]) + g_c * (o_c @ p_nsa[l])
        x = x + u.astype(x.dtype) @ w_out[l]
        return x + hier_moe(rmsnorm(x, norm2_g[l]), moe_w_group[l], moe_w_expert[l],
                            moe_w1[l], moe_w3[l], moe_w2[l])

    def prompt_layer(x, l):
        o_a, ssm, conv, mla, q, kv_c, kv_s, kv_w, ngate, br = shared(
            x, pos_p, l, jnp.zeros((b_p, GDN_CONV - 1, GDN_CONV_DIM), x.dtype),
            jnp.zeros((b_p, GDN_HEADS, GDN_DK, GDN_DV), x.dtype))
        q_lat, q_rope, ckv, krope = mla
        o_b = from_blocks(lax.map(
            lambda a: mla_attend(a[0], a[1], a[2], ckv, krope, pos_p, mla_w_uv[l]),
            (to_blocks(q_lat, qb), to_blocks(q_rope, qb), pos_p.reshape(n_qb, qb))))
        kvc = nsa_compress(kv_c, nsa_cmp_wpos[l], nsa_cmp_proj[l])
        n_sel = t_p // NSA_SEL_BLOCK
        sel_blocks = kv_s.reshape(b_p, n_sel, NSA_SEL_BLOCK, 2, NSA_GROUPS, NSA_DH)
        b_i = jnp.arange(b_p)[:, None, None, None]

        def gather_sel(idx):
            return sel_blocks[b_i, idx, :, :, g_i]

        kvw_pad = jnp.pad(kv_w, ((0, 0), (NSA_WINDOW, 0), (0, 0), (0, 0), (0, 0)))

        def nsa_block(a):
            q_blk, g_blk, i = a
            start = i * qb
            kvw = lax.dynamic_slice_in_dim(kvw_pad, start, NSA_WINDOW + qb, axis=1)
            wpos = start - NSA_WINDOW + jnp.arange(NSA_WINDOW + qb, dtype=jnp.int32)
            return nsa_core(q_blk, start + jnp.arange(qb, dtype=jnp.int32), kvc, gather_sel, n_sel,
                            kvw, wpos, g_blk)

        o_c = from_blocks(lax.map(nsa_block, (to_blocks(q, qb), to_blocks(ngate, qb),
                                              jnp.arange(n_qb, dtype=jnp.int32))))
        y = merge(x, o_a, o_b, o_c, br, l)
        return y, (ssm, conv, ckv, krope, kv_c, kv_s, kv_w[:, t_p - min(NSA_WINDOW, t_p):])

    def sample_layer(x, l):
        o_a, ssm, conv, mla, q, kv_c, kv_s, kv_w, ngate, br = shared(
            x, pos_s, l, state_gdn_conv[l], state_gdn_ssm[l])
        q_lat, q_rope, ckv, krope = mla
        ckv_all = jnp.concatenate([cache_mla_ckv[l, page_table].reshape(b_s, past, MLA_KV_LORA), ckv], axis=1)
        kr_all = jnp.concatenate([cache_mla_krope[l, page_table].reshape(b_s, past, MLA_ROPE), krope], axis=1)
        o_b = mla_attend(q_lat, q_rope, pos_s, ckv_all, kr_all,
                         jnp.arange(past + t_s, dtype=jnp.int32), mla_w_uv[l])
        kvc_rows = jnp.concatenate(
            [cache_nsa_cmp[l, page_table].reshape(b_s, past, 2, NSA_GROUPS, NSA_DH), kv_c], axis=1)
        kvc = nsa_compress(kvc_rows, nsa_cmp_wpos[l], nsa_cmp_proj[l])
        n_past_blk = past // NSA_SEL_BLOCK
        n_sel = -(-(past + t_s) // NSA_SEL_BLOCK)
        n_new_blk = n_sel - n_past_blk
        bpp = page // NSA_SEL_BLOCK
        new_blocks = jnp.pad(kv_s, ((0, 0), (0, n_new_blk * NSA_SEL_BLOCK - t_s), (0, 0), (0, 0), (0, 0)))
        new_blocks = new_blocks.reshape(b_s, n_new_blk, NSA_SEL_BLOCK, 2, NSA_GROUPS, NSA_DH)
        b_i = jnp.arange(b_s)[:, None, None, None]

        def gather_sel(idx):
            pidx = jnp.minimum(idx, n_past_blk - 1)
            phys = page_table[b_i, pidx // bpp]
            rows = (pidx % bpp)[..., None] * NSA_SEL_BLOCK + jnp.arange(NSA_SEL_BLOCK)
            from_pool = cache_nsa_sel[l, phys[..., None], rows, :, g_i[..., None]]
            from_new = new_blocks[b_i, jnp.clip(idx - n_past_blk, 0, n_new_blk - 1), :, :, g_i]
            return jnp.where((idx < n_past_blk)[..., None, None, None], from_pool, from_new)

        kvw = jnp.concatenate([cache_nsa_win[l], kv_w], axis=1)
        wpos = past - win_s + jnp.arange(win_s + t_s, dtype=jnp.int32)
        o_c = nsa_core(q, pos_s, kvc, gather_sel, n_sel, kvw, wpos, ngate)
        y = merge(x, o_a, o_b, o_c, br, l)
        return y, (ssm, conv, ckv, krope, kv_c, kv_s, kvw[:, t_s:])

    xp, xs = x_prompt, x_sample
    p_states, s_states = [], []
    for l in range(DEPTH):
        xp, st = prompt_layer(xp, l)
        p_states.append(st)
        xs, st = sample_layer(xs, l)
        s_states.append(st)
    p_ssm, p_conv, p_ckv, p_krope, p_cmp, p_sel, p_win = [
        jnp.stack([st[i] for st in p_states], axis=0) for i in range(7)]
    s_ssm, s_conv, s_ckv, s_krope, s_cmp, s_sel, s_win = [
        jnp.stack([st[i] for st in s_states], axis=0) for i in range(7)]
    return (xp, xs, p_ssm, p_conv, p_ckv, p_krope, p_cmp, p_sel, p_win,
            s_ssm, s_conv, s_ckv, s_krope, s_cmp, s_sel, s_win)
```

```python
import functools
import math

import jax
import jax.numpy as jnp
from jax import lax
from jax.experimental import pallas as pl
from jax.experimental.pallas import tpu as pltpu

F32 = jnp.float32
BF16 = jnp.bfloat16

D_MODEL = 1024
ROPE_THETA = 500000.0
NORM_EPS = 1e-6
NEG = -1e30

GDN_HEADS = 4
GDN_DK = 128
GDN_DV = 128
GDN_CONV = 4
GDN_CHUNK = 64
GDN_QK = GDN_HEADS * GDN_DK
GDN_VW = GDN_HEADS * GDN_DV
GDN_CONV_DIM = 2 * GDN_QK + GDN_VW

MLA_HEADS = 8
MLA_Q_LORA = 384
MLA_KV_LORA = 256
MLA_NOPE = 64
MLA_ROPE = 32
MLA_V = 64
MLA_SCALE = (MLA_NOPE + MLA_ROPE) ** -0.5
MLA_QK_PAD = 384

NSA_HEADS = 8
NSA_GROUPS = 2
NSA_HPG = NSA_HEADS // NSA_GROUPS
NSA_DH = 64
NSA_ROT = NSA_DH // 4
NSA_CMP_BLOCK = 32
NSA_SEL_BLOCK = 64
NSA_TOPK = 16
NSA_WINDOW = 512
NSA_FORCE = 1000.0
NSA_SCALE = NSA_DH ** -0.5
NSA_KV_COLS = 2 * NSA_GROUPS * NSA_DH

MOE_GROUPS = 4
MOE_EPG = 8
MOE_EXPERTS = MOE_GROUPS * MOE_EPG
MOE_TOPK = 2
MOE_DFF = 512
MOE_BLOCK = 128
ROUTER_PAD = 128

IN_SPLITS = (GDN_CONV_DIM, GDN_VW, GDN_HEADS, GDN_HEADS,
             MLA_Q_LORA, MLA_KV_LORA, MLA_ROPE,
             NSA_HEADS * NSA_DH, NSA_KV_COLS, NSA_KV_COLS, NSA_KV_COLS, 3 * NSA_HEADS,
             3 * D_MODEL)
IN_COLS = sum(IN_SPLITS)
IN_COLS_PAD = 7168

VMEM_LIMIT = 48 * 1024 * 1024

_NT = (((1,), (1,)), ((), ()))


def _cparams(sem):
    return pltpu.CompilerParams(dimension_semantics=sem, vmem_limit_bytes=VMEM_LIMIT)


def _norm_matmul_kernel(x_ref, g_ref, w_ref, o_ref, h_ref, *, norm):
    @pl.when(pl.program_id(1) == 0)
    def _():
        x = x_ref[...].astype(F32)
        if norm:
            ms = jnp.mean(x * x, axis=-1, keepdims=True)
            x = x * lax.rsqrt(ms + NORM_EPS) * g_ref[...]
        h_ref[...] = x.astype(BF16)

    o_ref[...] = jnp.dot(h_ref[...], w_ref[...], preferred_element_type=F32)


def norm_matmul(x, g, w, *, norm=True, tm=512, tn=1024):
    m, k = x.shape
    n = w.shape[1]
    tm = min(tm, m)
    tn = min(tn, n)
    assert m % tm == 0 and n % tn == 0
    return pl.pallas_call(
        functools.partial(_norm_matmul_kernel, norm=norm),
        grid=(m // tm, n // tn),
        in_specs=[pl.BlockSpec((tm, k), lambda i, j: (i, 0)),
                  pl.BlockSpec((1, k), lambda i, j: (0, 0)),
                  pl.BlockSpec((k, tn), lambda i, j: (0, j))],
        out_specs=pl.BlockSpec((tm, tn), lambda i, j: (i, j)),
        out_shape=jax.ShapeDtypeStruct((m, n), F32),
        scratch_shapes=[pltpu.VMEM((tm, k), BF16)],
        compiler_params=_cparams(("parallel", "arbitrary")),
        name="norm_matmul",
    )(x, g.reshape(1, k).astype(F32), w)


def _mla_flash_kernel(qi_ref, kj_ref, q_ref, k_ref, wuv_ref, o_ref, m_ref, l_ref, acc_ref, *, tq, tk):
    p = pl.program_id(1)
    qi = qi_ref[p]
    kj = kj_ref[p]
    rows = MLA_HEADS * tq

    @pl.when(kj == 0)
    def _():
        m_ref[...] = jnp.full(m_ref.shape, NEG, F32)
        l_ref[...] = jnp.zeros(l_ref.shape, F32)
        acc_ref[...] = jnp.zeros(acc_ref.shape, F32)

    q = q_ref[0].reshape(rows, MLA_QK_PAD)
    k = k_ref[0]
    s = lax.dot_general(q, k, _NT, preferred_element_type=F32)
    qpos = qi * tq + (lax.broadcasted_iota(jnp.int32, (rows, 1), 0) & (tq - 1))
    kpos = kj * tk + lax.broadcasted_iota(jnp.int32, (1, tk), 1)
    s = jnp.where(kpos <= qpos, s, NEG)
    m_prev = m_ref[...]
    m_new = jnp.maximum(m_prev, jnp.max(s, axis=1, keepdims=True))
    alpha = jnp.exp(m_prev - m_new)
    pexp = jnp.exp(s - m_new)
    l_ref[...] = alpha * l_ref[...] + jnp.sum(pexp, axis=1, keepdims=True)
    acc_ref[...] = alpha * acc_ref[...] + jnp.dot(pexp.astype(BF16), k[:, :MLA_KV_LORA],
                                                  preferred_element_type=F32)
    m_ref[...] = m_new

    @pl.when(kj == (qi * tq + tq - 1) // tk)
    def _():
        lat = (acc_ref[...] / l_ref[...]).astype(BF16)
        outs = [jnp.dot(lat[h * tq:(h + 1) * tq], wuv_ref[h], preferred_element_type=F32)
                for h in range(MLA_HEADS)]
        o_ref[0] = jnp.concatenate(outs, axis=-1)


def mla_flash(q_cat, k_cat, w_uv, *, tq=128, tk=512):
    n, _, t, _ = q_cat.shape
    assert t % tq == 0 and t % tk == 0 and (tq & (tq - 1)) == 0
    pairs = [(i, j) for i in range(t // tq) for j in range((i * tq + tq - 1) // tk + 1)]
    qi_tab = jnp.asarray([a for a, _ in pairs], jnp.int32)
    kj_tab = jnp.asarray([b for _, b in pairs], jnp.int32)
    rows = MLA_HEADS * tq
    grid_spec = pltpu.PrefetchScalarGridSpec(
        num_scalar_prefetch=2,
        grid=(n, len(pairs)),
        in_specs=[pl.BlockSpec((1, MLA_HEADS, tq, MLA_QK_PAD), lambda b, p, qi, kj: (b, 0, qi[p], 0)),
                  pl.BlockSpec((1, tk, MLA_QK_PAD), lambda b, p, qi, kj: (b, kj[p], 0)),
                  pl.BlockSpec((MLA_HEADS, MLA_KV_LORA, MLA_V), lambda b, p, qi, kj: (0, 0, 0))],
        out_specs=pl.BlockSpec((1, tq, MLA_HEADS * MLA_V), lambda b, p, qi, kj: (b, qi[p], 0)),
        scratch_shapes=[pltpu.VMEM((rows, 1), F32), pltpu.VMEM((rows, 1), F32),
                        pltpu.VMEM((rows, MLA_KV_LORA), F32)])
    return pl.pallas_call(
        functools.partial(_mla_flash_kernel, tq=tq, tk=tk),
        grid_spec=grid_spec,
        out_shape=jax.ShapeDtypeStruct((n, t, MLA_HEADS * MLA_V), F32),
        compiler_params=_cparams(("parallel", "arbitrary")),
        name="mla_flash",
    )(qi_tab, kj_tab, q_cat, k_cat, w_uv)


def _softmax_rows(s, mask):
    s = jnp.where(mask, s, NEG)
    m = jnp.max(s, axis=-1, keepdims=True)
    p = jnp.where(mask, jnp.exp(s - m), 0.0)
    return p / jnp.maximum(jnp.sum(p, axis=-1, keepdims=True), 1e-30)


def _nsa_prompt_kernel(q_ref, kc_ref, vc_ref, ks_ref, vs_ref, kw_ref, vw_ref,
                       ocmp_ref, osel_ref, owin_ref, *, tq, tk, nc, n_sel):
    start = pl.program_id(1) * tq
    grows = NSA_HPG * tq
    q = q_ref[0].reshape(NSA_HEADS * tq, NSA_DH)
    tokpos = start + lax.broadcasted_iota(jnp.int32, (tq, 1), 0)
    qpos = start + (lax.broadcasted_iota(jnp.int32, (grows, 1), 0) & (tq - 1))
    lane = lax.broadcasted_iota(jnp.int32, (1, nc), 1)
    half = nc // 2
    cid = jnp.where(lane < half, 2 * lane, 2 * (lane - half) + 1)
    c_end = cid * NSA_CMP_BLOCK + (NSA_CMP_BLOCK - 1)
    blk_lane = lax.broadcasted_iota(jnp.int32, (1, nc), 1)
    blk_row_f = lax.broadcasted_iota(jnp.int32, (nc, tq), 0).astype(F32)
    cur = tokpos // NSA_SEL_BLOCK
    n_kt = (start + tq - 1) // tk + 1
    lo = jnp.maximum(start - NSA_WINDOW, 0)
    wlen = NSA_WINDOW + tq

    for g in range(NSA_GROUPS):
        qg = q[g * grows:(g + 1) * grows]

        sc = lax.dot_general(qg, kc_ref[0, g], _NT, preferred_element_type=F32)
        pc = _softmax_rows(sc, c_end <= qpos)
        ocmp = jnp.dot(pc.astype(BF16), vc_ref[0, g], preferred_element_type=F32)
        ocmp_ref[0, g * NSA_HPG:(g + 1) * NSA_HPG] = ocmp.reshape(NSA_HPG, tq, NSA_DH)

        psum = pc[0:tq]
        for h in range(1, NSA_HPG):
            psum = psum + pc[h * tq:(h + 1) * tq]
        imp = psum + jnp.concatenate([psum[:, half:], psum[:, :half]], axis=-1)
        forced = (blk_lane == 0) | (blk_lane == cur) | (blk_lane == cur - 1)
        score = jnp.where(blk_lane > cur, -1e9, imp + NSA_FORCE * forced.astype(F32))
        score = jnp.where(blk_lane < n_sel, score, -3e38)
        score_t = score.T
        sel_t = jnp.zeros((nc, tq), F32)
        for _ in range(NSA_TOPK):
            mx = jnp.max(score_t, axis=0, keepdims=True)
            first = jnp.min(jnp.where(score_t == mx, blk_row_f, float(nc)), axis=0, keepdims=True)
            pick = blk_row_f == first
            sel_t = jnp.where(pick, 1.0, sel_t)
            score_t = jnp.where(pick, -3e38, score_t)
        sel = sel_t.T.astype(BF16)

        def sel_step(kt, carry):
            m_prev, l_prev, acc = carry
            off = pl.multiple_of(kt * tk, tk)
            k = ks_ref[0, g, pl.ds(off, tk), :]
            v = vs_ref[0, g, pl.ds(off, tk), :]
            s = lax.dot_general(qg, k, _NT, preferred_element_type=F32)
            keyi = kt * tk + lax.broadcasted_iota(jnp.int32, (1, tk), 1)
            expand = (lax.broadcasted_iota(jnp.int32, (nc, tk), 0) == keyi // NSA_SEL_BLOCK).astype(BF16)
            selm = jnp.dot(sel, expand, preferred_element_type=F32)
            valid = (selm > 0.5) & (keyi <= tokpos)
            s3 = jnp.where(valid[None], s.reshape(NSA_HPG, tq, tk), NEG)
            m_new = jnp.maximum(m_prev, jnp.max(s3, axis=-1, keepdims=True))
            alpha = jnp.exp(m_prev - m_new)
            p3 = jnp.exp(s3 - m_new)
            l_new = alpha * l_prev + jnp.sum(p3, axis=-1, keepdims=True)
            pv = jnp.dot(p3.reshape(grows, tk).astype(BF16), v, preferred_element_type=F32)
            return m_new, l_new, alpha * acc + pv.reshape(NSA_HPG, tq, NSA_DH)

        init = (jnp.full((NSA_HPG, tq, 1), NEG, F32), jnp.zeros((NSA_HPG, tq, 1), F32),
                jnp.zeros((NSA_HPG, tq, NSA_DH), F32))
        _, l_fin, acc = lax.fori_loop(0, n_kt, sel_step, init)
        osel_ref[0, g * NSA_HPG:(g + 1) * NSA_HPG] = acc / l_fin

        woff = pl.multiple_of(lo, tq)
        kw = kw_ref[0, g, pl.ds(woff, wlen), :]
        vw = vw_ref[0, g, pl.ds(woff, wlen), :]
        sw = lax.dot_general(qg, kw, _NT, preferred_element_type=F32)
        dpos = tokpos - (lo + lax.broadcasted_iota(jnp.int32, (1, wlen), 1))
        validw = (dpos >= 0) & (dpos < NSA_WINDOW)
        sw3 = sw.reshape(NSA_HPG, tq, wlen)
        pw = _softmax_rows(sw3, jnp.broadcast_to(validw[None], sw3.shape))
        ow = jnp.dot(pw.reshape(grows, wlen).astype(BF16), vw, preferred_element_type=F32)
        owin_ref[0, g * NSA_HPG:(g + 1) * NSA_HPG] = ow.reshape(NSA_HPG, tq, NSA_DH)


def nsa_prompt(q8, kc, vc, ks, vs, kw, vw, *, tq=128, tk=512):
    n, _, t, _ = q8.shape
    nc = kc.shape[2]
    n_sel = t // NSA_SEL_BLOCK
    assert nc == 128 and 2 * n_sel == nc and t % tk == 0 and t >= NSA_WINDOW + tq
    full = lambda a: pl.BlockSpec((1,) + a.shape[1:], lambda b, i: (b, 0, 0, 0))
    ospec = pl.BlockSpec((1, NSA_HEADS, tq, NSA_DH), lambda b, i: (b, 0, i, 0))
    oshape = jax.ShapeDtypeStruct((n, NSA_HEADS, t, NSA_DH), F32)
    return pl.pallas_call(
        functools.partial(_nsa_prompt_kernel, tq=tq, tk=tk, nc=nc, n_sel=n_sel),
        grid=(n, t // tq),
        in_specs=[pl.BlockSpec((1, NSA_HEADS, tq, NSA_DH), lambda b, i: (b, 0, i, 0)),
                  full(kc), full(vc), full(ks), full(vs), full(kw), full(vw)],
        out_specs=[ospec, ospec, ospec],
        out_shape=[oshape, oshape, oshape],
        compiler_params=_cparams(("parallel", "arbitrary")),
        name="nsa_prompt",
    )(q8, kc, vc, ks, vs, kw, vw)


def _merge_kernel(x_ref, oa_ref, ob_ref, oc_ref, br_ref, pa_ref, pb_ref, pc_ref, wo_ref, y_ref):
    d = D_MODEL
    br = br_ref[...]
    u = (jax.nn.sigmoid(br[:, 0:d]) * jnp.dot(oa_ref[...].astype(BF16), pa_ref[...], preferred_element_type=F32)
         + jax.nn.sigmoid(br[:, d:2 * d]) * jnp.dot(ob_ref[...].astype(BF16), pb_ref[...], preferred_element_type=F32)
         + jax.nn.sigmoid(br[:, 2 * d:3 * d]) * jnp.dot(oc_ref[...].astype(BF16), pc_ref[...], preferred_element_type=F32))
    y_ref[...] = x_ref[...] + jnp.dot(u.astype(BF16), wo_ref[...], preferred_element_type=F32)


def merge(x, o_a, o_b, o_c, br, p_a, p_b, p_c, w_out, *, tm=256):
    m, d = x.shape
    tm = min(tm, m)
    bw = o_a.shape[1]
    row = lambda w: pl.BlockSpec((tm, w), lambda i: (i, 0))
    res = lambda a: pl.BlockSpec(a.shape, lambda i: (0, 0))
    return pl.pallas_call(
        _merge_kernel,
        grid=(m // tm,),
        in_specs=[row(d), row(bw), row(bw), row(bw), row(3 * d), res(p_a), res(p_b), res(p_c), res(w_out)],
        out_specs=row(d),
        out_shape=jax.ShapeDtypeStruct((m, d), F32),
        compiler_params=_cparams(("parallel",)),
        name="merge",
    )(x, o_a, o_b, o_c, br, p_a, p_b, p_c, w_out)


def _router_kernel(x_ref, g_ref, wr_ref, h_ref, logit_ref):
    x = x_ref[...]
    ms = jnp.mean(x * x, axis=-1, keepdims=True)
    h = x * lax.rsqrt(ms + NORM_EPS) * g_ref[...]
    h_ref[...] = h.astype(BF16)
    logit_ref[...] = jnp.dot(h, wr_ref[...], preferred_element_type=F32, precision=lax.Precision.HIGHEST)


def norm_router(x, g, w_router, *, tm=512):
    m, d = x.shape
    tm = min(tm, m)
    return pl.pallas_call(
        _router_kernel,
        grid=(m // tm,),
        in_specs=[pl.BlockSpec((tm, d), lambda i: (i, 0)),
                  pl.BlockSpec((1, d), lambda i: (0, 0)),
                  pl.BlockSpec((d, ROUTER_PAD), lambda i: (0, 0))],
        out_specs=[pl.BlockSpec((tm, d), lambda i: (i, 0)),
                   pl.BlockSpec((tm, ROUTER_PAD), lambda i: (i, 0))],
        out_shape=[jax.ShapeDtypeStruct((m, d), BF16), jax.ShapeDtypeStruct((m, ROUTER_PAD), F32)],
        compiler_params=_cparams(("parallel",)),
        name="norm_router",
    )(x, g.reshape(1, d), w_router)


def _expert_kernel(be_ref, x_ref, rw_ref, w1_ref, w3_ref, w2_ref, o_ref):
    x = x_ref[...]
    a = jnp.dot(x, w1_ref[0], preferred_element_type=F32)
    b = jnp.dot(x, w3_ref[0], preferred_element_type=F32)
    hmid = (a * jax.nn.sigmoid(a)) * b
    y = jnp.dot(hmid.astype(BF16), w2_ref[0], preferred_element_type=F32)
    o_ref[...] = y * rw_ref[...]


def grouped_experts_blocks(xb, row_w, blk_e, w1, w3, w2):
    rows, d = xb.shape
    n_blk = rows // MOE_BLOCK
    grid_spec = pltpu.PrefetchScalarGridSpec(
        num_scalar_prefetch=1,
        grid=(n_blk,),
        in_specs=[pl.BlockSpec((MOE_BLOCK, d), lambda b, be: (b, 0)),
                  pl.BlockSpec((MOE_BLOCK, 1), lambda b, be: (b, 0)),
                  pl.BlockSpec((1, d, MOE_DFF), lambda b, be: (be[b], 0, 0)),
                  pl.BlockSpec((1, d, MOE_DFF), lambda b, be: (be[b], 0, 0)),
                  pl.BlockSpec((1, MOE_DFF, d), lambda b, be: (be[b], 0, 0))],
        out_specs=pl.BlockSpec((MOE_BLOCK, d), lambda b, be: (b, 0)))
    return pl.pallas_call(
        _expert_kernel,
        grid_spec=grid_spec,
        out_shape=jax.ShapeDtypeStruct((rows, d), F32),
        compiler_params=_cparams(("arbitrary",)),
        name="moe_experts",
    )(blk_e, xb, row_w.reshape(rows, 1), w1, w3, w2)


def hier_moe_residual(x, norm_g, w_router, w1, w3, w2):
    m, d = x.shape
    h, logits = norm_router(x, norm_g, w_router)
    g_logit = logits[:, :MOE_GROUPS]
    _, g_idx = lax.top_k(g_logit, 1)
    g_w = jnp.take_along_axis(jax.nn.softmax(g_logit, axis=-1), g_idx, axis=-1)
    e_logit = logits[:, MOE_GROUPS:MOE_GROUPS + MOE_EXPERTS].reshape(-1, MOE_GROUPS, MOE_EPG)
    e_logit = jnp.take_along_axis(e_logit, g_idx[:, :, None], axis=1)[:, 0]
    e_val, e_idx = lax.top_k(e_logit, MOE_TOPK)
    ew = jax.nn.softmax(e_val, axis=-1) * g_w
    eid = g_idx * MOE_EPG + e_idx

    n_assign = m * MOE_TOPK
    fe = eid.reshape(-1).astype(jnp.int32)
    order = jnp.argsort(fe)
    se = fe[order]
    stok = (jnp.arange(n_assign, dtype=jnp.int32) // MOE_TOPK)[order]
    sw = ew.reshape(-1)[order]
    counts = jax.ops.segment_sum(jnp.ones_like(fe), fe, num_segments=MOE_EXPERTS)
    padded = (counts + MOE_BLOCK - 1) // MOE_BLOCK * MOE_BLOCK
    ends = jnp.cumsum(padded)
    dest = (ends - padded)[se] + jnp.arange(n_assign, dtype=jnp.int32) - (jnp.cumsum(counts) - counts)[se]
    n_blk = -(-(n_assign + MOE_EXPERTS * (MOE_BLOCK - 1)) // MOE_BLOCK)
    rows = n_blk * MOE_BLOCK
    row_tok = jnp.full((rows,), m, jnp.int32).at[dest].set(stok)
    row_w = jnp.zeros((rows,), F32).at[dest].set(sw)
    blk_e = jnp.minimum(jnp.searchsorted(ends, jnp.arange(n_blk, dtype=jnp.int32) * MOE_BLOCK, side='right'),
                        MOE_EXPERTS - 1).astype(jnp.int32)
    xb = jnp.concatenate([h, jnp.zeros((1, d), h.dtype)], axis=0)[row_tok]
    yb = grouped_experts_blocks(xb, row_w, blk_e, w1, w3, w2)
    slot = jnp.zeros((n_assign,), jnp.int32).at[order].set(dest).reshape(m, MOE_TOPK)
    y = yb[slot[:, 0]]
    for k in range(1, MOE_TOPK):
        y = y + yb[slot[:, k]]
    return x + y


def _rmsnorm(x, g):
    xf = x.astype(F32)
    y = xf * lax.rsqrt(jnp.mean(xf * xf, axis=-1, keepdims=True) + NORM_EPS)
    return (y * g.astype(F32)).astype(x.dtype)


def _l2norm(x):
    xf = x.astype(F32)
    return (xf * lax.rsqrt(jnp.sum(xf * xf, axis=-1, keepdims=True) + NORM_EPS)).astype(x.dtype)


def _rope(x, pos, rot):
    half = rot // 2
    inv = ROPE_THETA ** (-(jnp.arange(half, dtype=F32) * 2.0 / rot))
    ang = pos.astype(F32)[:, None] * inv[None, :]
    cos = jnp.cos(ang)[:, None, :]
    sin = jnp.sin(ang)[:, None, :]
    xf = x.astype(F32)
    x1 = xf[..., :half]
    x2 = xf[..., half:rot]
    out = jnp.concatenate([x1 * cos - x2 * sin, x2 * cos + x1 * sin, xf[..., rot:]], axis=-1)
    return out.astype(x.dtype)


def _masked_softmax(s, mask, axis):
    s = jnp.where(mask, s, NEG)
    m = jnp.max(s, axis=axis, keepdims=True)
    p = jnp.where(mask, jnp.exp(s - m), 0.0)
    return p / jnp.maximum(jnp.sum(p, axis=axis, keepdims=True), 1e-30)


def _gdn_chunked(q, k, v, g, beta, s0, chunk):
    n, t, h, dk = q.shape
    dv = v.shape[-1]
    nc = t // chunk

    def blk(a):
        a = a.astype(F32).reshape((n, nc, chunk, h) + a.shape[3:])
        return jnp.moveaxis(jnp.moveaxis(a, 1, 0), 2, 3)

    q, k, v, g, beta = blk(q), blk(k), blk(v), blk(g), blk(beta)
    gc = jnp.cumsum(g, axis=-1)
    incl = jnp.tril(jnp.ones((chunk, chunk), bool))
    strict = jnp.tril(jnp.ones((chunk, chunk), bool), -1)
    diff = gc[..., :, None] - gc[..., None, :]
    decay = jnp.where(incl, jnp.exp(jnp.where(incl, diff, 0.0)), 0.0)
    kb = k * beta[..., None]
    a_kk = jnp.where(strict, jnp.einsum('...td,...sd->...ts', kb, k) * decay, 0.0)
    eye = jnp.eye(chunk, dtype=F32)
    rhs = jnp.concatenate([v * beta[..., None], kb * jnp.exp(gc)[..., None]], axis=-1)
    sol = lax.linalg.triangular_solve(eye + a_kk, rhs, left_side=True, lower=True, unit_diagonal=True)
    u, w = sol[..., :dv], sol[..., dv:]
    a_qk = jnp.einsum('...td,...sd->...ts', q, k) * decay

    def step(s, inp):
        qi, ki, ui, wi, gi, ai = inp
        v_new = ui - jnp.einsum('nhck,nhkv->nhcv', wi, s)
        o = (jnp.einsum('nhck,nhkv->nhcv', qi * jnp.exp(gi)[..., None], s)
             + jnp.einsum('nhts,nhsv->nhtv', ai, v_new))
        g_last = gi[..., -1:]
        s = (s * jnp.exp(g_last)[..., None]
             + jnp.einsum('nhck,nhcv->nhkv', ki * jnp.exp(g_last - gi)[..., None], v_new))
        return s, o

    s_fin, o = lax.scan(step, s0.astype(F32), (q, k, u, w, gc, a_qk))
    o = jnp.moveaxis(jnp.moveaxis(o, 3, 2), 0, 1).reshape(n, t, h, dv)
    return o, s_fin


def _gdn_mixer(qkv_raw, z, b_lin, a_lin, conv_hist, s0, conv_w, a_log, dt_bias, out_g):
    n, t, _ = qkv_raw.shape
    xh = jnp.concatenate([conv_hist, qkv_raw], axis=1)
    qkv = jax.nn.silu(sum(xh[:, i:i + t] * conv_w[i] for i in range(GDN_CONV)))
    q, k, v = jnp.split(qkv, [GDN_QK, 2 * GDN_QK], axis=-1)
    q = _l2norm(q.reshape(n, t, GDN_HEADS, GDN_DK)) * (GDN_DK ** -0.5)
    k = _l2norm(k.reshape(n, t, GDN_HEADS, GDN_DK))
    v = v.reshape(n, t, GDN_HEADS, GDN_DV)
    beta = jax.nn.sigmoid(b_lin.astype(F32))
    g = -jnp.exp(a_log.astype(F32)) * jax.nn.softplus(a_lin.astype(F32) + dt_bias.astype(F32))
    chunk = GDN_CHUNK if t % GDN_CHUNK == 0 else t
    o, s_new = _gdn_chunked(q, k, v, g, beta, s0, chunk)
    o = _rmsnorm(o, out_g).astype(F32) * jax.nn.silu(z.reshape(n, t, GDN_HEADS, GDN_DV).astype(F32))
    return o.reshape(n, t, GDN_VW).astype(qkv_raw.dtype), s_new.astype(s0.dtype), xh[:, t:]


def _mla_project(cq_raw, ckv_raw, kr_raw, pos, cq_g, ckv_g, w_uq_bf16, qn_g, qr_g, kr_g, w_uk):
    n, t, _ = cq_raw.shape
    q = norm_matmul(cq_raw.reshape(n * t, MLA_Q_LORA), cq_g, w_uq_bf16, tn=w_uq_bf16.shape[1])
    q = q.reshape(n, t, MLA_HEADS, MLA_NOPE + MLA_ROPE)
    q_nope = _rmsnorm(q[..., :MLA_NOPE], qn_g)
    q_rope = _rope(_rmsnorm(q[..., MLA_NOPE:], qr_g), pos, MLA_ROPE)
    ckv = _rmsnorm(ckv_raw, ckv_g)
    krope = _rope(_rmsnorm(kr_raw, kr_g)[:, :, None, :], pos, MLA_ROPE)[:, :, 0, :]
    q_lat = jnp.einsum('nthd,hcd->nthc', q_nope, w_uk)
    return q_lat, q_rope, ckv, krope


def _mla_attend(q_lat, q_rope, qpos, ckv, krope, kpos, w_uv):
    n, t = q_lat.shape[:2]
    s = (jnp.einsum('nthc,nsc->nhts', q_lat, ckv)
         + jnp.einsum('nthr,nsr->nhts', q_rope, krope)).astype(F32) * MLA_SCALE
    p = _masked_softmax(s, (kpos[None, :] <= qpos[:, None])[None, None], -1)
    lat = jnp.einsum('nhts,nsc->nthc', p.astype(ckv.dtype), ckv)
    return jnp.einsum('nthc,hcv->nthv', lat, w_uv).reshape(n, t, MLA_HEADS * MLA_V)


def _nsa_kv_rows(raw, kn_g, pos):
    n, t, _ = raw.shape
    kv = raw.reshape(n, t, 2, NSA_GROUPS, NSA_DH)
    k = _rope(_rmsnorm(kv[:, :, 0], kn_g), pos, NSA_ROT)
    return jnp.stack([k, kv[:, :, 1]], axis=2)


def _nsa_compress(rows, wpos, proj):
    n, length = rows.shape[:2]
    nc = length // NSA_CMP_BLOCK
    blk = rows[:, :nc * NSA_CMP_BLOCK].reshape(n, nc, NSA_CMP_BLOCK, 2, NSA_GROUPS, NSA_DH)
    pooled = jnp.einsum('ncikgd,kid->nckgd', blk, wpos)
    return jnp.einsum('nckgd,kde->nckge', pooled, proj)


def _nsa_core(q, qpos, kvc, gather_sel, n_sel, kvw, wpos, gates):
    n, tq = q.shape[:2]
    qg = q.reshape(n, tq, NSA_GROUPS, NSA_HPG, NSA_DH)
    nc = kvc.shape[1]
    s_c = jnp.einsum('ntgpd,ncgd->ntgpc', qg, kvc[:, :, 0]).astype(F32) * NSA_SCALE
    c_end = jnp.arange(nc, dtype=jnp.int32) * NSA_CMP_BLOCK + (NSA_CMP_BLOCK - 1)
    p_c = _masked_softmax(s_c, (c_end[None, :] <= qpos[:, None])[None, :, None, None, :], -1)
    o_cmp = jnp.einsum('ntgpc,ncgd->ntgpd', p_c.astype(kvc.dtype), kvc[:, :, 1])
    ratio = NSA_SEL_BLOCK // NSA_CMP_BLOCK
    imp = jnp.pad(p_c.sum(axis=3), ((0, 0), (0, 0), (0, 0), (0, ratio * n_sel - nc)))
    imp = imp.reshape(n, tq, NSA_GROUPS, n_sel, ratio).sum(-1)
    cur = (qpos // NSA_SEL_BLOCK)[:, None]
    bidx = jnp.arange(n_sel, dtype=jnp.int32)[None, :]
    forced = ((bidx == 0) | (bidx == cur) | (bidx == cur - 1)).astype(F32)
    score = jnp.where((bidx > cur)[None, :, None, :], -1e9, imp + NSA_FORCE * forced[None, :, None, :])
    _, idx = lax.top_k(score, min(NSA_TOPK, n_sel))
    kvb = gather_sel(idx)
    kpos = idx[..., None] * NSA_SEL_BLOCK + jnp.arange(NSA_SEL_BLOCK, dtype=jnp.int32)
    s_s = jnp.einsum('ntgpd,ntgksd->ntgpks', qg, kvb[..., 0, :]).astype(F32) * NSA_SCALE
    p_s = _masked_softmax(s_s, (kpos <= qpos[None, :, None, None, None])[:, :, :, None], (-2, -1))
    o_sel = jnp.einsum('ntgpks,ntgksd->ntgpd', p_s.astype(kvb.dtype), kvb[..., 1, :])
    s_w = jnp.einsum('ntgpd,nsgd->ntgps', qg, kvw[:, :, 0]).astype(F32) * NSA_SCALE
    dpos = qpos[:, None] - wpos[None, :]
    mask_w = (dpos >= 0) & (dpos < NSA_WINDOW) & (wpos[None, :] >= 0)
    p_w = _masked_softmax(s_w, mask_w[None, :, None, None, :], -1)
    o_win = jnp.einsum('ntgps,nsgd->ntgpd', p_w.astype(kvw.dtype), kvw[:, :, 1])
    gt = jax.nn.sigmoid(gates.astype(F32)).reshape(n, tq, NSA_GROUPS, NSA_HPG, 3)
    o = gt[..., 0:1] * o_cmp + gt[..., 1:2] * o_sel + gt[..., 2:3] * o_win
    return o.reshape(n, tq, NSA_HEADS * NSA_DH).astype(q.dtype)


def _heads_first(a):
    return jnp.transpose(a, (0, 2, 1, 3))


def kernel(x_prompt, x_sample, cache_mla_ckv, cache_mla_krope, cache_nsa_cmp, cache_nsa_sel, cache_nsa_win, state_gdn_ssm, state_gdn_conv, page_table, norm1_g, w_in, gdn_conv_w, gdn_a_log, gdn_dt_bias, gdn_out_g, mla_cq_g, mla_ckv_g, mla_w_uq, mla_qn_g, mla_qr_g, mla_kr_g, mla_w_uk, mla_w_uv, nsa_qn_g, nsa_kn_g, nsa_cmp_wpos, nsa_cmp_proj, p_gdn, p_mla, p_nsa, w_out, norm2_g, moe_w_group, moe_w_expert, moe_w1, moe_w3, moe_w2):
    depth = w_in.shape[0]
    b_p, t_p, d = x_prompt.shape
    b_s, t_s, _ = x_sample.shape
    page = cache_mla_ckv.shape[2]
    past = page_table.shape[1] * page
    win_s = cache_nsa_win.shape[2]
    pos_p = jnp.arange(t_p, dtype=jnp.int32)
    pos_s = past + jnp.arange(t_s, dtype=jnp.int32)
    offsets = [sum(IN_SPLITS[:i + 1]) for i in range(len(IN_SPLITS) - 1)]
    g_i = jnp.arange(NSA_GROUPS)[None, None, :, None]

    def layer_weights(l):
        w = {}
        w['w_in'] = jnp.pad(w_in[l], ((0, 0), (0, IN_COLS_PAD - IN_COLS))).astype(BF16)
        w['w_uq'] = mla_w_uq[l].astype(BF16)
        w['w_uv'] = mla_w_uv[l].astype(BF16)
        w['p_gdn'] = p_gdn[l].astype(BF16)
        w['p_mla'] = p_mla[l].astype(BF16)
        w['p_nsa'] = p_nsa[l].astype(BF16)
        w['w_out'] = w_out[l].astype(BF16)
        w['w_router'] = jnp.pad(jnp.concatenate([moe_w_group[l], moe_w_expert[l]], axis=1),
                                ((0, 0), (0, ROUTER_PAD - MOE_GROUPS - MOE_EXPERTS)))
        w['w1'] = moe_w1[l].astype(BF16)
        w['w3'] = moe_w3[l].astype(BF16)
        w['w2'] = moe_w2[l].astype(BF16)
        return w

    def shared(x, pos, l, w, conv_hist, s0):
        n, t, _ = x.shape
        proj = norm_matmul(x.reshape(n * t, d), norm1_g[l], w['w_in'])[:, :IN_COLS].reshape(n, t, IN_COLS)
        (qkv_raw, z, b_lin, a_lin, cq_raw, ckv_raw, kr_raw, nq_raw, ncmp_raw, nsel_raw, nwin_raw,
         ngate, br) = jnp.split(proj, offsets, axis=-1)
        o_a, ssm, conv = _gdn_mixer(qkv_raw, z, b_lin, a_lin, conv_hist, s0, gdn_conv_w[l],
                                    gdn_a_log[l], gdn_dt_bias[l], gdn_out_g[l])
        mla = _mla_project(cq_raw, ckv_raw, kr_raw, pos, mla_cq_g[l], mla_ckv_g[l], w['w_uq'],
                           mla_qn_g[l], mla_qr_g[l], mla_kr_g[l], mla_w_uk[l])
        q = _rope(_rmsnorm(nq_raw.reshape(n, t, NSA_HEADS, NSA_DH), nsa_qn_g[l]), pos, NSA_ROT)
        kv_c = _nsa_kv_rows(ncmp_raw, nsa_kn_g[l, 0], pos)
        kv_s = _nsa_kv_rows(nsel_raw, nsa_kn_g[l, 1], pos)
        kv_w = _nsa_kv_rows(nwin_raw, nsa_kn_g[l, 2], pos)
        return o_a, ssm, conv, mla, q, kv_c, kv_s, kv_w, ngate, br

    def finish(x, o_a, o_b, o_c, br, l, w):
        n, t, _ = x.shape
        m = n * t
        y = merge(x.reshape(m, d), o_a.reshape(m, -1), o_b.reshape(m, -1), o_c.reshape(m, -1),
                  br.reshape(m, -1), w['p_gdn'], w['p_mla'], w['p_nsa'], w['w_out'])
        y = hier_moe_residual(y, norm2_g[l], w['w_router'], w['w1'], w['w3'], w['w2'])
        return y.reshape(n, t, d)

    def prompt_layer(x, l, w):
        o_a, ssm, conv, mla, q, kv_c, kv_s, kv_w, ngate, br = shared(
            x, pos_p, l, w, jnp.zeros((b_p, GDN_CONV - 1, GDN_CONV_DIM), x.dtype),
            jnp.zeros((b_p, GDN_HEADS, GDN_DK, GDN_DV), x.dtype))
        q_lat, q_rope, ckv, krope = mla
        pad_q = jnp.zeros(q_lat.shape[:3] + (MLA_QK_PAD - MLA_KV_LORA - MLA_ROPE,), F32)
        q_cat = _heads_first((jnp.concatenate([q_lat, q_rope, pad_q], axis=-1) * MLA_SCALE).astype(BF16))
        pad_k = jnp.zeros(ckv.shape[:2] + (MLA_QK_PAD - MLA_KV_LORA - MLA_ROPE,), F32)
        k_cat = jnp.concatenate([ckv, krope, pad_k], axis=-1).astype(BF16)
        o_b = mla_flash(q_cat, k_cat, w['w_uv'])

        kvc = _nsa_compress(kv_c, nsa_cmp_wpos[l], nsa_cmp_proj[l])
        nc = kvc.shape[1]
        perm = jnp.concatenate([jnp.arange(0, nc, 2), jnp.arange(1, nc, 2)])
        kvc_p = jnp.transpose(kvc[:, perm], (0, 2, 3, 1, 4)).astype(BF16)
        kvs_t = jnp.transpose(kv_s, (0, 2, 3, 1, 4)).astype(BF16)
        kvw_t = jnp.transpose(kv_w, (0, 2, 3, 1, 4)).astype(BF16)
        q8 = _heads_first((q * NSA_SCALE).astype(BF16))
        o_cmp, o_sel, o_win = nsa_prompt(q8, kvc_p[:, 0], kvc_p[:, 1], kvs_t[:, 0], kvs_t[:, 1],
                                         kvw_t[:, 0], kvw_t[:, 1])
        gt = _heads_first(jax.nn.sigmoid(ngate.astype(F32)).reshape(b_p, t_p, NSA_HEADS, 3))
        o_c = gt[..., 0:1] * o_cmp + gt[..., 1:2] * o_sel + gt[..., 2:3] * o_win
        o_c = _heads_first(o_c).reshape(b_p, t_p, NSA_HEADS * NSA_DH)
        y = finish(x, o_a, o_b, o_c, br, l, w)
        return y, (ssm, conv, ckv, krope, kv_c, kv_s, kv_w[:, t_p - min(NSA_WINDOW, t_p):])

    def sample_layer(x, l, w):
        o_a, ssm, conv, mla, q, kv_c, kv_s, kv_w, ngate, br = shared(
            x, pos_s, l, w, state_gdn_conv[l], state_gdn_ssm[l])
        q_lat, q_rope, ckv, krope = mla
        ckv_all = jnp.concatenate([cache_mla_ckv[l, page_table].reshape(b_s, past, MLA_KV_LORA), ckv], axis=1)
        kr_all = jnp.concatenate([cache_mla_krope[l, page_table].reshape(b_s, past, MLA_ROPE), krope], axis=1)
        o_b = _mla_attend(q_lat, q_rope, pos_s, ckv_all, kr_all,
                          jnp.arange(past + t_s, dtype=jnp.int32), mla_w_uv[l])
        kvc_rows = jnp.concatenate(
            [cache_nsa_cmp[l, page_table].reshape(b_s, past, 2, NSA_GROUPS, NSA_DH), kv_c], axis=1)
        kvc = _nsa_compress(kvc_rows, nsa_cmp_wpos[l], nsa_cmp_proj[l])
        n_past_blk = past // NSA_SEL_BLOCK
        n_sel = -(-(past + t_s) // NSA_SEL_BLOCK)
        n_new_blk = n_sel - n_past_blk
        bpp = page // NSA_SEL_BLOCK
        new_blocks = jnp.pad(kv_s, ((0, 0), (0, n_new_blk * NSA_SEL_BLOCK - t_s), (0, 0), (0, 0), (0, 0)))
        new_blocks = new_blocks.reshape(b_s, n_new_blk, NSA_SEL_BLOCK, 2, NSA_GROUPS, NSA_DH)
        b_i = jnp.arange(b_s)[:, None, None, None]

        def gather_sel(idx):
            pidx = jnp.minimum(idx, n_past_blk - 1)
            phys = page_table[b_i, pidx // bpp]
            rows = (pidx % bpp)[..., None] * NSA_SEL_BLOCK + jnp.arange(NSA_SEL_BLOCK)
            from_pool = cache_nsa_sel[l, phys[..., None], rows, :, g_i[..., None]]
            from_new = new_blocks[b_i, jnp.clip(idx - n_past_blk, 0, n_new_blk - 1), :, :, g_i]
            return jnp.where((idx < n_past_blk)[..., None, None, None], from_pool, from_new)

        kvw = jnp.concatenate([cache_nsa_win[l], kv_w], axis=1)
        wpos = past - win_s + jnp.arange(win_s + t_s, dtype=jnp.int32)
        o_c = _nsa_core(q, pos_s, kvc, gather_sel, n_sel, kvw, wpos, ngate)
        y = finish(x, o_a, o_b, o_c, br, l, w)
        return y, (ssm, conv, ckv, krope, kv_c, kv_s, kvw[:, t_s:])

    xp, xs = x_prompt, x_sample
    p_states, s_states = [], []
    for l in range(depth):
        w = layer_weights(l)
        xp, st = prompt_layer(xp, l, w)
        p_states.append(st)
        xs, st = sample_layer(xs, l, w)
        s_states.append(st)
    p_out = [jnp.stack([st[i] for st in p_states], axis=0) for i in range(7)]
    s_out = [jnp.stack([st[i] for st in s_states], axis=0) for i in range(7)]
    return (xp, xs, *p_out, *s_out)
```

```python
import functools
import math

import numpy as np
import jax
import jax.numpy as jnp
from jax import lax
from jax.experimental import pallas as pl
from jax.experimental.pallas import tpu as pltpu

F32 = jnp.float32
BF16 = jnp.bfloat16
HIGHEST = lax.Precision.HIGHEST

D_MODEL = 1024
ROPE_THETA = 500000.0
NORM_EPS = 1e-6
NEG = -1e30
LANES = 128

GDN_HEADS = 4
GDN_DK = 128
GDN_DV = 128
GDN_CONV = 4
GDN_CHUNK = 64
GDN_QK = GDN_HEADS * GDN_DK
GDN_VW = GDN_HEADS * GDN_DV
GDN_CONV_DIM = 2 * GDN_QK + GDN_VW

MLA_HEADS = 8
MLA_Q_LORA = 384
MLA_KV_LORA = 256
MLA_NOPE = 64
MLA_ROPE = 32
MLA_HALF = MLA_ROPE // 2
MLA_V = 64
MLA_SCALE = (MLA_NOPE + MLA_ROPE) ** -0.5
MLA_QK = MLA_KV_LORA + 2 * LANES

NSA_HEADS = 8
NSA_GROUPS = 2
NSA_HPG = NSA_HEADS // NSA_GROUPS
NSA_DH = 64
NSA_ROT = NSA_DH // 4
NSA_CMP_BLOCK = 32
NSA_SEL_BLOCK = 64
NSA_TOPK = 16
NSA_WINDOW = 512
NSA_FORCE = 1000.0
NSA_SCALE = NSA_DH ** -0.5
NSA_KV_COLS = 2 * NSA_GROUPS * NSA_DH
NSA_QW = NSA_HEADS * NSA_DH

MOE_GROUPS = 4
MOE_EPG = 8
MOE_EXPERTS = MOE_GROUPS * MOE_EPG
MOE_TOPK = 2
MOE_DFF = 512
MOE_BLOCK = 128
ROUTER_PAD = LANES

IN_SPLITS = (GDN_CONV_DIM, GDN_VW, GDN_HEADS, GDN_HEADS,
             MLA_Q_LORA, MLA_KV_LORA, MLA_ROPE,
             NSA_HEADS * NSA_DH, NSA_KV_COLS, NSA_KV_COLS, NSA_KV_COLS, 3 * NSA_HEADS,
             3 * D_MODEL)
IN_OFF = [0]
for _w in IN_SPLITS:
    IN_OFF.append(IN_OFF[-1] + _w)

GDN_COLS = GDN_CONV_DIM + GDN_VW + LANES
MLA_COLS = MLA_Q_LORA + MLA_KV_LORA + 2 * LANES + LANES
NSA_COLS = NSA_QW + 3 * NSA_KV_COLS + LANES

VMEM_LIMIT = 52 * 1024 * 1024

_NT = (((1,), (1,)), ((), ()))
_TN = (((0,), (0,)), ((), ()))


def _cparams(sem):
    return pltpu.CompilerParams(dimension_semantics=sem, vmem_limit_bytes=VMEM_LIMIT)


def _block_diag_ones(n, grp):
    i = np.arange(n)
    return jnp.asarray((i[:, None] // grp) == (i[None, :] // grp), BF16)


def _group_sum(sq, bd):
    hi = sq.astype(BF16)
    lo = (sq - hi.astype(F32)).astype(BF16)
    return jnp.dot(hi, bd, preferred_element_type=F32) + jnp.dot(lo, bd, preferred_element_type=F32)


def _softmax_rows(s, mask):
    s = jnp.where(mask, s, NEG)
    m = jnp.max(s, axis=-1, keepdims=True)
    p = jnp.where(mask, jnp.exp(s - m), 0.0)
    return p / jnp.maximum(jnp.sum(p, axis=-1, keepdims=True), 1e-30)


def _silu(x):
    return x * jax.nn.sigmoid(x)


def _norm_matmul_kernel(x_ref, g_ref, w_ref, o_ref, h_ref):
    @pl.when(pl.program_id(1) == 0)
    def _():
        x = x_ref[...]
        ms = jnp.mean(x * x, axis=-1, keepdims=True)
        h_ref[...] = (x * lax.rsqrt(ms + NORM_EPS) * g_ref[...]).astype(BF16)

    o_ref[...] = jnp.dot(h_ref[...], w_ref[...], preferred_element_type=F32)


def norm_matmul(x, g, w, *, tm=512, tn=None):
    m, k = x.shape
    n = w.shape[1]
    tm = min(tm, m)
    tn = n if tn is None else tn
    assert m % tm == 0 and n % tn == 0
    return pl.pallas_call(
        _norm_matmul_kernel,
        grid=(m // tm, n // tn),
        in_specs=[pl.BlockSpec((tm, k), lambda i, j: (i, 0)),
                  pl.BlockSpec((1, k), lambda i, j: (0, 0)),
                  pl.BlockSpec((k, tn), lambda i, j: (0, j))],
        out_specs=pl.BlockSpec((tm, tn), lambda i, j: (i, j)),
        out_shape=jax.ShapeDtypeStruct((m, n), F32),
        scratch_shapes=[pltpu.VMEM((tm, k), BF16)],
        compiler_params=_cparams(("parallel", "arbitrary")),
        name="norm_matmul",
    )(x, g.reshape(1, k), w)


def _gdn_kernel(p_ref, hist_ref, s0_ref, cw_ref, alog_ref, dtb_ref, og_ref, o_ref, sfin_ref,
                prev_ref, s_ref, *, chunk, t_valid):
    c = pl.program_id(1)
    n_steps = pl.num_programs(1)
    qk, vw = GDN_QK, GDN_VW

    @pl.when(c == 0)
    def _():
        prev_ref[...] = hist_ref[0]
        s_ref[...] = s0_ref[0]

    raw = p_ref[0]
    xcat = jnp.concatenate([prev_ref[...], raw[:, :GDN_CONV_DIM]], axis=0)
    prev_ref[...] = xcat[chunk:chunk + 8]
    acc = xcat[8:8 + chunk] * cw_ref[GDN_CONV - 1:GDN_CONV, :]
    for i in range(GDN_CONV - 1):
        sh = pltpu.roll(xcat, GDN_CONV - 1 - i, 0)
        acc = acc + sh[8:8 + chunk] * cw_ref[i:i + 1, :]
    qkv = _silu(acc)

    ba = raw[:, GDN_CONV_DIM + GDN_VW:]
    beta_all = jax.nn.sigmoid(ba)
    xa = ba + dtb_ref[...]
    softplus = jnp.maximum(xa, 0.0) + jnp.log(1.0 + jnp.exp(-jnp.abs(xa)))
    g_all = -jnp.exp(alog_ref[...]) * softplus
    row = lax.broadcasted_iota(jnp.int32, (chunk, chunk), 0)
    col = lax.broadcasted_iota(jnp.int32, (chunk, chunk), 1)
    if t_valid < chunk:
        live = lax.broadcasted_iota(jnp.int32, (chunk, 1), 0) < t_valid
        beta_all = jnp.where(live, beta_all, 0.0)
        g_all = jnp.where(live, g_all, 0.0)
    incl = row >= col
    strict = row > col
    eye = row == col
    gc_all = jnp.dot(incl.astype(F32), g_all, preferred_element_type=F32, precision=HIGHEST)
    ones = jnp.ones((chunk, chunk), F32)
    n_fac = int(math.log2(chunk)) - 1

    for h in range(GDN_HEADS):
        q = qkv[:, h * GDN_DK:(h + 1) * GDN_DK]
        k = qkv[:, qk + h * GDN_DK:qk + (h + 1) * GDN_DK]
        v = qkv[:, 2 * qk + h * GDN_DV:2 * qk + (h + 1) * GDN_DV]
        q = q * lax.rsqrt(jnp.sum(q * q, axis=-1, keepdims=True) + NORM_EPS) * (GDN_DK ** -0.5)
        k = k * lax.rsqrt(jnp.sum(k * k, axis=-1, keepdims=True) + NORM_EPS)
        beta = beta_all[:, h:h + 1]
        gcol = gc_all[:, GDN_HEADS + h:GDN_HEADS + h + 1]
        gd = jnp.broadcast_to(gcol, (chunk, chunk))
        grow = jnp.dot(ones, jnp.where(eye, gd, 0.0), preferred_element_type=F32, precision=HIGHEST)
        decay = jnp.where(incl, jnp.exp(jnp.where(incl, gd - grow, 0.0)), 0.0)
        kb = k * beta
        kbf = k.astype(BF16)
        a_kk = jnp.where(strict, lax.dot_general(kb.astype(BF16), kbf, _NT, preferred_element_type=F32) * decay, 0.0)
        inv = jnp.where(eye, 1.0, 0.0) - a_kk
        pw = a_kk
        for _ in range(n_fac):
            pw = jnp.dot(pw, pw, preferred_element_type=F32, precision=HIGHEST)
            inv = inv + jnp.dot(inv, pw, preferred_element_type=F32, precision=HIGHEST)
        egc = jnp.exp(gcol)
        rhs = jnp.concatenate([v * beta, kb * egc], axis=1)
        sol = jnp.dot(inv, rhs, preferred_element_type=F32, precision=HIGHEST)
        u, w = sol[:, :GDN_DV], sol[:, GDN_DV:]
        a_qk = lax.dot_general(q.astype(BF16), kbf, _NT, preferred_element_type=F32) * decay
        s = s_ref[h]
        sb = s.astype(BF16)
        v_new = u - jnp.dot(w.astype(BF16), sb, preferred_element_type=F32)
        o = (jnp.dot((q * egc).astype(BF16), sb, preferred_element_type=F32)
             + jnp.dot(a_qk.astype(BF16), v_new.astype(BF16), preferred_element_type=F32))
        g_last = gcol[chunk - 1:chunk, :]
        kdec = k * jnp.exp(g_last - gcol)
        s_ref[h] = s * jnp.exp(g_last) + lax.dot_general(kdec.astype(BF16), v_new.astype(BF16), _TN,
                                                         preferred_element_type=F32)
        on = o * lax.rsqrt(jnp.mean(o * o, axis=-1, keepdims=True) + NORM_EPS) * og_ref[...]
        z = raw[:, GDN_CONV_DIM + h * GDN_DV:GDN_CONV_DIM + (h + 1) * GDN_DV]
        o_ref[0, :, h * GDN_DV:(h + 1) * GDN_DV] = on * _silu(z)

    @pl.when(c == n_steps - 1)
    def _():
        sfin_ref[0] = s_ref[...]


def gdn_mixer(p_gdn, hist8, s0, conv_w, a_log, dt_bias, out_g, *, chunk, t_valid):
    n, tp, _ = p_gdn.shape
    pad4 = jnp.zeros((GDN_HEADS,), F32)
    lane_row = lambda a: jnp.pad(jnp.concatenate([pad4, a]), (0, LANES - 2 * GDN_HEADS)).reshape(1, LANES)
    return pl.pallas_call(
        functools.partial(_gdn_kernel, chunk=chunk, t_valid=t_valid),
        grid=(n, tp // chunk),
        in_specs=[pl.BlockSpec((1, chunk, GDN_COLS), lambda b, c: (b, c, 0)),
                  pl.BlockSpec((1, 8, GDN_CONV_DIM), lambda b, c: (b, 0, 0)),
                  pl.BlockSpec((1, GDN_HEADS, GDN_DK, GDN_DV), lambda b, c: (b, 0, 0, 0)),
                  pl.BlockSpec((GDN_CONV, GDN_CONV_DIM), lambda b, c: (0, 0)),
                  pl.BlockSpec((1, LANES), lambda b, c: (0, 0)),
                  pl.BlockSpec((1, LANES), lambda b, c: (0, 0)),
                  pl.BlockSpec((1, GDN_DV), lambda b, c: (0, 0))],
        out_specs=[pl.BlockSpec((1, chunk, GDN_VW), lambda b, c: (b, c, 0)),
                   pl.BlockSpec((1, GDN_HEADS, GDN_DK, GDN_DV), lambda b, c: (b, 0, 0, 0))],
        out_shape=[jax.ShapeDtypeStruct((n, tp, GDN_VW), F32),
                   jax.ShapeDtypeStruct((n, GDN_HEADS, GDN_DK, GDN_DV), F32)],
        scratch_shapes=[pltpu.VMEM((8, GDN_CONV_DIM), F32), pltpu.VMEM((GDN_HEADS, GDN_DK, GDN_DV), F32)],
        compiler_params=_cparams(("parallel", "arbitrary")),
        name="gdn_mixer",
    )(p_gdn, hist8, s0, conv_w, lane_row(a_log), lane_row(dt_bias), out_g.reshape(1, GDN_DV))


def _mla_prep_kernel(p_ref, cos_ref, sin_ref, cqg_ref, ckvg_ref, wuq_ref, qng_ref, qrg_ref, krg_ref, wuk_ref,
                     bd64_ref, bd16_ref, qcat_ref, kcat_ref, ckv_ref, krope_ref):
    p = p_ref[...]
    cos = cos_ref[...]
    sin = sin_ref[...]
    lane = lax.broadcasted_iota(jnp.int32, (1, LANES), 1)

    cq = p[:, :MLA_Q_LORA]
    cq = cq * lax.rsqrt(jnp.mean(cq * cq, axis=-1, keepdims=True) + NORM_EPS) * cqg_ref[...]
    q = jnp.dot(cq.astype(BF16), wuq_ref[...], preferred_element_type=F32)
    nw = MLA_HEADS * MLA_NOPE
    nope = q[:, :nw]
    nope = nope * lax.rsqrt(_group_sum(nope * nope, bd64_ref[...]) * (1.0 / MLA_NOPE) + NORM_EPS) * qng_ref[...]
    x1 = q[:, nw:nw + LANES]
    x2 = q[:, nw + LANES:]
    r = lax.rsqrt(_group_sum(x1 * x1 + x2 * x2, bd16_ref[...]) * (1.0 / MLA_ROPE) + NORM_EPS)
    x1 = x1 * r * qrg_ref[0:1, :]
    x2 = x2 * r * qrg_ref[1:2, :]
    r1 = (x1 * cos - x2 * sin) * MLA_SCALE
    r2 = (x2 * cos + x1 * sin) * MLA_SCALE
    for h in range(MLA_HEADS):
        tile = nope[:, (h // 2) * LANES:(h // 2 + 1) * LANES].astype(BF16)
        q_lat = jnp.dot(tile, wuk_ref[h], preferred_element_type=F32) * MLA_SCALE
        mine = (lane // MLA_HALF) == h
        base = h * MLA_QK
        qcat_ref[:, base:base + MLA_KV_LORA] = q_lat.astype(BF16)
        qcat_ref[:, base + MLA_KV_LORA:base + MLA_KV_LORA + LANES] = jnp.where(mine, r1, 0.0).astype(BF16)
        qcat_ref[:, base + MLA_KV_LORA + LANES:base + MLA_QK] = jnp.where(mine, r2, 0.0).astype(BF16)

    ckv = p[:, MLA_Q_LORA:MLA_Q_LORA + MLA_KV_LORA]
    ckv = ckv * lax.rsqrt(jnp.mean(ckv * ckv, axis=-1, keepdims=True) + NORM_EPS) * ckvg_ref[...]
    ckv_ref[...] = ckv
    kr = p[:, MLA_Q_LORA + MLA_KV_LORA:MLA_Q_LORA + MLA_KV_LORA + 2 * LANES]
    ms = jnp.sum(kr * kr, axis=-1, keepdims=True) * (1.0 / (MLA_HEADS * MLA_ROPE))
    rk = lax.rsqrt(ms + NORM_EPS)
    k1 = kr[:, :LANES] * rk * krg_ref[0:1, :]
    k2 = kr[:, LANES:] * rk * krg_ref[1:2, :]
    k1r = k1 * cos - k2 * sin
    k2r = k2 * cos + k1 * sin
    kcat_ref[:, :MLA_KV_LORA] = ckv.astype(BF16)
    kcat_ref[:, MLA_KV_LORA:MLA_KV_LORA + LANES] = k1r.astype(BF16)
    kcat_ref[:, MLA_KV_LORA + LANES:] = k2r.astype(BF16)
    std = jnp.where(lane < MLA_HALF, k1r, pltpu.roll(k2r, MLA_HALF, 1))
    krope_ref[...] = std[:, :MLA_ROPE]


def mla_prep(p_mla, cos, sin, wts, *, tm):
    m = p_mla.shape[0]
    nt = cos.shape[0] // tm
    row = lambda w: pl.BlockSpec((tm, w), lambda i: (i, 0))
    res = lambda a: pl.BlockSpec(a.shape, lambda i: (0,) * a.ndim)
    tab = pl.BlockSpec((tm, LANES), lambda i: (i % nt, 0))
    consts = [wts['cq_g'], wts['ckv_g'], wts['w_uq'], wts['qn_g'], wts['qr_g'], wts['kr_g'], wts['w_uk'],
              wts['bd64_512'], wts['bd16']]
    return pl.pallas_call(
        _mla_prep_kernel,
        grid=(m // tm,),
        in_specs=[row(MLA_COLS), tab, tab] + [res(a) for a in consts],
        out_specs=[row(MLA_HEADS * MLA_QK), row(MLA_QK), row(MLA_KV_LORA), row(MLA_ROPE)],
        out_shape=[jax.ShapeDtypeStruct((m, MLA_HEADS * MLA_QK), BF16), jax.ShapeDtypeStruct((m, MLA_QK), BF16),
                   jax.ShapeDtypeStruct((m, MLA_KV_LORA), F32), jax.ShapeDtypeStruct((m, MLA_ROPE), F32)],
        compiler_params=_cparams(("parallel",)),
        name="mla_prep",
    )(p_mla, cos, sin, *consts)


def _mla_flash_kernel(qi_ref, kj_ref, q_ref, k_ref, wuv_ref, o_ref, m_ref, l_ref, acc_ref, *, tq, tk):
    p = pl.program_id(1)
    qi = qi_ref[p]
    kj = kj_ref[p]
    rows = MLA_HEADS * tq

    @pl.when(kj == 0)
    def _():
        m_ref[...] = jnp.full(m_ref.shape, NEG, F32)
        l_ref[...] = jnp.zeros(l_ref.shape, F32)
        acc_ref[...] = jnp.zeros(acc_ref.shape, F32)

    q = jnp.concatenate([q_ref[:, h * MLA_QK:(h + 1) * MLA_QK] for h in range(MLA_HEADS)], axis=0)
    k = k_ref[...]
    s = lax.dot_general(q, k, _NT, preferred_element_type=F32)
    qpos = qi * tq + (lax.broadcasted_iota(jnp.int32, (rows, 1), 0) & (tq - 1))
    kpos = kj * tk + lax.broadcasted_iota(jnp.int32, (1, tk), 1)
    s = jnp.where(kpos <= qpos, s, NEG)
    m_prev = m_ref[...]
    m_new = jnp.maximum(m_prev, jnp.max(s, axis=1, keepdims=True))
    alpha = jnp.exp(m_prev - m_new)
    pexp = jnp.exp(s - m_new)
    l_ref[...] = alpha * l_ref[...] + jnp.sum(pexp, axis=1, keepdims=True)
    acc_ref[...] = alpha * acc_ref[...] + jnp.dot(pexp.astype(BF16), k[:, :MLA_KV_LORA],
                                                  preferred_element_type=F32)
    m_ref[...] = m_new

    @pl.when(kj == (qi * tq + tq - 1) // tk)
    def _():
        lat = (acc_ref[...] / l_ref[...]).astype(BF16)
        outs = [jnp.dot(lat[h * tq:(h + 1) * tq], wuv_ref[h], preferred_element_type=F32)
                for h in range(MLA_HEADS)]
        o_ref[...] = jnp.concatenate(outs, axis=-1)


def mla_flash(q_cat, k_cat, w_uv, n, *, tq=128, tk=512):
    t = q_cat.shape[0] // n
    assert t % tq == 0 and t % tk == 0 and (tq & (tq - 1)) == 0
    pairs = [(i, j) for i in range(t // tq) for j in range((i * tq + tq - 1) // tk + 1)]
    qi_tab = jnp.asarray([a for a, _ in pairs], jnp.int32)
    kj_tab = jnp.asarray([b for _, b in pairs], jnp.int32)
    rows = MLA_HEADS * tq
    nq, nk = t // tq, t // tk
    grid_spec = pltpu.PrefetchScalarGridSpec(
        num_scalar_prefetch=2,
        grid=(n, len(pairs)),
        in_specs=[pl.BlockSpec((tq, MLA_HEADS * MLA_QK), lambda b, p, qi, kj: (b * nq + qi[p], 0)),
                  pl.BlockSpec((tk, MLA_QK), lambda b, p, qi, kj: (b * nk + kj[p], 0)),
                  pl.BlockSpec((MLA_HEADS, MLA_KV_LORA, MLA_V), lambda b, p, qi, kj: (0, 0, 0))],
        out_specs=pl.BlockSpec((tq, MLA_HEADS * MLA_V), lambda b, p, qi, kj: (b * nq + qi[p], 0)),
        scratch_shapes=[pltpu.VMEM((rows, 1), F32), pltpu.VMEM((rows, 1), F32),
                        pltpu.VMEM((rows, MLA_KV_LORA), F32)])
    return pl.pallas_call(
        functools.partial(_mla_flash_kernel, tq=tq, tk=tk),
        grid_spec=grid_spec,
        out_shape=jax.ShapeDtypeStruct((n * t, MLA_HEADS * MLA_V), F32),
        compiler_params=_cparams(("parallel", "arbitrary")),
        name="mla_flash",
    )(qi_tab, kj_tab, q_cat, k_cat, w_uv)


def _mla_decode_kernel(pt_ref, ql_ref, qr_ref, cnew_ref, rnew_ref, wuv_ref, *refs, pg, t_s):
    ck_refs = refs[:pg]
    kr_refs = refs[pg:2 * pg]
    o_ref, m_ref, l_ref, acc_ref = refs[2 * pg:]
    j = pl.program_id(1)
    rows = ql_ref.shape[1]
    tp = rows // MLA_HEADS

    @pl.when(j == 0)
    def _():
        m_ref[...] = jnp.full(m_ref.shape, NEG, F32)
        l_ref[...] = jnp.zeros(l_ref.shape, F32)
        acc_ref[...] = jnp.zeros(acc_ref.shape, F32)

    ql = ql_ref[0]
    qr = qr_ref[0]
    cks, ss = [], []
    for i in range(pg):
        ck = ck_refs[i][0, 0].astype(BF16)
        kr = kr_refs[i][0, 0].astype(BF16)
        cks.append(ck)
        ss.append(lax.dot_general(ql, ck, _NT, preferred_element_type=F32)
                  + lax.dot_general(qr, kr, _NT, preferred_element_type=F32))
    s = jnp.concatenate(ss, axis=1)
    m_prev = m_ref[...]
    m_new = jnp.maximum(m_prev, jnp.max(s, axis=1, keepdims=True))
    alpha = jnp.exp(m_prev - m_new)
    pexp = jnp.exp(s - m_new)
    page = s.shape[1] // pg
    pv = jnp.dot(pexp[:, :page].astype(BF16), cks[0], preferred_element_type=F32)
    for i in range(1, pg):
        pv = pv + jnp.dot(pexp[:, i * page:(i + 1) * page].astype(BF16), cks[i], preferred_element_type=F32)
    l_ref[...] = alpha * l_ref[...] + jnp.sum(pexp, axis=1, keepdims=True)
    acc_ref[...] = alpha * acc_ref[...] + pv
    m_ref[...] = m_new

    @pl.when(j == pl.num_programs(1) - 1)
    def _():
        cn = cnew_ref[0]
        rn = rnew_ref[0]
        sn = (lax.dot_general(ql, cn, _NT, preferred_element_type=F32)
              + lax.dot_general(qr, rn, _NT, preferred_element_type=F32))
        tq = lax.broadcasted_iota(jnp.int32, (rows, 1), 0) // MLA_HEADS
        tk = lax.broadcasted_iota(jnp.int32, (1, tp), 1)
        sn = jnp.where((tk <= tq) & (tk < t_s), sn, NEG)
        m2 = jnp.maximum(m_ref[...], jnp.max(sn, axis=1, keepdims=True))
        a2 = jnp.exp(m_ref[...] - m2)
        pn = jnp.exp(sn - m2)
        l2 = a2 * l_ref[...] + jnp.sum(pn, axis=1, keepdims=True)
        acc2 = a2 * acc_ref[...] + jnp.dot(pn.astype(BF16), cn, preferred_element_type=F32)
        lat = (acc2 / l2).astype(BF16)
        full = jnp.dot(lat, wuv_ref[...], preferred_element_type=F32)
        hrow = lax.broadcasted_iota(jnp.int32, (rows, 1), 0) % MLA_HEADS
        hcol = lax.broadcasted_iota(jnp.int32, (1, MLA_HEADS * MLA_V), 1) // MLA_V
        full = jnp.where(hrow == hcol, full, 0.0)
        o_ref[0] = jnp.sum(full.reshape(tp, MLA_HEADS, MLA_HEADS * MLA_V), axis=1)


def mla_decode(page_table, q_lat, q_rope, ckv_new, kr_new, w_uv_all, cache_ckv, cache_kr, l, *, t_s, pg=8):
    b, rows, _ = q_lat.shape
    tp = rows // MLA_HEADS
    n_pages = page_table.shape[1]
    page = cache_ckv.shape[2]
    assert n_pages % pg == 0
    pt = page_table.reshape(-1)

    def page_spec(width, i):
        return pl.BlockSpec((1, 1, page, width), lambda s, j, pt_ref: (l, pt_ref[s * n_pages + j * pg + i], 0, 0))

    seq = lambda a: pl.BlockSpec((1,) + a.shape[1:], lambda s, j, pt_ref: (s, 0, 0))
    grid_spec = pltpu.PrefetchScalarGridSpec(
        num_scalar_prefetch=1,
        grid=(b, n_pages // pg),
        in_specs=[seq(q_lat), seq(q_rope), seq(ckv_new), seq(kr_new),
                  pl.BlockSpec(w_uv_all.shape, lambda s, j, pt_ref: (0, 0))]
                 + [page_spec(MLA_KV_LORA, i) for i in range(pg)]
                 + [page_spec(MLA_ROPE, i) for i in range(pg)],
        out_specs=pl.BlockSpec((1, tp, MLA_HEADS * MLA_V), lambda s, j, pt_ref: (s, 0, 0)),
        scratch_shapes=[pltpu.VMEM((rows, 1), F32), pltpu.VMEM((rows, 1), F32),
                        pltpu.VMEM((rows, MLA_KV_LORA), F32)])
    return pl.pallas_call(
        functools.partial(_mla_decode_kernel, pg=pg, t_s=t_s),
        grid_spec=grid_spec,
        out_shape=jax.ShapeDtypeStruct((b, tp, MLA_HEADS * MLA_V), F32),
        compiler_params=_cparams(("parallel", "arbitrary")),
        name="mla_decode",
    )(pt, q_lat, q_rope, ckv_new, kr_new, w_uv_all, *([cache_ckv] * pg), *([cache_kr] * pg))


def _rope_lanes(x, c, s1, s2):
    n = x.shape[1]
    half = NSA_ROT // 2
    reps = n // LANES
    tile = (lambda a: jnp.concatenate([a] * reps, axis=1)) if reps > 1 else (lambda a: a)
    return x * tile(c) + pltpu.roll(x, n - half, 1) * tile(s1) + pltpu.roll(x, half, 1) * tile(s2)


def _nsa_prep_kernel(p_ref, c_ref, s1_ref, s2_ref, qng_ref, kng_ref, bdq_ref, bdk_ref, wt_ref, pbd_ref,
                     q_ref, kvc_ref, kvs_ref, kvw_ref, gate_ref, *cmp_out, compress):
    p = p_ref[...]
    c, s1, s2 = c_ref[...], s1_ref[...], s2_ref[...]
    q = p[:, :NSA_QW]
    q = q * lax.rsqrt(_group_sum(q * q, bdq_ref[...]) * (1.0 / NSA_DH) + NORM_EPS) * qng_ref[...]
    q_ref[...] = (_rope_lanes(q, c, s1, s2) * NSA_SCALE).astype(BF16)
    outs = (kvc_ref, kvs_ref, kvw_ref)
    kc = None
    for j in range(3):
        base = NSA_QW + j * NSA_KV_COLS
        k = p[:, base:base + LANES]
        k = k * lax.rsqrt(_group_sum(k * k, bdk_ref[...]) * (1.0 / NSA_DH) + NORM_EPS) * kng_ref[j:j + 1, :]
        k = _rope_lanes(k, c, s1, s2)
        outs[j][:, :LANES] = k
        outs[j][:, LANES:] = p[:, base + LANES:base + NSA_KV_COLS]
        if j == 0:
            kc = k
    gate_ref[...] = jax.nn.sigmoid(p[:, NSA_QW + 3 * NSA_KV_COLS:])
    if compress:
        tm = p.shape[0]
        rows = jnp.concatenate([kc, p[:, NSA_QW + LANES:NSA_QW + NSA_KV_COLS]], axis=1)
        pooled = jnp.sum((rows * wt_ref[...]).reshape(tm // NSA_CMP_BLOCK, NSA_CMP_BLOCK, NSA_KV_COLS), axis=1)
        cmp_out[0][...] = jnp.dot(pooled.astype(BF16), pbd_ref[...], preferred_element_type=F32)


def nsa_prep(p_nsa, tabs, wts, *, tm, compress):
    m = p_nsa.shape[0]
    nt = tabs[0].shape[0] // tm
    row = lambda w: pl.BlockSpec((tm, w), lambda i: (i, 0))
    res = lambda a: pl.BlockSpec(a.shape, lambda i: (0,) * a.ndim)
    tab = pl.BlockSpec((tm, LANES), lambda i: (i % nt, 0))
    consts = [wts['nsa_qn_g'], wts['nsa_kn_g'], wts['bd64_512'], wts['bd64_128'], wts['cmp_w'][:tm], wts['cmp_proj']]
    out_specs = [row(NSA_QW), row(NSA_KV_COLS), row(NSA_KV_COLS), row(NSA_KV_COLS), row(LANES)]
    out_shape = [jax.ShapeDtypeStruct((m, NSA_QW), BF16)] + [jax.ShapeDtypeStruct((m, NSA_KV_COLS), F32)] * 3 \
        + [jax.ShapeDtypeStruct((m, LANES), F32)]
    if compress:
        out_specs.append(pl.BlockSpec((tm // NSA_CMP_BLOCK, NSA_KV_COLS), lambda i: (i, 0)))
        out_shape.append(jax.ShapeDtypeStruct((m // NSA_CMP_BLOCK, NSA_KV_COLS), F32))
    return pl.pallas_call(
        functools.partial(_nsa_prep_kernel, compress=compress),
        grid=(m // tm,),
        in_specs=[row(NSA_COLS), tab, tab, tab] + [res(a) for a in consts],
        out_specs=out_specs,
        out_shape=out_shape,
        compiler_params=_cparams(("parallel",)),
        name="nsa_prep",
    )(p_nsa, *tabs, *consts)


def _group_rows(q, g):
    lane = lax.broadcasted_iota(jnp.int32, (1, LANES), 1)
    in_half = (lane // NSA_DH) == g
    zero = jnp.zeros((), q.dtype)
    parts = []
    for pidx in range(NSA_HPG):
        h = g * NSA_HPG + pidx
        tile = q[:, (h // 2) * LANES:(h // 2 + 1) * LANES]
        if h % 2 != g:
            tile = jnp.concatenate([tile[:, NSA_DH:], tile[:, :NSA_DH]], axis=1)
        parts.append(jnp.where(in_half, tile, zero))
    return jnp.concatenate(parts, axis=0)


def _place_heads(o_groups, t):
    lane = lax.broadcasted_iota(jnp.int32, (1, LANES), 1)
    low = lane < NSA_DH
    tiles = []
    for j in range(NSA_HEADS // 2):
        pair = []
        for h in (2 * j, 2 * j + 1):
            g, pidx = divmod(h, NSA_HPG)
            x = o_groups[g][pidx * t:(pidx + 1) * t]
            if h % 2 != g:
                x = jnp.concatenate([x[:, NSA_DH:], x[:, :NSA_DH]], axis=1)
            pair.append(x)
        tiles.append(jnp.where(low, pair[0], pair[1]))
    return jnp.concatenate(tiles, axis=1)


def _expand_gates(gs, exp_ref, j):
    hi = gs.astype(BF16)
    lo = (gs - hi.astype(F32)).astype(BF16)
    e = exp_ref[j]
    return jnp.dot(hi, e, preferred_element_type=F32) + jnp.dot(lo, e, preferred_element_type=F32)


def _top_blocks_t(score_t, blk_row_f, n_rows):
    sel_t = jnp.zeros(score_t.shape, F32)
    for _ in range(NSA_TOPK):
        mx = jnp.max(score_t, axis=0, keepdims=True)
        first = jnp.min(jnp.where(score_t == mx, blk_row_f, float(n_rows)), axis=0, keepdims=True)
        pick = blk_row_f == first
        sel_t = jnp.where(pick, 1.0, sel_t)
        score_t = jnp.where(pick, -3e38, score_t)
    return sel_t


def _nsa_prompt_kernel(q_ref, gate_ref, kvc_ref, kvs_ref, kvw_ref, pool_ref, exp_ref, o_ref, *, tq, tk, nc, n_sel):
    start = pl.program_id(1) * tq
    grows = NSA_HPG * tq
    q = q_ref[...]
    tokpos = start + lax.broadcasted_iota(jnp.int32, (tq, 1), 0)
    qpos = start + (lax.broadcasted_iota(jnp.int32, (grows, 1), 0) & (tq - 1))
    c_end = lax.broadcasted_iota(jnp.int32, (1, nc), 1) * NSA_CMP_BLOCK + (NSA_CMP_BLOCK - 1)
    blk_lane = lax.broadcasted_iota(jnp.int32, (1, LANES), 1)
    blk_row_f = lax.broadcasted_iota(jnp.int32, (LANES, tq), 0).astype(F32)
    cur = tokpos // NSA_SEL_BLOCK
    n_kt = (start + tq - 1) // tk + 1
    lo = jnp.maximum(start - NSA_WINDOW, 0)
    wlen = NSA_WINDOW + tq
    kvc = kvc_ref[0]
    kc = kvc[:, :LANES].astype(BF16)
    vc = kvc[:, LANES:].astype(BF16)
    o_cmp, o_sel, o_win = [], [], []

    for g in range(NSA_GROUPS):
        qg = _group_rows(q, g)

        sc = lax.dot_general(qg, kc, _NT, preferred_element_type=F32)
        pc = _softmax_rows(sc, c_end <= qpos)
        o_cmp.append(jnp.dot(pc.astype(BF16), vc, preferred_element_type=F32))

        psum = pc[0:tq]
        for h in range(1, NSA_HPG):
            psum = psum + pc[h * tq:(h + 1) * tq]
        imp = _group_sum(psum, pool_ref[...])
        forced = (blk_lane == 0) | (blk_lane == cur) | (blk_lane == cur - 1)
        score = jnp.where(blk_lane > cur, -1e9, imp + NSA_FORCE * forced.astype(F32))
        score = jnp.where(blk_lane < n_sel, score, -3e38)
        sel = _top_blocks_t(score.T, blk_row_f, LANES).T.astype(BF16)

        def sel_step(kt, carry):
            m_prev, l_prev, acc = carry
            off = pl.multiple_of(kt * tk, tk)
            kv = kvs_ref[0, pl.ds(off, tk), :]
            k = kv[:, :LANES].astype(BF16)
            v = kv[:, LANES:].astype(BF16)
            s = lax.dot_general(qg, k, _NT, preferred_element_type=F32)
            keyi = kt * tk + lax.broadcasted_iota(jnp.int32, (1, tk), 1)
            expand = (lax.broadcasted_iota(jnp.int32, (LANES, tk), 0) == keyi // NSA_SEL_BLOCK).astype(BF16)
            selm = jnp.dot(sel, expand, preferred_element_type=F32)
            valid = (selm > 0.5) & (keyi <= tokpos)
            s3 = jnp.where(valid[None], s.reshape(NSA_HPG, tq, tk), NEG)
            m_new = jnp.maximum(m_prev, jnp.max(s3, axis=-1, keepdims=True))
            alpha = jnp.exp(m_prev - m_new)
            p3 = jnp.exp(s3 - m_new)
            l_new = alpha * l_prev + jnp.sum(p3, axis=-1, keepdims=True)
            pv = jnp.dot(p3.reshape(grows, tk).astype(BF16), v, preferred_element_type=F32)
            return m_new, l_new, alpha * acc + pv.reshape(NSA_HPG, tq, LANES)

        init = (jnp.full((NSA_HPG, tq, 1), NEG, F32), jnp.zeros((NSA_HPG, tq, 1), F32),
                jnp.zeros((NSA_HPG, tq, LANES), F32))
        _, l_fin, acc = lax.fori_loop(0, n_kt, sel_step, init)
        o_sel.append((acc / l_fin).reshape(grows, LANES))

        woff = pl.multiple_of(lo, tq)
        kvw = kvw_ref[0, pl.ds(woff, wlen), :]
        sw = lax.dot_general(qg, kvw[:, :LANES].astype(BF16), _NT, preferred_element_type=F32)
        dpos = tokpos - (lo + lax.broadcasted_iota(jnp.int32, (1, wlen), 1))
        validw = (dpos >= 0) & (dpos < NSA_WINDOW)
        sw3 = sw.reshape(NSA_HPG, tq, wlen)
        pw = _softmax_rows(sw3, jnp.broadcast_to(validw[None], sw3.shape))
        o_win.append(jnp.dot(pw.reshape(grows, wlen).astype(BF16), kvw[:, LANES:].astype(BF16),
                             preferred_element_type=F32))

    gs = gate_ref[...]
    o_ref[...] = (_expand_gates(gs, exp_ref, 0) * _place_heads(o_cmp, tq)
                  + _expand_gates(gs, exp_ref, 1) * _place_heads(o_sel, tq)
                  + _expand_gates(gs, exp_ref, 2) * _place_heads(o_win, tq))


def nsa_prompt(q8, gates, kvc, kv_s, kv_w, pool, gate_exp, n, *, tq=128, tk=512):
    t = q8.shape[0] // n
    nc = kvc.shape[1]
    n_sel = t // NSA_SEL_BLOCK
    assert nc == LANES and 2 * n_sel == nc and t % tk == 0 and t >= NSA_WINDOW + tq
    nq = t // tq
    full = lambda a: pl.BlockSpec((1,) + a.shape[1:], lambda b, i: (b, 0, 0))
    res = lambda a: pl.BlockSpec(a.shape, lambda b, i: (0,) * a.ndim)
    return pl.pallas_call(
        functools.partial(_nsa_prompt_kernel, tq=tq, tk=tk, nc=nc, n_sel=n_sel),
        grid=(n, nq),
        in_specs=[pl.BlockSpec((tq, NSA_QW), lambda b, i: (b * nq + i, 0)),
                  pl.BlockSpec((tq, LANES), lambda b, i: (b * nq + i, 0)),
                  full(kvc), full(kv_s), full(kv_w), res(pool), res(gate_exp)],
        out_specs=pl.BlockSpec((tq, NSA_QW), lambda b, i: (b * nq + i, 0)),
        out_shape=jax.ShapeDtypeStruct((n * t, NSA_QW), F32),
        compiler_params=_cparams(("parallel", "arbitrary")),
        name="nsa_prompt",
    )(q8, gates, kvc, kv_s, kv_w, pool, gate_exp)


def _nsa_decode_kernel(pt_ref, q_ref, gate_ref, snew_ref, wnew_ref, win_ref, wt_ref, pbd_ref, pool_ref, exp_ref,
                       *refs, pg, n_steps, t_s, past):
    cmp_refs = refs[:pg]
    sel_refs = refs[pg:2 * pg]
    o_ref = refs[2 * pg]
    pooled_ref, sel_ref, ocmp_ref, m_ref, l_ref, acc_ref = refs[2 * pg + 1:]
    j = pl.program_id(1)
    tp = q_ref.shape[1]
    grows = NSA_HPG * tp
    page = cmp_refs[0].shape[2]
    nc = pooled_ref.shape[0]
    nblk = sel_ref.shape[2]
    q = q_ref[0]
    tokpos = past + lax.broadcasted_iota(jnp.int32, (tp, 1), 0)
    qgs = [_group_rows(q, g) for g in range(NSA_GROUPS)]

    @pl.when(j < n_steps)
    def _():
        rows = jnp.concatenate([r[0, 0] for r in cmp_refs], axis=0)
        wt = jnp.concatenate([wt_ref[...]] * pg, axis=0)
        nb = pg * page // NSA_CMP_BLOCK
        pooled = jnp.sum((rows * wt).reshape(nb, NSA_CMP_BLOCK, NSA_KV_COLS), axis=1)
        pooled_ref[pl.ds(pl.multiple_of(j * nb, nb), nb), :] = pooled

    @pl.when(j == n_steps - 1)
    def _():
        kvc = jnp.dot(pooled_ref[...].astype(BF16), pbd_ref[...], preferred_element_type=F32)
        kc = kvc[:, :LANES].astype(BF16)
        vc = kvc[:, LANES:].astype(BF16)
        qpos = past + (lax.broadcasted_iota(jnp.int32, (grows, 1), 0) & (tp - 1))
        c_end = lax.broadcasted_iota(jnp.int32, (1, nc), 1) * NSA_CMP_BLOCK + (NSA_CMP_BLOCK - 1)
        blk_lane = lax.broadcasted_iota(jnp.int32, (1, nblk), 1)
        blk_lane_f = blk_lane.astype(F32)
        cur = tokpos // NSA_SEL_BLOCK
        n_sel = -(-(past + t_s) // NSA_SEL_BLOCK)
        for g in range(NSA_GROUPS):
            sc = lax.dot_general(qgs[g], kc, _NT, preferred_element_type=F32)
            pc = _softmax_rows(sc, c_end <= qpos)
            ocmp_ref[g] = jnp.dot(pc.astype(BF16), vc, preferred_element_type=F32)
            psum = pc[0:tp]
            for h in range(1, NSA_HPG):
                psum = psum + pc[h * tp:(h + 1) * tp]
            imp = _group_sum(psum, pool_ref[...])
            forced = (blk_lane == 0) | (blk_lane == cur) | (blk_lane == cur - 1)
            score = jnp.where(blk_lane > cur, -1e9, imp + NSA_FORCE * forced.astype(F32))
            score = jnp.where(blk_lane < n_sel, score, -3e38)
            sel = jnp.zeros((tp, nblk), F32)
            for _ in range(NSA_TOPK):
                mx = jnp.max(score, axis=1, keepdims=True)
                first = jnp.min(jnp.where(score == mx, blk_lane_f, float(nblk)), axis=1, keepdims=True)
                pick = blk_lane_f == first
                sel = jnp.where(pick, 1.0, sel)
                score = jnp.where(pick, -3e38, score)
            sel_ref[g] = sel
        m_ref[...] = jnp.full(m_ref.shape, NEG, F32)
        l_ref[...] = jnp.zeros(l_ref.shape, F32)
        acc_ref[...] = jnp.zeros(acc_ref.shape, F32)

    def sel_update(g, s, valid, vs):
        nk = s.shape[1]
        s3 = jnp.where(valid[None], s.reshape(NSA_HPG, tp, nk), NEG)
        m_prev = m_ref[g]
        m_new = jnp.maximum(m_prev, jnp.max(s3, axis=-1, keepdims=True))
        alpha = jnp.exp(m_prev - m_new)
        p3 = jnp.where(valid[None], jnp.exp(s3 - m_new), 0.0)
        pb = p3.reshape(grows, nk).astype(BF16)
        off = 0
        pv = None
        for v in vs:
            term = jnp.dot(pb[:, off:off + v.shape[0]], v, preferred_element_type=F32)
            pv = term if pv is None else pv + term
            off += v.shape[0]
        l_ref[g] = alpha * l_ref[g] + jnp.sum(p3, axis=-1, keepdims=True)
        acc_ref[g] = alpha * acc_ref[g] + pv.reshape(NSA_HPG, tp, LANES)
        m_ref[g] = m_new

    @pl.when(j >= n_steps)
    def _():
        jj = j - n_steps
        tiles = [r[0, 0] for r in sel_refs]
        ks = [x[:, :LANES].astype(BF16) for x in tiles]
        vs = [x[:, LANES:].astype(BF16) for x in tiles]
        nk = pg * page
        keyi = jj * nk + lax.broadcasted_iota(jnp.int32, (1, nk), 1)
        expand = (lax.broadcasted_iota(jnp.int32, (nblk, nk), 0) == keyi // NSA_SEL_BLOCK).astype(BF16)
        for g in range(NSA_GROUPS):
            s = jnp.concatenate([lax.dot_general(qgs[g], k, _NT, preferred_element_type=F32) for k in ks], axis=1)
            selm = jnp.dot(sel_ref[g].astype(BF16), expand, preferred_element_type=F32)
            sel_update(g, s, selm > 0.5, vs)

    @pl.when(j == 2 * n_steps - 1)
    def _():
        snew = snew_ref[0]
        wnew = wnew_ref[0]
        tk = lax.broadcasted_iota(jnp.int32, (1, tp), 1)
        tq = lax.broadcasted_iota(jnp.int32, (tp, 1), 0)
        causal_new = (tk <= tq) & (tk < t_s)
        new_blk = past // NSA_SEL_BLOCK
        blk_lane = lax.broadcasted_iota(jnp.int32, (1, nblk), 1)
        win = win_ref[0, 0]
        win_s = win.shape[0]
        wk = win[:, :LANES].astype(BF16)
        wv = win[:, LANES:].astype(BF16)
        wi = lax.broadcasted_iota(jnp.int32, (1, win_s), 1)
        dpos_c = tokpos - (past - win_s + wi)
        valid_c = (dpos_c >= 0) & (dpos_c < NSA_WINDOW)
        o_sel, o_win = [], []
        for g in range(NSA_GROUPS):
            picked = jnp.sum(jnp.where(blk_lane == new_blk, sel_ref[g], 0.0), axis=1, keepdims=True) > 0.5
            s = lax.dot_general(qgs[g], snew[:, :LANES].astype(BF16), _NT, preferred_element_type=F32)
            sel_update(g, s, causal_new & picked, [snew[:, LANES:].astype(BF16)])
            o_sel.append((acc_ref[g] / jnp.maximum(l_ref[g], 1e-30)).reshape(grows, LANES))

            sc = lax.dot_general(qgs[g], wk, _NT, preferred_element_type=F32)
            sn = lax.dot_general(qgs[g], wnew[:, :LANES].astype(BF16), _NT, preferred_element_type=F32)
            sc3 = jnp.where(valid_c[None], sc.reshape(NSA_HPG, tp, win_s), NEG)
            sn3 = jnp.where(causal_new[None], sn.reshape(NSA_HPG, tp, tp), NEG)
            mw = jnp.maximum(jnp.max(sc3, axis=-1, keepdims=True), jnp.max(sn3, axis=-1, keepdims=True))
            pcw = jnp.where(valid_c[None], jnp.exp(sc3 - mw), 0.0)
            pnw = jnp.where(causal_new[None], jnp.exp(sn3 - mw), 0.0)
            den = jnp.maximum(jnp.sum(pcw, axis=-1, keepdims=True) + jnp.sum(pnw, axis=-1, keepdims=True), 1e-30)
            ow = (jnp.dot(pcw.reshape(grows, win_s).astype(BF16), wv, preferred_element_type=F32)
                  + jnp.dot(pnw.reshape(grows, tp).astype(BF16), wnew[:, LANES:].astype(BF16),
                            preferred_element_type=F32))
            o_win.append(ow / den.reshape(grows, 1))
        gs = gate_ref[0]
        o_cmp = [ocmp_ref[g] for g in range(NSA_GROUPS)]
        o_ref[0] = (_expand_gates(gs, exp_ref, 0) * _place_heads(o_cmp, tp)
                    + _expand_gates(gs, exp_ref, 1) * _place_heads(o_sel, tp)
                    + _expand_gates(gs, exp_ref, 2) * _place_heads(o_win, tp))


def nsa_decode(page_table, q8, gates, kvs_new, kvw_new, cache_cmp, cache_sel, cache_win, wts, l, *, t_s, pg=8):
    b, tp, _ = q8.shape
    n_pages = page_table.shape[1]
    page = cache_cmp.shape[2]
    past = n_pages * page
    n_steps = n_pages // pg
    nc = past // NSA_CMP_BLOCK
    assert (past + t_s) // NSA_CMP_BLOCK == nc
    nblk = wts['pool_s'].shape[1]
    pt = page_table.reshape(-1)

    def cmp_spec(i):
        return pl.BlockSpec((1, 1, page, NSA_KV_COLS),
                            lambda s, j, pt_ref: (l, pt_ref[s * n_pages + jnp.minimum(j, n_steps - 1) * pg + i], 0, 0))

    def sel_spec(i):
        return pl.BlockSpec((1, 1, page, NSA_KV_COLS),
                            lambda s, j, pt_ref: (l, pt_ref[s * n_pages + jnp.maximum(j - n_steps, 0) * pg + i], 0, 0))

    seq = lambda a: pl.BlockSpec((1,) + a.shape[1:], lambda s, j, pt_ref: (s, 0, 0))
    res = lambda a: pl.BlockSpec(a.shape, lambda s, j, pt_ref: (0,) * a.ndim)
    consts = [wts['cmp_w'][:page], wts['cmp_proj'], wts['pool_s'], wts['gate_exp']]
    grid_spec = pltpu.PrefetchScalarGridSpec(
        num_scalar_prefetch=1,
        grid=(b, 2 * n_steps),
        in_specs=[seq(q8), seq(gates), seq(kvs_new), seq(kvw_new),
                  pl.BlockSpec((1, 1) + cache_win.shape[2:], lambda s, j, pt_ref: (l, s, 0, 0))]
                 + [res(a) for a in consts]
                 + [cmp_spec(i) for i in range(pg)] + [sel_spec(i) for i in range(pg)],
        out_specs=pl.BlockSpec((1, tp, NSA_QW), lambda s, j, pt_ref: (s, 0, 0)),
        scratch_shapes=[pltpu.VMEM((nc, NSA_KV_COLS), F32),
                        pltpu.VMEM((NSA_GROUPS, tp, nblk), F32),
                        pltpu.VMEM((NSA_GROUPS, NSA_HPG * tp, LANES), F32),
                        pltpu.VMEM((NSA_GROUPS, NSA_HPG, tp, 1), F32),
                        pltpu.VMEM((NSA_GROUPS, NSA_HPG, tp, 1), F32),
                        pltpu.VMEM((NSA_GROUPS, NSA_HPG, tp, LANES), F32)])
    return pl.pallas_call(
        functools.partial(_nsa_decode_kernel, pg=pg, n_steps=n_steps, t_s=t_s, past=past),
        grid_spec=grid_spec,
        out_shape=jax.ShapeDtypeStruct((b, tp, NSA_QW), F32),
        compiler_params=_cparams(("parallel", "arbitrary")),
        name="nsa_decode",
    )(pt, q8, gates, kvs_new, kvw_new, cache_win, *consts, *([cache_cmp] * pg), *([cache_sel] * pg))


def _merge_kernel(x_ref, oa_ref, ob_ref, oc_ref, br_ref, pa_ref, pb_ref, pc_ref, wo_ref, y_ref):
    d = D_MODEL
    br = br_ref[...]
    u = (jax.nn.sigmoid(br[:, 0:d]) * jnp.dot(oa_ref[...].astype(BF16), pa_ref[...], preferred_element_type=F32)
         + jax.nn.sigmoid(br[:, d:2 * d]) * jnp.dot(ob_ref[...].astype(BF16), pb_ref[...], preferred_element_type=F32)
         + jax.nn.sigmoid(br[:, 2 * d:3 * d]) * jnp.dot(oc_ref[...].astype(BF16), pc_ref[...], preferred_element_type=F32))
    y_ref[...] = x_ref[...] + jnp.dot(u.astype(BF16), wo_ref[...], preferred_element_type=F32)


def merge(x, o_a, o_b, o_c, br, p_a, p_b, p_c, w_out, *, tm=256):
    m, d = x.shape
    tm = min(tm, m)
    bw = o_a.shape[1]
    row = lambda w: pl.BlockSpec((tm, w), lambda i: (i, 0))
    res = lambda a: pl.BlockSpec(a.shape, lambda i: (0, 0))
    return pl.pallas_call(
        _merge_kernel,
        grid=(m // tm,),
        in_specs=[row(d), row(bw), row(bw), row(bw), row(3 * d), res(p_a), res(p_b), res(p_c), res(w_out)],
        out_specs=row(d),
        out_shape=jax.ShapeDtypeStruct((m, d), F32),
        compiler_params=_cparams(("parallel",)),
        name="merge",
    )(x, o_a, o_b, o_c, br, p_a, p_b, p_c, w_out)


def _router_kernel(x_ref, g_ref, wr_ref, h_ref, logit_ref):
    x = x_ref[...]
    ms = jnp.mean(x * x, axis=-1, keepdims=True)
    h = x * lax.rsqrt(ms + NORM_EPS) * g_ref[...]
    h_ref[...] = h.astype(BF16)
    logit_ref[...] = jnp.dot(h, wr_ref[...], preferred_element_type=F32, precision=HIGHEST)


def norm_router(x, g, w_router, *, tm=512):
    m, d = x.shape
    tm = min(tm, m)
    return pl.pallas_call(
        _router_kernel,
        grid=(m // tm,),
        in_specs=[pl.BlockSpec((tm, d), lambda i: (i, 0)),
                  pl.BlockSpec((1, d), lambda i: (0, 0)),
                  pl.BlockSpec((d, ROUTER_PAD), lambda i: (0, 0))],
        out_specs=[pl.BlockSpec((tm, d), lambda i: (i, 0)),
                   pl.BlockSpec((tm, ROUTER_PAD), lambda i: (i, 0))],
        out_shape=[jax.ShapeDtypeStruct((m, d), BF16), jax.ShapeDtypeStruct((m, ROUTER_PAD), F32)],
        compiler_params=_cparams(("parallel",)),
        name="norm_router",
    )(x, g.reshape(1, d), w_router)


def _expert_kernel(be_ref, x_ref, rw_ref, w1_ref, w3_ref, w2_ref, o_ref):
    x = x_ref[...]
    a = jnp.dot(x, w1_ref[0], preferred_element_type=F32)
    b = jnp.dot(x, w3_ref[0], preferred_element_type=F32)
    hmid = _silu(a) * b
    y = jnp.dot(hmid.astype(BF16), w2_ref[0], preferred_element_type=F32)
    o_ref[...] = y * rw_ref[...]


def grouped_experts_blocks(xb, row_w, blk_e, w1, w3, w2):
    rows, d = xb.shape
    n_blk = rows // MOE_BLOCK
    grid_spec = pltpu.PrefetchScalarGridSpec(
        num_scalar_prefetch=1,
        grid=(n_blk,),
        in_specs=[pl.BlockSpec((MOE_BLOCK, d), lambda b, be: (b, 0)),
                  pl.BlockSpec((MOE_BLOCK, 1), lambda b, be: (b, 0)),
                  pl.BlockSpec((1, d, MOE_DFF), lambda b, be: (be[b], 0, 0)),
                  pl.BlockSpec((1, d, MOE_DFF), lambda b, be: (be[b], 0, 0)),
                  pl.BlockSpec((1, MOE_DFF, d), lambda b, be: (be[b], 0, 0))],
        out_specs=pl.BlockSpec((MOE_BLOCK, d), lambda b, be: (b, 0)))
    return pl.pallas_call(
        _expert_kernel,
        grid_spec=grid_spec,
        out_shape=jax.ShapeDtypeStruct((rows, d), F32),
        compiler_params=_cparams(("arbitrary",)),
        name="moe_experts",
    )(blk_e, xb, row_w.reshape(rows, 1), w1, w3, w2)


def hier_moe_residual(x, norm_g, w_router, w1, w3, w2):
    m, d = x.shape
    h, logits = norm_router(x, norm_g, w_router)
    g_logit = logits[:, :MOE_GROUPS]
    _, g_idx = lax.top_k(g_logit, 1)
    g_w = jnp.take_along_axis(jax.nn.softmax(g_logit, axis=-1), g_idx, axis=-1)
    e_logit = logits[:, MOE_GROUPS:MOE_GROUPS + MOE_EXPERTS].reshape(-1, MOE_GROUPS, MOE_EPG)
    e_logit = jnp.take_along_axis(e_logit, g_idx[:, :, None], axis=1)[:, 0]
    e_val, e_idx = lax.top_k(e_logit, MOE_TOPK)
    ew = jax.nn.softmax(e_val, axis=-1) * g_w
    eid = g_idx * MOE_EPG + e_idx

    n_assign = m * MOE_TOPK
    fe = eid.reshape(-1).astype(jnp.int32)
    onehot = (fe[:, None] == jnp.arange(MOE_EXPERTS, dtype=jnp.int32)[None, :]).astype(jnp.int32)
    csum = jnp.cumsum(onehot, axis=0)
    rank = jnp.sum((csum - onehot) * onehot, axis=1)
    counts = csum[-1]
    padded = (counts + MOE_BLOCK - 1) // MOE_BLOCK * MOE_BLOCK
    ends = jnp.cumsum(padded)
    dest = (ends - padded)[fe] + rank
    n_blk = -(-(n_assign + MOE_EXPERTS * (MOE_BLOCK - 1)) // MOE_BLOCK)
    rows = n_blk * MOE_BLOCK
    tok = jnp.arange(n_assign, dtype=jnp.int32) // MOE_TOPK
    row_tok = jnp.full((rows,), m, jnp.int32).at[dest].set(tok)
    row_w = jnp.zeros((rows,), F32).at[dest].set(ew.reshape(-1))
    blk_e = jnp.minimum(jnp.searchsorted(ends, jnp.arange(n_blk, dtype=jnp.int32) * MOE_BLOCK, side='right'),
                        MOE_EXPERTS - 1).astype(jnp.int32)
    xb = jnp.concatenate([h, jnp.zeros((1, d), h.dtype)], axis=0)[row_tok]
    yb = grouped_experts_blocks(xb, row_w, blk_e, w1, w3, w2)
    slot = dest.reshape(m, MOE_TOPK)
    y = yb[slot[:, 0]]
    for k in range(1, MOE_TOPK):
        y = y + yb[slot[:, k]]
    return x + y


def _rope_tables(pos):
    posf = pos.astype(F32)[:, None]
    inv_m = ROPE_THETA ** (-(jnp.arange(MLA_HALF, dtype=F32) * 2.0 / MLA_ROPE))
    ang_m = posf * inv_m[None, :]
    cos_m = jnp.tile(jnp.cos(ang_m), (1, LANES // MLA_HALF))
    sin_m = jnp.tile(jnp.sin(ang_m), (1, LANES // MLA_HALF))
    half = NSA_ROT // 2
    inv_n = ROPE_THETA ** (-(jnp.arange(half, dtype=F32) * 2.0 / NSA_ROT))
    ang_n = posf * inv_n[None, :]
    cn, sn = jnp.cos(ang_n), jnp.sin(ang_n)
    t = pos.shape[0]
    one = jnp.ones((t, NSA_DH - NSA_ROT), F32)
    zero = jnp.zeros((t, NSA_DH - NSA_ROT), F32)
    z8 = jnp.zeros((t, half), F32)
    c = jnp.concatenate([cn, cn, one], axis=1)
    s1 = jnp.concatenate([-sn, z8, zero], axis=1)
    s2 = jnp.concatenate([z8, sn, zero], axis=1)
    rep = lambda a: jnp.tile(a, (1, LANES // NSA_DH))
    return (cos_m, sin_m), (rep(c), rep(s1), rep(s2))


def _layer_weights(l, w_in, raw):
    w = {}
    wl = w_in[l]
    d = wl.shape[0]
    col = lambda i: wl[:, IN_OFF[i]:IN_OFF[i + 1]]
    zeros = lambda n: jnp.zeros((d, n), F32)
    w['w_gdn'] = jnp.concatenate([col(0), col(1), col(2), col(3), zeros(LANES - 2 * GDN_HEADS)], axis=1).astype(BF16)
    kr = col(6)
    kr_rep = jnp.concatenate([jnp.tile(kr[:, :MLA_HALF], (1, MLA_HEADS)), jnp.tile(kr[:, MLA_HALF:], (1, MLA_HEADS))],
                             axis=1)
    w['w_mla'] = jnp.concatenate([col(4), col(5), kr_rep, zeros(LANES)], axis=1).astype(BF16)
    w['w_nsa'] = jnp.concatenate([col(7), col(8), col(9), col(10), col(11), zeros(LANES - 3 * NSA_HEADS)],
                                 axis=1).astype(BF16)
    w['w_br'] = col(12).astype(BF16)

    uq = raw['mla_w_uq'][l].reshape(MLA_Q_LORA, MLA_HEADS, MLA_NOPE + MLA_ROPE)
    w['w_uq'] = jnp.concatenate([uq[:, :, :MLA_NOPE].reshape(MLA_Q_LORA, -1),
                                 uq[:, :, MLA_NOPE:MLA_NOPE + MLA_HALF].reshape(MLA_Q_LORA, -1),
                                 uq[:, :, MLA_NOPE + MLA_HALF:].reshape(MLA_Q_LORA, -1)], axis=1).astype(BF16)
    uk = jnp.transpose(raw['mla_w_uk'][l], (0, 2, 1))
    zk = jnp.zeros_like(uk)
    even = (jnp.arange(MLA_HEADS) % 2 == 0)[:, None, None]
    w['w_uk'] = jnp.concatenate([jnp.where(even, uk, zk), jnp.where(even, zk, uk)], axis=1).astype(BF16)
    w['w_uv'] = raw['mla_w_uv'][l].astype(BF16)
    w['w_uv_all'] = jnp.transpose(raw['mla_w_uv'][l], (1, 0, 2)).reshape(MLA_KV_LORA, MLA_HEADS * MLA_V).astype(BF16)
    w['cq_g'] = raw['mla_cq_g'][l].reshape(1, -1)
    w['ckv_g'] = raw['mla_ckv_g'][l].reshape(1, -1)
    w['qn_g'] = jnp.tile(raw['mla_qn_g'][l], MLA_HEADS).reshape(1, -1)
    split_tile = lambda g: jnp.stack([jnp.tile(g[:MLA_HALF], MLA_HEADS), jnp.tile(g[MLA_HALF:], MLA_HEADS)])
    w['qr_g'] = split_tile(raw['mla_qr_g'][l])
    w['kr_g'] = split_tile(raw['mla_kr_g'][l])
    w['bd64_512'] = _block_diag_ones(NSA_QW, NSA_DH)
    w['bd64_128'] = _block_diag_ones(LANES, NSA_DH)
    w['bd16'] = _block_diag_ones(LANES, MLA_HALF)

    w['nsa_qn_g'] = jnp.tile(raw['nsa_qn_g'][l], NSA_HEADS).reshape(1, -1)
    w['nsa_kn_g'] = jnp.tile(raw['nsa_kn_g'][l], (1, NSA_GROUPS))
    wpos = raw['nsa_cmp_wpos'][l]
    wrow = jnp.concatenate([jnp.tile(wpos[0], (1, NSA_GROUPS)), jnp.tile(wpos[1], (1, NSA_GROUPS))], axis=1)
    w['cmp_w'] = jnp.tile(wrow, (512 // NSA_CMP_BLOCK, 1))
    proj = raw['nsa_cmp_proj'][l]
    pbd = jnp.zeros((NSA_KV_COLS, NSA_KV_COLS), F32)
    for kv in range(2):
        for g in range(NSA_GROUPS):
            o = (kv * NSA_GROUPS + g) * NSA_DH
            pbd = pbd.at[o:o + NSA_DH, o:o + NSA_DH].set(proj[kv])
    w['cmp_proj'] = pbd.astype(BF16)
    ge = np.zeros((3, LANES, NSA_QW), np.float32)
    for j in range(3):
        for h in range(NSA_HEADS):
            ge[j, 3 * h + j, h * NSA_DH:(h + 1) * NSA_DH] = 1.0
    w['gate_exp'] = jnp.asarray(ge, BF16)

    w['p_gdn'] = raw['p_gdn'][l].astype(BF16)
    w['p_mla'] = raw['p_mla'][l].astype(BF16)
    w['p_nsa'] = raw['p_nsa'][l].astype(BF16)
    w['w_out'] = raw['w_out'][l].astype(BF16)
    w['w_router'] = jnp.pad(jnp.concatenate([raw['moe_w_group'][l], raw['moe_w_expert'][l]], axis=1),
                            ((0, 0), (0, ROUTER_PAD - MOE_GROUPS - MOE_EXPERTS)))
    w['w1'] = raw['moe_w1'][l].astype(BF16)
    w['w3'] = raw['moe_w3'][l].astype(BF16)
    w['w2'] = raw['moe_w2'][l].astype(BF16)
    return w


def _pool_matrix(n_in, n_out):
    pm = np.zeros((n_in, n_out), np.float32)
    ratio = NSA_SEL_BLOCK // NSA_CMP_BLOCK
    for c in range(n_in):
        if c // ratio < n_out:
            pm[c, c // ratio] = 1.0
    return jnp.asarray(pm, BF16)


def kernel(x_prompt, x_sample, cache_mla_ckv, cache_mla_krope, cache_nsa_cmp, cache_nsa_sel, cache_nsa_win, state_gdn_ssm, state_gdn_conv, page_table, norm1_g, w_in, gdn_conv_w, gdn_a_log, gdn_dt_bias, gdn_out_g, mla_cq_g, mla_ckv_g, mla_w_uq, mla_qn_g, mla_qr_g, mla_kr_g, mla_w_uk, mla_w_uv, nsa_qn_g, nsa_kn_g, nsa_cmp_wpos, nsa_cmp_proj, p_gdn, p_mla, p_nsa, w_out, norm2_g, moe_w_group, moe_w_expert, moe_w1, moe_w3, moe_w2):
    depth = w_in.shape[0]
    b_p, t_p, d = x_prompt.shape
    b_s, t_s, _ = x_sample.shape
    n_pool, page = cache_mla_ckv.shape[1:3]
    past = page_table.shape[1] * page
    win_s = cache_nsa_win.shape[2]
    assert t_p >= GDN_CONV - 1 and t_s >= GDN_CONV - 1 and t_p % GDN_CHUNK == 0
    tp_s = 8
    assert t_s <= tp_s
    raw = dict(mla_w_uq=mla_w_uq, mla_w_uk=mla_w_uk, mla_w_uv=mla_w_uv, mla_cq_g=mla_cq_g, mla_ckv_g=mla_ckv_g,
               mla_qn_g=mla_qn_g, mla_qr_g=mla_qr_g, mla_kr_g=mla_kr_g, nsa_qn_g=nsa_qn_g, nsa_kn_g=nsa_kn_g,
               nsa_cmp_wpos=nsa_cmp_wpos, nsa_cmp_proj=nsa_cmp_proj, p_gdn=p_gdn, p_mla=p_mla, p_nsa=p_nsa,
               w_out=w_out, moe_w_group=moe_w_group, moe_w_expert=moe_w_expert, moe_w1=moe_w1, moe_w3=moe_w3,
               moe_w2=moe_w2)
    pos_p = jnp.arange(t_p, dtype=jnp.int32)
    pos_s = jnp.tile(past + jnp.arange(t_s, dtype=jnp.int32), b_s)
    mla_tab_p, nsa_tab_p = _rope_tables(pos_p)
    mla_tab_s, nsa_tab_s = _rope_tables(pos_s)
    pool_p = _pool_matrix(LANES, LANES)
    nc_s = past // NSA_CMP_BLOCK
    pool_s = _pool_matrix(nc_s, nc_s)
    cache_cmp2 = cache_nsa_cmp.reshape(depth, n_pool, page, NSA_KV_COLS)
    cache_sel2 = cache_nsa_sel.reshape(depth, n_pool, page, NSA_KV_COLS)
    cache_win2 = cache_nsa_win.reshape(depth, b_s, win_s, NSA_KV_COLS)

    def project(x2, l, w):
        g = norm1_g[l]
        return (norm_matmul(x2, g, w['w_gdn']), norm_matmul(x2, g, w['w_mla']),
                norm_matmul(x2, g, w['w_nsa']), norm_matmul(x2, g, w['w_br']))

    def finish(x2, o_a, o_b, o_c, br, l, w):
        y = merge(x2, o_a, o_b, o_c, br, w['p_gdn'], w['p_mla'], w['p_nsa'], w['w_out'])
        return hier_moe_residual(y, norm2_g[l], w['w_router'], w['w1'], w['w3'], w['w2'])

    def prompt_layer(x, l, w):
        m = b_p * t_p
        x2 = x.reshape(m, d)
        pg_, pm_, pn_, br = project(x2, l, w)
        pg3 = pg_.reshape(b_p, t_p, GDN_COLS)
        o_a, ssm = gdn_mixer(pg3, jnp.zeros((b_p, 8, GDN_CONV_DIM), F32),
                             jnp.zeros((b_p, GDN_HEADS, GDN_DK, GDN_DV), F32),
                             gdn_conv_w[l], gdn_a_log[l], gdn_dt_bias[l], gdn_out_g[l],
                             chunk=GDN_CHUNK, t_valid=GDN_CHUNK)
        conv = pg3[:, t_p - (GDN_CONV - 1):, :GDN_CONV_DIM]
        q_cat, k_cat, ckv, krope = mla_prep(pm_, *mla_tab_p, w, tm=256)
        o_b = mla_flash(q_cat, k_cat, w['w_uv'], b_p)
        q8, kv_c, kv_s, kv_w, gates, kvc = nsa_prep(pn_, nsa_tab_p, w, tm=256, compress=True)
        o_c = nsa_prompt(q8, gates, kvc.reshape(b_p, t_p // NSA_CMP_BLOCK, NSA_KV_COLS),
                         kv_s.reshape(b_p, t_p, NSA_KV_COLS), kv_w.reshape(b_p, t_p, NSA_KV_COLS),
                         pool_p, w['gate_exp'], b_p)
        y = finish(x2, o_a.reshape(m, GDN_VW), o_b, o_c, br, l, w).reshape(b_p, t_p, d)
        kv5 = lambda a: a.reshape(b_p, t_p, 2, NSA_GROUPS, NSA_DH)
        kvw5 = kv5(kv_w)
        return y, (ssm, conv, ckv.reshape(b_p, t_p, MLA_KV_LORA), krope.reshape(b_p, t_p, MLA_ROPE),
                   kv5(kv_c), kv5(kv_s), kvw5[:, t_p - min(NSA_WINDOW, t_p):])

    def sample_layer(x, l, w):
        m = b_s * t_s
        x2 = x.reshape(m, d)
        pg_, pm_, pn_, br = project(x2, l, w)
        pad_t = lambda a: jnp.pad(a.reshape(b_s, t_s, -1), ((0, 0), (0, tp_s - t_s), (0, 0)))
        pg3 = pg_.reshape(b_s, t_s, GDN_COLS)
        hist8 = jnp.pad(state_gdn_conv[l], ((0, 0), (8 - (GDN_CONV - 1), 0), (0, 0)))
        o_a, ssm = gdn_mixer(pad_t(pg_), hist8, state_gdn_ssm[l], gdn_conv_w[l], gdn_a_log[l], gdn_dt_bias[l],
                             gdn_out_g[l], chunk=tp_s, t_valid=t_s)
        o_a = o_a[:, :t_s].reshape(m, GDN_VW)
        conv = pg3[:, t_s - (GDN_CONV - 1):, :GDN_CONV_DIM]

        q_cat, k_cat, ckv, krope = mla_prep(pm_, *mla_tab_s, w, tm=m)
        qc = q_cat.reshape(m, MLA_HEADS, MLA_QK)
        q_lat = qc[:, :, :MLA_KV_LORA]
        hsel = jnp.arange(MLA_HEADS)
        r1 = qc[:, :, MLA_KV_LORA:MLA_KV_LORA + LANES].reshape(m, MLA_HEADS, MLA_HEADS, MLA_HALF)[:, hsel, hsel]
        r2 = qc[:, :, MLA_KV_LORA + LANES:].reshape(m, MLA_HEADS, MLA_HEADS, MLA_HALF)[:, hsel, hsel]
        q_rope = jnp.concatenate([r1, r2], axis=-1)
        rows_of = lambda a: pad_t(a.reshape(m, -1)).reshape(b_s, tp_s * MLA_HEADS, -1)
        o_b = mla_decode(page_table, rows_of(q_lat), rows_of(q_rope), pad_t(ckv).astype(BF16),
                         pad_t(krope).astype(BF16), w['w_uv_all'], cache_mla_ckv, cache_mla_krope, l, t_s=t_s)
        o_b = o_b[:, :t_s].reshape(m, MLA_HEADS * MLA_V)

        q8, kv_c, kv_s, kv_w, gates = nsa_prep(pn_, nsa_tab_s, w, tm=m, compress=False)
        ws = dict(w, pool_s=pool_s)
        o_c = nsa_decode(page_table, pad_t(q8), pad_t(gates), pad_t(kv_s), pad_t(kv_w), cache_cmp2, cache_sel2,
                         cache_win2, ws, l, t_s=t_s)
        o_c = o_c[:, :t_s].reshape(m, NSA_QW)
        y = finish(x2, o_a, o_b, o_c, br, l, w).reshape(b_s, t_s, d)
        kv5 = lambda a: a.reshape(b_s, t_s, 2, NSA_GROUPS, NSA_DH)
        win_new = jnp.concatenate([cache_nsa_win[l], kv5(kv_w)], axis=1)[:, t_s:]
        return y, (ssm, conv, ckv.reshape(b_s, t_s, MLA_KV_LORA), krope.reshape(b_s, t_s, MLA_ROPE),
                   kv5(kv_c), kv5(kv_s), win_new)

    xp, xs = x_prompt, x_sample
    p_states, s_states = [], []
    for l in range(depth):
        w = _layer_weights(l, w_in, raw)
        xp, st = prompt_layer(xp, l, w)
        p_states.append(st)
        xs, st = sample_layer(xs, l, w)
        s_states.append(st)
    p_out = [jnp.stack([st[i] for st in p_states], axis=0) for i in range(7)]
    s_out = [jnp.stack([st[i] for st in s_states], axis=0) for i in range(7)]
    return (xp, xs, *p_out, *s_out)
```

```python
import functools
import math

import numpy as np
import jax
import jax.numpy as jnp
from jax import lax
from jax.experimental import pallas as pl
from jax.experimental.pallas import tpu as pltpu

F32 = jnp.float32
BF16 = jnp.bfloat16
HIGHEST = lax.Precision.HIGHEST

D_MODEL = 1024
ROPE_THETA = 500000.0
NORM_EPS = 1e-6
NEG = -1e30
LANES = 128

GDN_HEADS = 4
GDN_DK = 128
GDN_DV = 128
GDN_CONV = 4
GDN_CHUNK = 64
GDN_QK = GDN_HEADS * GDN_DK
GDN_VW = GDN_HEADS * GDN_DV
GDN_CONV_DIM = 2 * GDN_QK + GDN_VW

MLA_HEADS = 8
MLA_Q_LORA = 384
MLA_KV_LORA = 256
MLA_NOPE = 64
MLA_ROPE = 32
MLA_HALF = MLA_ROPE // 2
MLA_V = 64
MLA_SCALE = (MLA_NOPE + MLA_ROPE) ** -0.5
MLA_QK = MLA_KV_LORA + 2 * LANES

NSA_HEADS = 8
NSA_GROUPS = 2
NSA_HPG = NSA_HEADS // NSA_GROUPS
NSA_DH = 64
NSA_ROT = NSA_DH // 4
NSA_CMP_BLOCK = 32
NSA_SEL_BLOCK = 64
NSA_TOPK = 16
NSA_WINDOW = 512
NSA_FORCE = 1000.0
NSA_SCALE = NSA_DH ** -0.5
NSA_KV_COLS = 2 * NSA_GROUPS * NSA_DH
NSA_QW = NSA_HEADS * NSA_DH

MOE_GROUPS = 4
MOE_EPG = 8
MOE_EXPERTS = MOE_GROUPS * MOE_EPG
MOE_TOPK = 2
MOE_DFF = 512
MOE_BLOCK = 128
ROUTER_PAD = LANES

IN_SPLITS = (GDN_CONV_DIM, GDN_VW, GDN_HEADS, GDN_HEADS,
             MLA_Q_LORA, MLA_KV_LORA, MLA_ROPE,
             NSA_HEADS * NSA_DH, NSA_KV_COLS, NSA_KV_COLS, NSA_KV_COLS, 3 * NSA_HEADS,
             3 * D_MODEL)
IN_OFF = [0]
for _w in IN_SPLITS:
    IN_OFF.append(IN_OFF[-1] + _w)

GDN_COLS = GDN_CONV_DIM + GDN_VW + LANES
MLA_COLS = MLA_Q_LORA + MLA_KV_LORA + 2 * LANES + LANES
NSA_COLS = NSA_QW + 3 * NSA_KV_COLS + LANES

VMEM_LIMIT = 52 * 1024 * 1024

_NT = (((1,), (1,)), ((), ()))
_TN = (((0,), (0,)), ((), ()))


def _cparams(sem):
    return pltpu.CompilerParams(dimension_semantics=sem, vmem_limit_bytes=VMEM_LIMIT)


def _block_diag_ones(n, grp):
    i = np.arange(n)
    return jnp.asarray((i[:, None] // grp) == (i[None, :] // grp), BF16)


def _group_sum(sq, bd):
    hi = sq.astype(BF16)
    lo = (sq - hi.astype(F32)).astype(BF16)
    return jnp.dot(hi, bd, preferred_element_type=F32) + jnp.dot(lo, bd, preferred_element_type=F32)


def _softmax_rows(s, mask):
    s = jnp.where(mask, s, NEG)
    m = jnp.max(s, axis=-1, keepdims=True)
    p = jnp.where(mask, jnp.exp(s - m), 0.0)
    return p / jnp.maximum(jnp.sum(p, axis=-1, keepdims=True), 1e-30)


def _silu(x):
    return x * jax.nn.sigmoid(x)


def _norm_matmul_kernel(x_ref, g_ref, w_ref, o_ref, h_ref):
    @pl.when(pl.program_id(1) == 0)
    def _():
        x = x_ref[...]
        ms = jnp.mean(x * x, axis=-1, keepdims=True)
        h_ref[...] = (x * lax.rsqrt(ms + NORM_EPS) * g_ref[...]).astype(BF16)

    o_ref[...] = jnp.dot(h_ref[...], w_ref[...], preferred_element_type=F32)


def norm_matmul(x, g, w, *, tm=512, tn=None):
    m, k = x.shape
    n = w.shape[1]
    tm = min(tm, m)
    tn = n if tn is None else tn
    assert m % tm == 0 and n % tn == 0
    return pl.pallas_call(
        _norm_matmul_kernel,
        grid=(m // tm, n // tn),
        in_specs=[pl.BlockSpec((tm, k), lambda i, j: (i, 0)),
                  pl.BlockSpec((1, k), lambda i, j: (0, 0)),
                  pl.BlockSpec((k, tn), lambda i, j: (0, j))],
        out_specs=pl.BlockSpec((tm, tn), lambda i, j: (i, j)),
        out_shape=jax.ShapeDtypeStruct((m, n), F32),
        scratch_shapes=[pltpu.VMEM((tm, k), BF16)],
        compiler_params=_cparams(("parallel", "arbitrary")),
        name="norm_matmul",
    )(x, g.reshape(1, k), w)


def _gdn_kernel(p_ref, hist_ref, s0_ref, cw_ref, alog_ref, dtb_ref, og_ref, o_ref, sfin_ref,
                prev_ref, s_ref, *, chunk, t_valid):
    c = pl.program_id(1)
    n_steps = pl.num_programs(1)
    qk, vw = GDN_QK, GDN_VW

    @pl.when(c == 0)
    def _():
        prev_ref[...] = hist_ref[0]
        s_ref[...] = s0_ref[0]

    raw = p_ref[0]
    xcat = jnp.concatenate([prev_ref[...], raw[:, :GDN_CONV_DIM]], axis=0)
    prev_ref[...] = xcat[chunk:chunk + 8]
    acc = xcat[8:8 + chunk] * cw_ref[GDN_CONV - 1:GDN_CONV, :]
    for i in range(GDN_CONV - 1):
        sh = pltpu.roll(xcat, GDN_CONV - 1 - i, 0)
        acc = acc + sh[8:8 + chunk] * cw_ref[i:i + 1, :]
    qkv = _silu(acc)

    ba = raw[:, GDN_CONV_DIM + GDN_VW:]
    beta_all = jax.nn.sigmoid(ba)
    xa = ba + dtb_ref[...]
    softplus = jnp.maximum(xa, 0.0) + jnp.log(1.0 + jnp.exp(-jnp.abs(xa)))
    g_all = -jnp.exp(alog_ref[...]) * softplus
    row = lax.broadcasted_iota(jnp.int32, (chunk, chunk), 0)
    col = lax.broadcasted_iota(jnp.int32, (chunk, chunk), 1)
    if t_valid < chunk:
        live = lax.broadcasted_iota(jnp.int32, (chunk, 1), 0) < t_valid
        beta_all = jnp.where(live, beta_all, 0.0)
        g_all = jnp.where(live, g_all, 0.0)
    incl = row >= col
    strict = row > col
    eye = row == col
    gc_all = jnp.dot(incl.astype(F32), g_all, preferred_element_type=F32, precision=HIGHEST)
    ones = jnp.ones((chunk, chunk), F32)
    n_fac = int(math.log2(chunk)) - 1

    for h in range(GDN_HEADS):
        q = qkv[:, h * GDN_DK:(h + 1) * GDN_DK]
        k = qkv[:, qk + h * GDN_DK:qk + (h + 1) * GDN_DK]
        v = qkv[:, 2 * qk + h * GDN_DV:2 * qk + (h + 1) * GDN_DV]
        q = q * lax.rsqrt(jnp.sum(q * q, axis=-1, keepdims=True) + NORM_EPS) * (GDN_DK ** -0.5)
        k = k * lax.rsqrt(jnp.sum(k * k, axis=-1, keepdims=True) + NORM_EPS)
        beta = beta_all[:, h:h + 1]
        gcol = gc_all[:, GDN_HEADS + h:GDN_HEADS + h + 1]
        gd = jnp.broadcast_to(gcol, (chunk, chunk))
        grow = jnp.dot(ones, jnp.where(eye, gd, 0.0), preferred_element_type=F32, precision=HIGHEST)
        decay = jnp.where(incl, jnp.exp(jnp.where(incl, gd - grow, 0.0)), 0.0)
        kb = k * beta
        kbf = k.astype(BF16)
        a_kk = jnp.where(strict, lax.dot_general(kb.astype(BF16), kbf, _NT, preferred_element_type=F32) * decay, 0.0)
        inv = jnp.where(eye, 1.0, 0.0) - a_kk
        pw = a_kk
        for _ in range(n_fac):
            pw = jnp.dot(pw, pw, preferred_element_type=F32, precision=HIGHEST)
            inv = inv + jnp.dot(inv, pw, preferred_element_type=F32, precision=HIGHEST)
        egc = jnp.exp(gcol)
        rhs = jnp.concatenate([v * beta, kb * egc], axis=1)
        sol = jnp.dot(inv, rhs, preferred_element_type=F32, precision=HIGHEST)
        u, w = sol[:, :GDN_DV], sol[:, GDN_DV:]
        a_qk = lax.dot_general(q.astype(BF16), kbf, _NT, preferred_element_type=F32) * decay
        s = s_ref[h]
        sb = s.astype(BF16)
        v_new = u - jnp.dot(w.astype(BF16), sb, preferred_element_type=F32)
        o = (jnp.dot((q * egc).astype(BF16), sb, preferred_element_type=F32)
             + jnp.dot(a_qk.astype(BF16), v_new.astype(BF16), preferred_element_type=F32))
        g_last = gcol[chunk - 1:chunk, :]
        kdec = k * jnp.exp(g_last - gcol)
        s_ref[h] = s * jnp.exp(g_last) + lax.dot_general(kdec.astype(BF16), v_new.astype(BF16), _TN,
                                                         preferred_element_type=F32)
        on = o * lax.rsqrt(jnp.mean(o * o, axis=-1, keepdims=True) + NORM_EPS) * og_ref[...]
        z = raw[:, GDN_CONV_DIM + h * GDN_DV:GDN_CONV_DIM + (h + 1) * GDN_DV]
        o_ref[0, :, h * GDN_DV:(h + 1) * GDN_DV] = on * _silu(z)

    @pl.when(c == n_steps - 1)
    def _():
        sfin_ref[0] = s_ref[...]


def gdn_mixer(p_gdn, hist8, s0, conv_w, a_log, dt_bias, out_g, *, chunk, t_valid):
    n, tp, _ = p_gdn.shape
    pad4 = jnp.zeros((GDN_HEADS,), F32)
    lane_row = lambda a: jnp.pad(jnp.concatenate([pad4, a]), (0, LANES - 2 * GDN_HEADS)).reshape(1, LANES)
    return pl.pallas_call(
        functools.partial(_gdn_kernel, chunk=chunk, t_valid=t_valid),
        grid=(n, tp // chunk),
        in_specs=[pl.BlockSpec((1, chunk, GDN_COLS), lambda b, c: (b, c, 0)),
                  pl.BlockSpec((1, 8, GDN_CONV_DIM), lambda b, c: (b, 0, 0)),
                  pl.BlockSpec((1, GDN_HEADS, GDN_DK, GDN_DV), lambda b, c: (b, 0, 0, 0)),
                  pl.BlockSpec((GDN_CONV, GDN_CONV_DIM), lambda b, c: (0, 0)),
                  pl.BlockSpec((1, LANES), lambda b, c: (0, 0)),
                  pl.BlockSpec((1, LANES), lambda b, c: (0, 0)),
                  pl.BlockSpec((1, GDN_DV), lambda b, c: (0, 0))],
        out_specs=[pl.BlockSpec((1, chunk, GDN_VW), lambda b, c: (b, c, 0)),
                   pl.BlockSpec((1, GDN_HEADS, GDN_DK, GDN_DV), lambda b, c: (b, 0, 0, 0))],
        out_shape=[jax.ShapeDtypeStruct((n, tp, GDN_VW), F32),
                   jax.ShapeDtypeStruct((n, GDN_HEADS, GDN_DK, GDN_DV), F32)],
        scratch_shapes=[pltpu.VMEM((8, GDN_CONV_DIM), F32), pltpu.VMEM((GDN_HEADS, GDN_DK, GDN_DV), F32)],
        compiler_params=_cparams(("parallel", "arbitrary")),
        name="gdn_mixer",
    )(p_gdn, hist8, s0, conv_w, lane_row(a_log), lane_row(dt_bias), out_g.reshape(1, GDN_DV))


def _mla_prep_kernel(p_ref, cos_ref, sin_ref, cqg_ref, ckvg_ref, wuq_ref, qng_ref, qrg_ref, krg_ref, wuk_ref,
                     bd64_ref, bd16_ref, qcat_ref, kcat_ref, ckv_ref, krope_ref):
    p = p_ref[...]
    cos = cos_ref[...]
    sin = sin_ref[...]
    lane = lax.broadcasted_iota(jnp.int32, (1, LANES), 1)

    cq = p[:, :MLA_Q_LORA]
    cq = cq * lax.rsqrt(jnp.mean(cq * cq, axis=-1, keepdims=True) + NORM_EPS) * cqg_ref[...]
    q = jnp.dot(cq.astype(BF16), wuq_ref[...], preferred_element_type=F32)
    nw = MLA_HEADS * MLA_NOPE
    nope = q[:, :nw]
    nope = nope * lax.rsqrt(_group_sum(nope * nope, bd64_ref[...]) * (1.0 / MLA_NOPE) + NORM_EPS) * qng_ref[...]
    x1 = q[:, nw:nw + LANES]
    x2 = q[:, nw + LANES:]
    r = lax.rsqrt(_group_sum(x1 * x1 + x2 * x2, bd16_ref[...]) * (1.0 / MLA_ROPE) + NORM_EPS)
    x1 = x1 * r * qrg_ref[0:1, :]
    x2 = x2 * r * qrg_ref[1:2, :]
    r1 = (x1 * cos - x2 * sin) * MLA_SCALE
    r2 = (x2 * cos + x1 * sin) * MLA_SCALE
    for h in range(MLA_HEADS):
        tile = nope[:, (h // 2) * LANES:(h // 2 + 1) * LANES].astype(BF16)
        q_lat = jnp.dot(tile, wuk_ref[h], preferred_element_type=F32) * MLA_SCALE
        mine = (lane // MLA_HALF) == h
        base = h * MLA_QK
        qcat_ref[:, base:base + MLA_KV_LORA] = q_lat.astype(BF16)
        qcat_ref[:, base + MLA_KV_LORA:base + MLA_KV_LORA + LANES] = jnp.where(mine, r1, 0.0).astype(BF16)
        qcat_ref[:, base + MLA_KV_LORA + LANES:base + MLA_QK] = jnp.where(mine, r2, 0.0).astype(BF16)

    ckv = p[:, MLA_Q_LORA:MLA_Q_LORA + MLA_KV_LORA]
    ckv = ckv * lax.rsqrt(jnp.mean(ckv * ckv, axis=-1, keepdims=True) + NORM_EPS) * ckvg_ref[...]
    ckv_ref[...] = ckv
    kr = p[:, MLA_Q_LORA + MLA_KV_LORA:MLA_Q_LORA + MLA_KV_LORA + 2 * LANES]
    ms = jnp.sum(kr * kr, axis=-1, keepdims=True) * (1.0 / (MLA_HEADS * MLA_ROPE))
    rk = lax.rsqrt(ms + NORM_EPS)
    k1 = kr[:, :LANES] * rk * krg_ref[0:1, :]
    k2 = kr[:, LANES:] * rk * krg_ref[1:2, :]
    k1r = k1 * cos - k2 * sin
    k2r = k2 * cos + k1 * sin
    kcat_ref[:, :MLA_KV_LORA] = ckv.astype(BF16)
    kcat_ref[:, MLA_KV_LORA:MLA_KV_LORA + LANES] = k1r.astype(BF16)
    kcat_ref[:, MLA_KV_LORA + LANES:] = k2r.astype(BF16)
    std = jnp.where(lane < MLA_HALF, k1r, pltpu.roll(k2r, MLA_HALF, 1))
    krope_ref[...] = std[:, :MLA_ROPE]


def mla_prep(p_mla, cos, sin, wts, *, tm):
    m = p_mla.shape[0]
    nt = cos.shape[0] // tm
    row = lambda w: pl.BlockSpec((tm, w), lambda i: (i, 0))
    res = lambda a: pl.BlockSpec(a.shape, lambda i: (0,) * a.ndim)
    tab = pl.BlockSpec((tm, LANES), lambda i: (i % nt, 0))
    consts = [wts['cq_g'], wts['ckv_g'], wts['w_uq'], wts['qn_g'], wts['qr_g'], wts['kr_g'], wts['w_uk'],
              wts['bd64_512'], wts['bd16']]
    return pl.pallas_call(
        _mla_prep_kernel,
        grid=(m // tm,),
        in_specs=[row(MLA_COLS), tab, tab] + [res(a) for a in consts],
        out_specs=[row(MLA_HEADS * MLA_QK), row(MLA_QK), row(MLA_KV_LORA), row(MLA_ROPE)],
        out_shape=[jax.ShapeDtypeStruct((m, MLA_HEADS * MLA_QK), BF16), jax.ShapeDtypeStruct((m, MLA_QK), BF16),
                   jax.ShapeDtypeStruct((m, MLA_KV_LORA), F32), jax.ShapeDtypeStruct((m, MLA_ROPE), F32)],
        compiler_params=_cparams(("parallel",)),
        name="mla_prep",
    )(p_mla, cos, sin, *consts)


def _mla_flash_kernel(qi_ref, kj_ref, q_ref, k_ref, wuv_ref, o_ref, m_ref, l_ref, acc_ref, *, tq, tk):
    p = pl.program_id(1)
    qi = qi_ref[p]
    kj = kj_ref[p]
    rows = MLA_HEADS * tq

    @pl.when(kj == 0)
    def _():
        m_ref[...] = jnp.full(m_ref.shape, NEG, F32)
        l_ref[...] = jnp.zeros(l_ref.shape, F32)
        acc_ref[...] = jnp.zeros(acc_ref.shape, F32)

    q = jnp.concatenate([q_ref[:, h * MLA_QK:(h + 1) * MLA_QK] for h in range(MLA_HEADS)], axis=0)
    k = k_ref[...]
    s = lax.dot_general(q, k, _NT, preferred_element_type=F32)
    qpos = qi * tq + (lax.broadcasted_iota(jnp.int32, (rows, 1), 0) & (tq - 1))
    kpos = kj * tk + lax.broadcasted_iota(jnp.int32, (1, tk), 1)
    s = jnp.where(kpos <= qpos, s, NEG)
    m_prev = m_ref[...]
    m_new = jnp.maximum(m_prev, jnp.max(s, axis=1, keepdims=True))
    alpha = jnp.exp(m_prev - m_new)
    pexp = jnp.exp(s - m_new)
    l_ref[...] = alpha * l_ref[...] + jnp.sum(pexp, axis=1, keepdims=True)
    acc_ref[...] = alpha * acc_ref[...] + jnp.dot(pexp.astype(BF16), k[:, :MLA_KV_LORA],
                                                  preferred_element_type=F32)
    m_ref[...] = m_new

    @pl.when(kj == (qi * tq + tq - 1) // tk)
    def _():
        lat = (acc_ref[...] / l_ref[...]).astype(BF16)
        outs = [jnp.dot(lat[h * tq:(h + 1) * tq], wuv_ref[h], preferred_element_type=F32)
                for h in range(MLA_HEADS)]
        o_ref[...] = jnp.concatenate(outs, axis=-1)


def mla_flash(q_cat, k_cat, w_uv, n, *, tq=128, tk=512):
    t = q_cat.shape[0] // n
    assert t % tq == 0 and t % tk == 0 and (tq & (tq - 1)) == 0
    pairs = [(i, j) for i in range(t // tq) for j in range((i * tq + tq - 1) // tk + 1)]
    qi_tab = jnp.asarray([a for a, _ in pairs], jnp.int32)
    kj_tab = jnp.asarray([b for _, b in pairs], jnp.int32)
    rows = MLA_HEADS * tq
    nq, nk = t // tq, t // tk
    grid_spec = pltpu.PrefetchScalarGridSpec(
        num_scalar_prefetch=2,
        grid=(n, len(pairs)),
        in_specs=[pl.BlockSpec((tq, MLA_HEADS * MLA_QK), lambda b, p, qi, kj: (b * nq + qi[p], 0)),
                  pl.BlockSpec((tk, MLA_QK), lambda b, p, qi, kj: (b * nk + kj[p], 0)),
                  pl.BlockSpec((MLA_HEADS, MLA_KV_LORA, MLA_V), lambda b, p, qi, kj: (0, 0, 0))],
        out_specs=pl.BlockSpec((tq, MLA_HEADS * MLA_V), lambda b, p, qi, kj: (b * nq + qi[p], 0)),
        scratch_shapes=[pltpu.VMEM((rows, 1), F32), pltpu.VMEM((rows, 1), F32),
                        pltpu.VMEM((rows, MLA_KV_LORA), F32)])
    return pl.pallas_call(
        functools.partial(_mla_flash_kernel, tq=tq, tk=tk),
        grid_spec=grid_spec,
        out_shape=jax.ShapeDtypeStruct((n * t, MLA_HEADS * MLA_V), F32),
        compiler_params=_cparams(("parallel", "arbitrary")),
        name="mla_flash",
    )(qi_tab, kj_tab, q_cat, k_cat, w_uv)


def _mla_decode_kernel(pt_ref, ql_ref, qr_ref, cnew_ref, rnew_ref, wuv_ref, *refs, pg, t_s):
    ck_refs = refs[:pg]
    kr_refs = refs[pg:2 * pg]
    o_ref, m_ref, l_ref, acc_ref = refs[2 * pg:]
    j = pl.program_id(1)
    rows = ql_ref.shape[1]
    tp = rows // MLA_HEADS

    @pl.when(j == 0)
    def _():
        m_ref[...] = jnp.full(m_ref.shape, NEG, F32)
        l_ref[...] = jnp.zeros(l_ref.shape, F32)
        acc_ref[...] = jnp.zeros(acc_ref.shape, F32)

    ql = ql_ref[0]
    qr = qr_ref[0]
    cks, ss = [], []
    for i in range(pg):
        ck = ck_refs[i][0, 0].astype(BF16)
        kr_t = kr_refs[i][0, 0].astype(BF16)
        cks.append(ck)
        ss.append(lax.dot_general(ql, ck, _NT, preferred_element_type=F32)
                  + jnp.dot(qr, kr_t, preferred_element_type=F32))
    s = jnp.concatenate(ss, axis=1)
    m_prev = m_ref[...]
    m_new = jnp.maximum(m_prev, jnp.max(s, axis=1, keepdims=True))
    alpha = jnp.exp(m_prev - m_new)
    pexp = jnp.exp(s - m_new)
    page = s.shape[1] // pg
    pv = jnp.dot(pexp[:, :page].astype(BF16), cks[0], preferred_element_type=F32)
    for i in range(1, pg):
        pv = pv + jnp.dot(pexp[:, i * page:(i + 1) * page].astype(BF16), cks[i], preferred_element_type=F32)
    l_ref[...] = alpha * l_ref[...] + jnp.sum(pexp, axis=1, keepdims=True)
    acc_ref[...] = alpha * acc_ref[...] + pv
    m_ref[...] = m_new

    @pl.when(j == pl.num_programs(1) - 1)
    def _():
        cn = cnew_ref[0]
        rn = rnew_ref[0]
        sn = (lax.dot_general(ql, cn, _NT, preferred_element_type=F32)
              + lax.dot_general(qr, rn, _NT, preferred_element_type=F32))
        tq = lax.broadcasted_iota(jnp.int32, (rows, 1), 0) // MLA_HEADS
        tk = lax.broadcasted_iota(jnp.int32, (1, tp), 1)
        sn = jnp.where((tk <= tq) & (tk < t_s), sn, NEG)
        m2 = jnp.maximum(m_ref[...], jnp.max(sn, axis=1, keepdims=True))
        a2 = jnp.exp(m_ref[...] - m2)
        pn = jnp.exp(sn - m2)
        l2 = a2 * l_ref[...] + jnp.sum(pn, axis=1, keepdims=True)
        acc2 = a2 * acc_ref[...] + jnp.dot(pn.astype(BF16), cn, preferred_element_type=F32)
        lat = (acc2 / l2).astype(BF16)
        full = jnp.dot(lat, wuv_ref[...], preferred_element_type=F32)
        hrow = lax.broadcasted_iota(jnp.int32, (rows, 1), 0) % MLA_HEADS
        hcol = lax.broadcasted_iota(jnp.int32, (1, MLA_HEADS * MLA_V), 1) // MLA_V
        full = jnp.where(hrow == hcol, full, 0.0)
        o_ref[0] = jnp.sum(full.reshape(tp, MLA_HEADS, MLA_HEADS * MLA_V), axis=1)


def mla_decode(page_table, q_lat, q_rope, ckv_new, kr_new, w_uv_all, cache_ckv, cache_kr_t, l, *, t_s, pg=16):
    b, rows, _ = q_lat.shape
    tp = rows // MLA_HEADS
    n_pages = page_table.shape[1]
    page = cache_ckv.shape[2]
    assert n_pages % pg == 0
    pt = page_table.reshape(-1)

    def page_spec(shape, i):
        return pl.BlockSpec((1, 1) + shape, lambda s, j, pt_ref: (l, pt_ref[s * n_pages + j * pg + i], 0, 0))

    seq = lambda a: pl.BlockSpec((1,) + a.shape[1:], lambda s, j, pt_ref: (s, 0, 0))
    grid_spec = pltpu.PrefetchScalarGridSpec(
        num_scalar_prefetch=1,
        grid=(b, n_pages // pg),
        in_specs=[seq(q_lat), seq(q_rope), seq(ckv_new), seq(kr_new),
                  pl.BlockSpec(w_uv_all.shape, lambda s, j, pt_ref: (0, 0))]
                 + [page_spec((page, MLA_KV_LORA), i) for i in range(pg)]
                 + [page_spec((MLA_ROPE, page), i) for i in range(pg)],
        out_specs=pl.BlockSpec((1, tp, MLA_HEADS * MLA_V), lambda s, j, pt_ref: (s, 0, 0)),
        scratch_shapes=[pltpu.VMEM((rows, 1), F32), pltpu.VMEM((rows, 1), F32),
                        pltpu.VMEM((rows, MLA_KV_LORA), F32)])
    return pl.pallas_call(
        functools.partial(_mla_decode_kernel, pg=pg, t_s=t_s),
        grid_spec=grid_spec,
        out_shape=jax.ShapeDtypeStruct((b, tp, MLA_HEADS * MLA_V), F32),
        compiler_params=_cparams(("parallel", "arbitrary")),
        name="mla_decode",
    )(pt, q_lat, q_rope, ckv_new, kr_new, w_uv_all, *([cache_ckv] * pg), *([cache_kr_t] * pg))


def _rope_lanes(x, c, s1, s2):
    n = x.shape[1]
    half = NSA_ROT // 2
    reps = n // LANES
    tile = (lambda a: jnp.concatenate([a] * reps, axis=1)) if reps > 1 else (lambda a: a)
    return x * tile(c) + pltpu.roll(x, n - half, 1) * tile(s1) + pltpu.roll(x, half, 1) * tile(s2)


def _nsa_prep_kernel(p_ref, c_ref, s1_ref, s2_ref, qng_ref, kng_ref, bdq_ref, bdk_ref, wt_ref, pbd_ref,
                     q_ref, kvc_ref, kvs_ref, kvw_ref, gate_ref, *cmp_out, compress):
    p = p_ref[...]
    c, s1, s2 = c_ref[...], s1_ref[...], s2_ref[...]
    q = p[:, :NSA_QW]
    q = q * lax.rsqrt(_group_sum(q * q, bdq_ref[...]) * (1.0 / NSA_DH) + NORM_EPS) * qng_ref[...]
    q_ref[...] = (_rope_lanes(q, c, s1, s2) * NSA_SCALE).astype(BF16)
    outs = (kvc_ref, kvs_ref, kvw_ref)
    kc = None
    for j in range(3):
        base = NSA_QW + j * NSA_KV_COLS
        k = p[:, base:base + LANES]
        k = k * lax.rsqrt(_group_sum(k * k, bdk_ref[...]) * (1.0 / NSA_DH) + NORM_EPS) * kng_ref[j:j + 1, :]
        k = _rope_lanes(k, c, s1, s2)
        outs[j][:, :LANES] = k
        outs[j][:, LANES:] = p[:, base + LANES:base + NSA_KV_COLS]
        if j == 0:
            kc = k
    gate_ref[...] = jax.nn.sigmoid(p[:, NSA_QW + 3 * NSA_KV_COLS:])
    if compress:
        tm = p.shape[0]
        rows = jnp.concatenate([kc, p[:, NSA_QW + LANES:NSA_QW + NSA_KV_COLS]], axis=1)
        pooled = jnp.sum((rows * wt_ref[...]).reshape(tm // NSA_CMP_BLOCK, NSA_CMP_BLOCK, NSA_KV_COLS), axis=1)
        cmp_out[0][...] = jnp.dot(pooled.astype(BF16), pbd_ref[...], preferred_element_type=F32)


def nsa_prep(p_nsa, tabs, wts, *, tm, compress):
    m = p_nsa.shape[0]
    nt = tabs[0].shape[0] // tm
    row = lambda w: pl.BlockSpec((tm, w), lambda i: (i, 0))
    res = lambda a: pl.BlockSpec(a.shape, lambda i: (0,) * a.ndim)
    tab = pl.BlockSpec((tm, LANES), lambda i: (i % nt, 0))
    consts = [wts['nsa_qn_g'], wts['nsa_kn_g'], wts['bd64_512'], wts['bd64_128'], wts['cmp_w'][:tm], wts['cmp_proj']]
    out_specs = [row(NSA_QW), row(NSA_KV_COLS), row(NSA_KV_COLS), row(NSA_KV_COLS), row(LANES)]
    out_shape = [jax.ShapeDtypeStruct((m, NSA_QW), BF16)] + [jax.ShapeDtypeStruct((m, NSA_KV_COLS), F32)] * 3 \
        + [jax.ShapeDtypeStruct((m, LANES), F32)]
    if compress:
        out_specs.append(pl.BlockSpec((tm // NSA_CMP_BLOCK, NSA_KV_COLS), lambda i: (i, 0)))
        out_shape.append(jax.ShapeDtypeStruct((m // NSA_CMP_BLOCK, NSA_KV_COLS), F32))
    return pl.pallas_call(
        functools.partial(_nsa_prep_kernel, compress=compress),
        grid=(m // tm,),
        in_specs=[row(NSA_COLS), tab, tab, tab] + [res(a) for a in consts],
        out_specs=out_specs,
        out_shape=out_shape,
        compiler_params=_cparams(("parallel",)),
        name="nsa_prep",
    )(p_nsa, *tabs, *consts)


def _group_rows(q, g):
    lane = lax.broadcasted_iota(jnp.int32, (1, LANES), 1)
    in_half = (lane // NSA_DH) == g
    zero = jnp.zeros((), q.dtype)
    parts = []
    for pidx in range(NSA_HPG):
        h = g * NSA_HPG + pidx
        tile = q[:, (h // 2) * LANES:(h // 2 + 1) * LANES]
        if h % 2 != g:
            tile = jnp.concatenate([tile[:, NSA_DH:], tile[:, :NSA_DH]], axis=1)
        parts.append(jnp.where(in_half, tile, zero))
    return jnp.concatenate(parts, axis=0)


def _place_heads(o_groups, t):
    lane = lax.broadcasted_iota(jnp.int32, (1, LANES), 1)
    low = lane < NSA_DH
    tiles = []
    for j in range(NSA_HEADS // 2):
        pair = []
        for h in (2 * j, 2 * j + 1):
            g, pidx = divmod(h, NSA_HPG)
            x = o_groups[g][pidx * t:(pidx + 1) * t]
            if h % 2 != g:
                x = jnp.concatenate([x[:, NSA_DH:], x[:, :NSA_DH]], axis=1)
            pair.append(x)
        tiles.append(jnp.where(low, pair[0], pair[1]))
    return jnp.concatenate(tiles, axis=1)


def _expand_gates(gs, exp_ref, j):
    hi = gs.astype(BF16)
    lo = (gs - hi.astype(F32)).astype(BF16)
    e = exp_ref[j]
    return jnp.dot(hi, e, preferred_element_type=F32) + jnp.dot(lo, e, preferred_element_type=F32)


def _top_blocks_t(score_t, blk_row_f, n_rows):
    sel_t = jnp.zeros(score_t.shape, F32)
    for _ in range(NSA_TOPK):
        mx = jnp.max(score_t, axis=0, keepdims=True)
        first = jnp.min(jnp.where(score_t == mx, blk_row_f, float(n_rows)), axis=0, keepdims=True)
        pick = blk_row_f == first
        sel_t = jnp.where(pick, 1.0, sel_t)
        score_t = jnp.where(pick, -3e38, score_t)
    return sel_t


def _nsa_prompt_kernel(q_ref, gate_ref, kvc_ref, kvs_ref, kvw_ref, pool_ref, exp_ref, o_ref, *, tq, tk, nc, n_sel):
    start = pl.program_id(1) * tq
    grows = NSA_HPG * tq
    q = q_ref[...]
    tokpos = start + lax.broadcasted_iota(jnp.int32, (tq, 1), 0)
    qpos = start + (lax.broadcasted_iota(jnp.int32, (grows, 1), 0) & (tq - 1))
    c_end = lax.broadcasted_iota(jnp.int32, (1, nc), 1) * NSA_CMP_BLOCK + (NSA_CMP_BLOCK - 1)
    blk_lane = lax.broadcasted_iota(jnp.int32, (1, LANES), 1)
    blk_row_f = lax.broadcasted_iota(jnp.int32, (LANES, tq), 0).astype(F32)
    cur = tokpos // NSA_SEL_BLOCK
    n_kt = (start + tq - 1) // tk + 1
    lo = jnp.maximum(start - NSA_WINDOW, 0)
    wlen = NSA_WINDOW + tq
    kvc = kvc_ref[0]
    kc = kvc[:, :LANES].astype(BF16)
    vc = kvc[:, LANES:].astype(BF16)
    o_cmp, o_sel, o_win = [], [], []

    for g in range(NSA_GROUPS):
        qg = _group_rows(q, g)

        sc = lax.dot_general(qg, kc, _NT, preferred_element_type=F32)
        pc = _softmax_rows(sc, c_end <= qpos)
        o_cmp.append(jnp.dot(pc.astype(BF16), vc, preferred_element_type=F32))

        psum = pc[0:tq]
        for h in range(1, NSA_HPG):
            psum = psum + pc[h * tq:(h + 1) * tq]
        imp = _group_sum(psum, pool_ref[...])
        forced = (blk_lane == 0) | (blk_lane == cur) | (blk_lane == cur - 1)
        score = jnp.where(blk_lane > cur, -1e9, imp + NSA_FORCE * forced.astype(F32))
        score = jnp.where(blk_lane < n_sel, score, -3e38)
        sel = _top_blocks_t(score.T, blk_row_f, LANES).T.astype(BF16)

        def sel_step(kt, carry):
            m_prev, l_prev, acc = carry
            off = pl.multiple_of(kt * tk, tk)
            kv = kvs_ref[0, pl.ds(off, tk), :]
            k = kv[:, :LANES].astype(BF16)
            v = kv[:, LANES:].astype(BF16)
            s = lax.dot_general(qg, k, _NT, preferred_element_type=F32)
            keyi = kt * tk + lax.broadcasted_iota(jnp.int32, (1, tk), 1)
            expand = (lax.broadcasted_iota(jnp.int32, (LANES, tk), 0) == keyi // NSA_SEL_BLOCK).astype(BF16)
            selm = jnp.dot(sel, expand, preferred_element_type=F32)
            valid = (selm > 0.5) & (keyi <= tokpos)
            s3 = jnp.where(valid[None], s.reshape(NSA_HPG, tq, tk), NEG)
            m_new = jnp.maximum(m_prev, jnp.max(s3, axis=-1, keepdims=True))
            alpha = jnp.exp(m_prev - m_new)
            p3 = jnp.exp(s3 - m_new)
            l_new = alpha * l_prev + jnp.sum(p3, axis=-1, keepdims=True)
            pv = jnp.dot(p3.reshape(grows, tk).astype(BF16), v, preferred_element_type=F32)
            return m_new, l_new, alpha * acc + pv.reshape(NSA_HPG, tq, LANES)

        init = (jnp.full((NSA_HPG, tq, 1), NEG, F32), jnp.zeros((NSA_HPG, tq, 1), F32),
                jnp.zeros((NSA_HPG, tq, LANES), F32))
        _, l_fin, acc = lax.fori_loop(0, n_kt, sel_step, init)
        o_sel.append((acc / l_fin).reshape(grows, LANES))

        woff = pl.multiple_of(lo, tq)
        kvw = kvw_ref[0, pl.ds(woff, wlen), :]
        sw = lax.dot_general(qg, kvw[:, :LANES].astype(BF16), _NT, preferred_element_type=F32)
        dpos = tokpos - (lo + lax.broadcasted_iota(jnp.int32, (1, wlen), 1))
        validw = (dpos >= 0) & (dpos < NSA_WINDOW)
        sw3 = sw.reshape(NSA_HPG, tq, wlen)
        pw = _softmax_rows(sw3, jnp.broadcast_to(validw[None], sw3.shape))
        o_win.append(jnp.dot(pw.reshape(grows, wlen).astype(BF16), kvw[:, LANES:].astype(BF16),
                             preferred_element_type=F32))

    gs = gate_ref[...]
    o_ref[...] = (_expand_gates(gs, exp_ref, 0) * _place_heads(o_cmp, tq)
                  + _expand_gates(gs, exp_ref, 1) * _place_heads(o_sel, tq)
                  + _expand_gates(gs, exp_ref, 2) * _place_heads(o_win, tq))


def nsa_prompt(q8, gates, kvc, kv_s, kv_w, pool, gate_exp, n, *, tq=128, tk=512):
    t = q8.shape[0] // n
    nc = kvc.shape[1]
    n_sel = t // NSA_SEL_BLOCK
    assert nc == LANES and 2 * n_sel == nc and t % tk == 0 and t >= NSA_WINDOW + tq
    nq = t // tq
    full = lambda a: pl.BlockSpec((1,) + a.shape[1:], lambda b, i: (b, 0, 0))
    res = lambda a: pl.BlockSpec(a.shape, lambda b, i: (0,) * a.ndim)
    return pl.pallas_call(
        functools.partial(_nsa_prompt_kernel, tq=tq, tk=tk, nc=nc, n_sel=n_sel),
        grid=(n, nq),
        in_specs=[pl.BlockSpec((tq, NSA_QW), lambda b, i: (b * nq + i, 0)),
                  pl.BlockSpec((tq, LANES), lambda b, i: (b * nq + i, 0)),
                  full(kvc), full(kv_s), full(kv_w), res(pool), res(gate_exp)],
        out_specs=pl.BlockSpec((tq, NSA_QW), lambda b, i: (b * nq + i, 0)),
        out_shape=jax.ShapeDtypeStruct((n * t, NSA_QW), F32),
        compiler_params=_cparams(("parallel", "arbitrary")),
        name="nsa_prompt",
    )(q8, gates, kvc, kv_s, kv_w, pool, gate_exp)


def _nsa_decode_kernel(pt_ref, q_ref, gate_ref, snew_ref, wnew_ref, win_ref, wt_ref, pbd_ref, pool_ref, exp_ref,
                       e0_ref, *refs, pg, n_steps, t_s, past):
    cmp_refs = refs[:pg]
    sel_refs = refs[pg:2 * pg]
    o_ref = refs[2 * pg]
    pooled_ref, selstep_ref, ocmp_ref, m_ref, l_ref, acc_ref = refs[2 * pg + 1:]
    j = pl.program_id(1)
    tp = q_ref.shape[1]
    grows = NSA_HPG * tp
    rows = NSA_GROUPS * grows
    page = cmp_refs[0].shape[3]
    nc = pooled_ref.shape[1]
    nblk = pool_ref.shape[1]
    bps = pg * page // NSA_SEL_BLOCK
    q = q_ref[0]
    qs = jnp.concatenate([_group_rows(q, g) for g in range(NSA_GROUPS)], axis=0)
    tokpos = past + lax.broadcasted_iota(jnp.int32, (tp, 1), 0)
    rowtok = lax.broadcasted_iota(jnp.int32, (rows, 1), 0) & (tp - 1)

    @pl.when(j == 0)
    def _():
        pooled_ref[...] = jnp.zeros(pooled_ref.shape, F32)

    @pl.when(j < n_steps)
    def _():
        cpp = page // NSA_CMP_BLOCK
        r_blk = lax.broadcasted_iota(jnp.int32, (page, nc), 0) // NSA_CMP_BLOCK
        c_idx = lax.broadcasted_iota(jnp.int32, (page, nc), 1)
        acc = pooled_ref[...]
        for i in range(pg):
            xw = cmp_refs[i][0, 0] * wt_ref[...]
            place = (c_idx == (j * pg + i) * cpp + r_blk).astype(BF16)
            acc = acc + _group_sum(xw, place)
        pooled_ref[...] = acc

    @pl.when(j == n_steps - 1)
    def _():
        kvc_t = jnp.dot(pbd_ref[...], pooled_ref[...].astype(BF16), preferred_element_type=F32)
        kc_t = kvc_t[:LANES].astype(BF16)
        vc_t = kvc_t[LANES:].astype(BF16)
        c_end = lax.broadcasted_iota(jnp.int32, (1, nc), 1) * NSA_CMP_BLOCK + (NSA_CMP_BLOCK - 1)
        sc = jnp.dot(qs, kc_t, preferred_element_type=F32)
        pc = _softmax_rows(sc, c_end <= past + rowtok)
        ocmp_ref[...] = lax.dot_general(pc.astype(BF16), vc_t, _NT, preferred_element_type=F32)
        blk_lane = lax.broadcasted_iota(jnp.int32, (1, nblk), 1)
        blk_lane_f = blk_lane.astype(F32)
        cur = tokpos // NSA_SEL_BLOCK
        n_sel = -(-(past + t_s) // NSA_SEL_BLOCK)
        forced = (blk_lane == 0) | (blk_lane == cur) | (blk_lane == cur - 1)
        sels = []
        for g in range(NSA_GROUPS):
            psum = pc[g * grows:g * grows + tp]
            for h in range(1, NSA_HPG):
                psum = psum + pc[g * grows + h * tp:g * grows + (h + 1) * tp]
            imp = _group_sum(psum, pool_ref[...])
            score = jnp.where(blk_lane > cur, -1e9, imp + NSA_FORCE * forced.astype(F32))
            score = jnp.where(blk_lane < n_sel, score, -3e38)
            sel = jnp.zeros((tp, nblk), F32)
            for _ in range(NSA_TOPK):
                mx = jnp.max(score, axis=1, keepdims=True)
                first = jnp.min(jnp.where(score == mx, blk_lane_f, float(nblk)), axis=1, keepdims=True)
                pick = blk_lane_f == first
                sel = jnp.where(pick, 1.0, sel)
                score = jnp.where(pick, -3e38, score)
            sels.append(jnp.concatenate([sel] * NSA_HPG, axis=0))
        sel_rows = jnp.concatenate(sels, axis=0)
        zpad = jnp.zeros((rows, LANES - bps), F32)
        for s in range(nblk // bps):
            selstep_ref[s] = jnp.concatenate([sel_rows[:, s * bps:(s + 1) * bps], zpad], axis=1)
        m_ref[...] = jnp.full(m_ref.shape, NEG, F32)
        l_ref[...] = jnp.zeros(l_ref.shape, F32)
        acc_ref[...] = jnp.zeros(acc_ref.shape, F32)

    def online(s, valid):
        s = jnp.where(valid, s, NEG)
        m_prev = m_ref[...]
        m_new = jnp.maximum(m_prev, jnp.max(s, axis=-1, keepdims=True))
        alpha = jnp.exp(m_prev - m_new)
        p = jnp.where(valid, jnp.exp(s - m_new), 0.0)
        l_ref[...] = alpha * l_ref[...] + jnp.sum(p, axis=-1, keepdims=True)
        m_ref[...] = m_new
        return alpha, p.astype(BF16)

    @pl.when(j >= n_steps)
    def _():
        jj = j - n_steps
        tiles = [r[0, 0] for r in sel_refs]
        s = jnp.concatenate([jnp.dot(qs, x[:LANES].astype(BF16), preferred_element_type=F32) for x in tiles], axis=1)
        selm = jnp.dot(selstep_ref[jj].astype(BF16), e0_ref[...], preferred_element_type=F32)
        alpha, p = online(s, selm > 0.5)
        pv = None
        for i, x in enumerate(tiles):
            term = lax.dot_general(p[:, i * page:(i + 1) * page], x[LANES:].astype(BF16), _NT,
                                   preferred_element_type=F32)
            pv = term if pv is None else pv + term
        acc_ref[...] = alpha * acc_ref[...] + pv

    @pl.when(j == 2 * n_steps - 1)
    def _():
        snew = snew_ref[0]
        wnew = wnew_ref[0]
        tk = lax.broadcasted_iota(jnp.int32, (1, tp), 1)
        causal_new = (tk <= rowtok) & (tk < t_s)
        new_blk = past // NSA_SEL_BLOCK
        lane = lax.broadcasted_iota(jnp.int32, (1, LANES), 1)
        last = selstep_ref[new_blk // bps]
        picked = jnp.sum(jnp.where(lane == new_blk % bps, last, 0.0), axis=1, keepdims=True) > 0.5
        s = lax.dot_general(qs, snew[:, :LANES].astype(BF16), _NT, preferred_element_type=F32)
        alpha, p = online(s, causal_new & picked)
        acc = alpha * acc_ref[...] + jnp.dot(p, snew[:, LANES:].astype(BF16), preferred_element_type=F32)
        o_sel = acc / jnp.maximum(l_ref[...], 1e-30)

        win = win_ref[0, 0]
        win_s = win.shape[1]
        wi = lax.broadcasted_iota(jnp.int32, (1, win_s), 1)
        dpos_c = (past + rowtok) - (past - win_s + wi)
        valid_c = (dpos_c >= 0) & (dpos_c < NSA_WINDOW)
        sc = jnp.dot(qs, win[:LANES].astype(BF16), preferred_element_type=F32)
        sn = lax.dot_general(qs, wnew[:, :LANES].astype(BF16), _NT, preferred_element_type=F32)
        sc = jnp.where(valid_c, sc, NEG)
        sn = jnp.where(causal_new, sn, NEG)
        mw = jnp.maximum(jnp.max(sc, axis=-1, keepdims=True), jnp.max(sn, axis=-1, keepdims=True))
        pcw = jnp.where(valid_c, jnp.exp(sc - mw), 0.0)
        pnw = jnp.where(causal_new, jnp.exp(sn - mw), 0.0)
        den = jnp.maximum(jnp.sum(pcw, axis=-1, keepdims=True) + jnp.sum(pnw, axis=-1, keepdims=True), 1e-30)
        o_win = (lax.dot_general(pcw.astype(BF16), win[LANES:].astype(BF16), _NT, preferred_element_type=F32)
                 + jnp.dot(pnw.astype(BF16), wnew[:, LANES:].astype(BF16), preferred_element_type=F32)) / den
        split = lambda a: [a[g * grows:(g + 1) * grows] for g in range(NSA_GROUPS)]
        gs = gate_ref[0]
        o_ref[0] = (_expand_gates(gs, exp_ref, 0) * _place_heads(split(ocmp_ref[...]), tp)
                    + _expand_gates(gs, exp_ref, 1) * _place_heads(split(o_sel), tp)
                    + _expand_gates(gs, exp_ref, 2) * _place_heads(split(o_win), tp))


def nsa_decode(page_table, q8, gates, kvs_new, kvw_new, cache_cmp_t, cache_sel_t, cache_win_t, wts, l, *, t_s, pg=16):
    b, tp, _ = q8.shape
    n_pages = page_table.shape[1]
    page = cache_cmp_t.shape[3]
    past = n_pages * page
    n_steps = n_pages // pg
    nc = past // NSA_CMP_BLOCK
    assert (past + t_s) // NSA_CMP_BLOCK == nc
    nblk = wts['pool_s'].shape[1]
    bps = pg * page // NSA_SEL_BLOCK
    assert nblk % bps == 0 and bps <= LANES and (past // NSA_SEL_BLOCK) // bps < nblk // bps
    rows = NSA_HEADS * tp
    pt = page_table.reshape(-1)
    e0 = jnp.asarray(np.arange(LANES)[:, None] == (np.arange(pg * page)[None, :] // NSA_SEL_BLOCK), BF16)

    def cmp_spec(i):
        return pl.BlockSpec((1, 1, NSA_KV_COLS, page),
                            lambda s, j, pt_ref: (l, pt_ref[s * n_pages + jnp.minimum(j, n_steps - 1) * pg + i], 0, 0))

    def sel_spec(i):
        return pl.BlockSpec((1, 1, NSA_KV_COLS, page),
                            lambda s, j, pt_ref: (l, pt_ref[s * n_pages + jnp.maximum(j - n_steps, 0) * pg + i], 0, 0))

    seq = lambda a: pl.BlockSpec((1,) + a.shape[1:], lambda s, j, pt_ref: (s, 0, 0))
    res = lambda a: pl.BlockSpec(a.shape, lambda s, j, pt_ref: (0,) * a.ndim)
    consts = [wts['cmp_w_t'], wts['cmp_proj_t'], wts['pool_s'], wts['gate_exp'], e0]
    grid_spec = pltpu.PrefetchScalarGridSpec(
        num_scalar_prefetch=1,
        grid=(b, 2 * n_steps),
        in_specs=[seq(q8), seq(gates), seq(kvs_new), seq(kvw_new),
                  pl.BlockSpec((1, 1) + cache_win_t.shape[2:], lambda s, j, pt_ref: (l, s, 0, 0))]
                 + [res(a) for a in consts]
                 + [cmp_spec(i) for i in range(pg)] + [sel_spec(i) for i in range(pg)],
        out_specs=pl.BlockSpec((1, tp, NSA_QW), lambda s, j, pt_ref: (s, 0, 0)),
        scratch_shapes=[pltpu.VMEM((NSA_KV_COLS, nc), F32),
                        pltpu.VMEM((nblk // bps, rows, LANES), F32),
                        pltpu.VMEM((rows, LANES), F32),
                        pltpu.VMEM((rows, 1), F32),
                        pltpu.VMEM((rows, 1), F32),
                        pltpu.VMEM((rows, LANES), F32)])
    return pl.pallas_call(
        functools.partial(_nsa_decode_kernel, pg=pg, n_steps=n_steps, t_s=t_s, past=past),
        grid_spec=grid_spec,
        out_shape=jax.ShapeDtypeStruct((b, tp, NSA_QW), F32),
        compiler_params=_cparams(("parallel", "arbitrary")),
        name="nsa_decode",
    )(pt, q8, gates, kvs_new, kvw_new, cache_win_t, *consts, *([cache_cmp_t] * pg), *([cache_sel_t] * pg))


def _merge_kernel(x_ref, oa_ref, ob_ref, oc_ref, br_ref, pa_ref, pb_ref, pc_ref, wo_ref, y_ref):
    d = D_MODEL
    br = br_ref[...]
    u = (jax.nn.sigmoid(br[:, 0:d]) * jnp.dot(oa_ref[...].astype(BF16), pa_ref[...], preferred_element_type=F32)
         + jax.nn.sigmoid(br[:, d:2 * d]) * jnp.dot(ob_ref[...].astype(BF16), pb_ref[...], preferred_element_type=F32)
         + jax.nn.sigmoid(br[:, 2 * d:3 * d]) * jnp.dot(oc_ref[...].astype(BF16), pc_ref[...], preferred_element_type=F32))
    y_ref[...] = x_ref[...] + jnp.dot(u.astype(BF16), wo_ref[...], preferred_element_type=F32)


def merge(x, o_a, o_b, o_c, br, p_a, p_b, p_c, w_out, *, tm=256):
    m, d = x.shape
    tm = min(tm, m)
    bw = o_a.shape[1]
    row = lambda w: pl.BlockSpec((tm, w), lambda i: (i, 0))
    res = lambda a: pl.BlockSpec(a.shape, lambda i: (0, 0))
    return pl.pallas_call(
        _merge_kernel,
        grid=(m // tm,),
        in_specs=[row(d), row(bw), row(bw), row(bw), row(3 * d), res(p_a), res(p_b), res(p_c), res(w_out)],
        out_specs=row(d),
        out_shape=jax.ShapeDtypeStruct((m, d), F32),
        compiler_params=_cparams(("parallel",)),
        name="merge",
    )(x, o_a, o_b, o_c, br, p_a, p_b, p_c, w_out)


def _router_kernel(x_ref, g_ref, wr_ref, h_ref, route_ref):
    x = x_ref[...]
    ms = jnp.mean(x * x, axis=-1, keepdims=True)
    hb = (x * lax.rsqrt(ms + NORM_EPS) * g_ref[...]).astype(BF16)
    h_ref[...] = hb
    logits = jnp.dot(hb, wr_ref[...], preferred_element_type=F32)
    lane = lax.broadcasted_iota(jnp.int32, (1, ROUTER_PAD), 1)
    lane_f = lane.astype(F32)
    big = float(ROUTER_PAD)

    def top1(vals):
        mx = jnp.max(vals, axis=-1, keepdims=True)
        return mx, jnp.min(jnp.where(vals == mx, lane_f, big), axis=-1, keepdims=True)

    gl = jnp.where(lane < MOE_GROUPS, logits, NEG)
    gmax, gidx = top1(gl)
    g_w = 1.0 / jnp.sum(jnp.where(lane < MOE_GROUPS, jnp.exp(gl - gmax), 0.0), axis=-1, keepdims=True)
    lo = MOE_GROUPS + MOE_EPG * gidx
    el = jnp.where((lane_f >= lo) & (lane_f < lo + MOE_EPG), logits, NEG)
    e1, i1 = top1(el)
    e2, i2 = top1(jnp.where(lane_f == i1, NEG, el))
    z = jnp.exp(e2 - e1)
    w1 = g_w / (1.0 + z)
    w2 = g_w * z / (1.0 + z)
    route_ref[...] = jnp.where(lane == 0, i1 - MOE_GROUPS,
                               jnp.where(lane == 1, i2 - MOE_GROUPS,
                                         jnp.where(lane == 2, w1, jnp.where(lane == 3, w2, 0.0))))


def norm_router(x, g, w_router, *, tm=512):
    m, d = x.shape
    tm = min(tm, m)
    return pl.pallas_call(
        _router_kernel,
        grid=(m // tm,),
        in_specs=[pl.BlockSpec((tm, d), lambda i: (i, 0)),
                  pl.BlockSpec((1, d), lambda i: (0, 0)),
                  pl.BlockSpec((d, ROUTER_PAD), lambda i: (0, 0))],
        out_specs=[pl.BlockSpec((tm, d), lambda i: (i, 0)),
                   pl.BlockSpec((tm, ROUTER_PAD), lambda i: (i, 0))],
        out_shape=[jax.ShapeDtypeStruct((m, d), BF16), jax.ShapeDtypeStruct((m, ROUTER_PAD), F32)],
        compiler_params=_cparams(("parallel",)),
        name="norm_router",
    )(x, g.reshape(1, d), w_router)


def _expert_kernel(be_ref, x_ref, rw_ref, w1_ref, w3_ref, w2_ref, o_ref):
    x = x_ref[...]
    a = jnp.dot(x, w1_ref[0], preferred_element_type=F32)
    b = jnp.dot(x, w3_ref[0], preferred_element_type=F32)
    hmid = _silu(a) * b
    y = jnp.dot(hmid.astype(BF16), w2_ref[0], preferred_element_type=F32)
    o_ref[...] = y * rw_ref[...]


def grouped_experts_blocks(xb, row_w, blk_e, w1, w3, w2):
    rows, d = xb.shape
    n_blk = rows // MOE_BLOCK
    grid_spec = pltpu.PrefetchScalarGridSpec(
        num_scalar_prefetch=1,
        grid=(n_blk,),
        in_specs=[pl.BlockSpec((MOE_BLOCK, d), lambda b, be: (b, 0)),
                  pl.BlockSpec((MOE_BLOCK, 1), lambda b, be: (b, 0)),
                  pl.BlockSpec((1, d, MOE_DFF), lambda b, be: (be[b], 0, 0)),
                  pl.BlockSpec((1, d, MOE_DFF), lambda b, be: (be[b], 0, 0)),
                  pl.BlockSpec((1, MOE_DFF, d), lambda b, be: (be[b], 0, 0))],
        out_specs=pl.BlockSpec((MOE_BLOCK, d), lambda b, be: (b, 0)))
    return pl.pallas_call(
        _expert_kernel,
        grid_spec=grid_spec,
        out_shape=jax.ShapeDtypeStruct((rows, d), F32),
        compiler_params=_cparams(("arbitrary",)),
        name="moe_experts",
    )(blk_e, xb, row_w.reshape(rows, 1), w1, w3, w2)


def hier_moe_residual(x, norm_g, w_router, w1, w3, w2):
    m, d = x.shape
    h, route = norm_router(x, norm_g, w_router)
    eid = route[:, :MOE_TOPK].astype(jnp.int32)
    ew = route[:, MOE_TOPK:2 * MOE_TOPK]

    n_assign = m * MOE_TOPK
    fe = eid.reshape(-1).astype(jnp.int32)
    onehot = (fe[:, None] == jnp.arange(MOE_EXPERTS, dtype=jnp.int32)[None, :]).astype(jnp.int32)
    csum = jnp.cumsum(onehot, axis=0)
    rank = jnp.sum((csum - onehot) * onehot, axis=1)
    counts = csum[-1]
    padded = (counts + MOE_BLOCK - 1) // MOE_BLOCK * MOE_BLOCK
    ends = jnp.cumsum(padded)
    dest = (ends - padded)[fe] + rank
    n_blk = -(-(n_assign + MOE_EXPERTS * (MOE_BLOCK - 1)) // MOE_BLOCK)
    rows = n_blk * MOE_BLOCK
    tok = jnp.arange(n_assign, dtype=jnp.int32) // MOE_TOPK
    row_tok = jnp.full((rows,), m, jnp.int32).at[dest].set(tok)
    row_w = jnp.zeros((rows,), F32).at[dest].set(ew.reshape(-1))
    blk_e = jnp.minimum(jnp.searchsorted(ends, jnp.arange(n_blk, dtype=jnp.int32) * MOE_BLOCK, side='right'),
                        MOE_EXPERTS - 1).astype(jnp.int32)
    xb = jnp.concatenate([h, jnp.zeros((1, d), h.dtype)], axis=0)[row_tok]
    yb = grouped_experts_blocks(xb, row_w, blk_e, w1, w3, w2)
    slot = dest.reshape(m, MOE_TOPK)
    y = yb[slot[:, 0]]
    for k in range(1, MOE_TOPK):
        y = y + yb[slot[:, k]]
    return x + y


def _rope_tables(pos):
    posf = pos.astype(F32)[:, None]
    inv_m = ROPE_THETA ** (-(jnp.arange(MLA_HALF, dtype=F32) * 2.0 / MLA_ROPE))
    ang_m = posf * inv_m[None, :]
    cos_m = jnp.tile(jnp.cos(ang_m), (1, LANES // MLA_HALF))
    sin_m = jnp.tile(jnp.sin(ang_m), (1, LANES // MLA_HALF))
    half = NSA_ROT // 2
    inv_n = ROPE_THETA ** (-(jnp.arange(half, dtype=F32) * 2.0 / NSA_ROT))
    ang_n = posf * inv_n[None, :]
    cn, sn = jnp.cos(ang_n), jnp.sin(ang_n)
    t = pos.shape[0]
    one = jnp.ones((t, NSA_DH - NSA_ROT), F32)
    zero = jnp.zeros((t, NSA_DH - NSA_ROT), F32)
    z8 = jnp.zeros((t, half), F32)
    c = jnp.concatenate([cn, cn, one], axis=1)
    s1 = jnp.concatenate([-sn, z8, zero], axis=1)
    s2 = jnp.concatenate([z8, sn, zero], axis=1)
    rep = lambda a: jnp.tile(a, (1, LANES // NSA_DH))
    return (cos_m, sin_m), (rep(c), rep(s1), rep(s2))


def _layer_weights(l, w_in, raw):
    w = {}
    wl = w_in[l]
    d = wl.shape[0]
    col = lambda i: wl[:, IN_OFF[i]:IN_OFF[i + 1]]
    zeros = lambda n: jnp.zeros((d, n), F32)
    w['w_gdn'] = jnp.concatenate([col(0), col(1), col(2), col(3), zeros(LANES - 2 * GDN_HEADS)], axis=1).astype(BF16)
    kr = col(6)
    kr_rep = jnp.concatenate([jnp.tile(kr[:, :MLA_HALF], (1, MLA_HEADS)), jnp.tile(kr[:, MLA_HALF:], (1, MLA_HEADS))],
                             axis=1)
    w['w_mla'] = jnp.concatenate([col(4), col(5), kr_rep, zeros(LANES)], axis=1).astype(BF16)
    w['w_nsa'] = jnp.concatenate([col(7), col(8), col(9), col(10), col(11), zeros(LANES - 3 * NSA_HEADS)],
                                 axis=1).astype(BF16)
    w['w_br'] = col(12).astype(BF16)

    uq = raw['mla_w_uq'][l].reshape(MLA_Q_LORA, MLA_HEADS, MLA_NOPE + MLA_ROPE)
    w['w_uq'] = jnp.concatenate([uq[:, :, :MLA_NOPE].reshape(MLA_Q_LORA, -1),
                                 uq[:, :, MLA_NOPE:MLA_NOPE + MLA_HALF].reshape(MLA_Q_LORA, -1),
                                 uq[:, :, MLA_NOPE + MLA_HALF:].reshape(MLA_Q_LORA, -1)], axis=1).astype(BF16)
    uk = jnp.transpose(raw['mla_w_uk'][l], (0, 2, 1))
    zk = jnp.zeros_like(uk)
    even = (jnp.arange(MLA_HEADS) % 2 == 0)[:, None, None]
    w['w_uk'] = jnp.concatenate([jnp.where(even, uk, zk), jnp.where(even, zk, uk)], axis=1).astype(BF16)
    w['w_uv'] = raw['mla_w_uv'][l].astype(BF16)
    w['w_uv_all'] = jnp.transpose(raw['mla_w_uv'][l], (1, 0, 2)).reshape(MLA_KV_LORA, MLA_HEADS * MLA_V).astype(BF16)
    w['cq_g'] = raw['mla_cq_g'][l].reshape(1, -1)
    w['ckv_g'] = raw['mla_ckv_g'][l].reshape(1, -1)
    w['qn_g'] = jnp.tile(raw['mla_qn_g'][l], MLA_HEADS).reshape(1, -1)
    split_tile = lambda g: jnp.stack([jnp.tile(g[:MLA_HALF], MLA_HEADS), jnp.tile(g[MLA_HALF:], MLA_HEADS)])
    w['qr_g'] = split_tile(raw['mla_qr_g'][l])
    w['kr_g'] = split_tile(raw['mla_kr_g'][l])
    w['bd64_512'] = _block_diag_ones(NSA_QW, NSA_DH)
    w['bd64_128'] = _block_diag_ones(LANES, NSA_DH)
    w['bd16'] = _block_diag_ones(LANES, MLA_HALF)

    w['nsa_qn_g'] = jnp.tile(raw['nsa_qn_g'][l], NSA_HEADS).reshape(1, -1)
    w['nsa_kn_g'] = jnp.tile(raw['nsa_kn_g'][l], (1, NSA_GROUPS))
    wpos = raw['nsa_cmp_wpos'][l]
    wrow = jnp.concatenate([jnp.tile(wpos[0], (1, NSA_GROUPS)), jnp.tile(wpos[1], (1, NSA_GROUPS))], axis=1)
    w['cmp_w'] = jnp.tile(wrow, (512 // NSA_CMP_BLOCK, 1))
    proj = raw['nsa_cmp_proj'][l]
    pbd = jnp.zeros((NSA_KV_COLS, NSA_KV_COLS), F32)
    for kv in range(2):
        for g in range(NSA_GROUPS):
            o = (kv * NSA_GROUPS + g) * NSA_DH
            pbd = pbd.at[o:o + NSA_DH, o:o + NSA_DH].set(proj[kv])
    w['cmp_proj'] = pbd.astype(BF16)
    w['cmp_w_t'] = jnp.tile(wrow.T, (1, LANES // NSA_CMP_BLOCK))
    w['cmp_proj_t'] = pbd.T.astype(BF16)
    ge = np.zeros((3, LANES, NSA_QW), np.float32)
    for j in range(3):
        for h in range(NSA_HEADS):
            ge[j, 3 * h + j, h * NSA_DH:(h + 1) * NSA_DH] = 1.0
    w['gate_exp'] = jnp.asarray(ge, BF16)

    w['p_gdn'] = raw['p_gdn'][l].astype(BF16)
    w['p_mla'] = raw['p_mla'][l].astype(BF16)
    w['p_nsa'] = raw['p_nsa'][l].astype(BF16)
    w['w_out'] = raw['w_out'][l].astype(BF16)
    w['w_router'] = jnp.pad(jnp.concatenate([raw['moe_w_group'][l], raw['moe_w_expert'][l]], axis=1),
                            ((0, 0), (0, ROUTER_PAD - MOE_GROUPS - MOE_EXPERTS))).astype(BF16)
    w['w1'] = raw['moe_w1'][l].astype(BF16)
    w['w3'] = raw['moe_w3'][l].astype(BF16)
    w['w2'] = raw['moe_w2'][l].astype(BF16)
    return w


def _pool_matrix(n_in, n_out):
    pm = np.zeros((n_in, n_out), np.float32)
    ratio = NSA_SEL_BLOCK // NSA_CMP_BLOCK
    for c in range(n_in):
        if c // ratio < n_out:
            pm[c, c // ratio] = 1.0
    return jnp.asarray(pm, BF16)


def kernel(x_prompt, x_sample, cache_mla_ckv, cache_mla_krope, cache_nsa_cmp, cache_nsa_sel, cache_nsa_win, state_gdn_ssm, state_gdn_conv, page_table, norm1_g, w_in, gdn_conv_w, gdn_a_log, gdn_dt_bias, gdn_out_g, mla_cq_g, mla_ckv_g, mla_w_uq, mla_qn_g, mla_qr_g, mla_kr_g, mla_w_uk, mla_w_uv, nsa_qn_g, nsa_kn_g, nsa_cmp_wpos, nsa_cmp_proj, p_gdn, p_mla, p_nsa, w_out, norm2_g, moe_w_group, moe_w_expert, moe_w1, moe_w3, moe_w2):
    depth = w_in.shape[0]
    b_p, t_p, d = x_prompt.shape
    b_s, t_s, _ = x_sample.shape
    n_pool, page = cache_mla_ckv.shape[1:3]
    past = page_table.shape[1] * page
    win_s = cache_nsa_win.shape[2]
    assert t_p >= GDN_CONV - 1 and t_s >= GDN_CONV - 1 and t_p % GDN_CHUNK == 0
    tp_s = 8
    assert t_s <= tp_s
    raw = dict(mla_w_uq=mla_w_uq, mla_w_uk=mla_w_uk, mla_w_uv=mla_w_uv, mla_cq_g=mla_cq_g, mla_ckv_g=mla_ckv_g,
               mla_qn_g=mla_qn_g, mla_qr_g=mla_qr_g, mla_kr_g=mla_kr_g, nsa_qn_g=nsa_qn_g, nsa_kn_g=nsa_kn_g,
               nsa_cmp_wpos=nsa_cmp_wpos, nsa_cmp_proj=nsa_cmp_proj, p_gdn=p_gdn, p_mla=p_mla, p_nsa=p_nsa,
               w_out=w_out, moe_w_group=moe_w_group, moe_w_expert=moe_w_expert, moe_w1=moe_w1, moe_w3=moe_w3,
               moe_w2=moe_w2)
    pos_p = jnp.arange(t_p, dtype=jnp.int32)
    pos_s = jnp.tile(past + jnp.arange(t_s, dtype=jnp.int32), b_s)
    mla_tab_p, nsa_tab_p = _rope_tables(pos_p)
    mla_tab_s, nsa_tab_s = _rope_tables(pos_s)
    pool_p = _pool_matrix(LANES, LANES)
    nc_s = past // NSA_CMP_BLOCK
    pool_s = _pool_matrix(nc_s, nc_s)
    feat_major = lambda a: jnp.transpose(a, (0, 1, 3, 4, 5, 2)).reshape(a.shape[0], a.shape[1], NSA_KV_COLS, a.shape[2])
    cache_cmp_t = feat_major(cache_nsa_cmp)
    cache_sel_t = feat_major(cache_nsa_sel)
    cache_win_t = feat_major(cache_nsa_win)
    cache_kr_t = jnp.transpose(cache_mla_krope, (0, 1, 3, 2))

    def project(x2, l, w):
        g = norm1_g[l]
        return (norm_matmul(x2, g, w['w_gdn']), norm_matmul(x2, g, w['w_mla']),
                norm_matmul(x2, g, w['w_nsa']), norm_matmul(x2, g, w['w_br']))

    def finish(x2, o_a, o_b, o_c, br, l, w):
        y = merge(x2, o_a, o_b, o_c, br, w['p_gdn'], w['p_mla'], w['p_nsa'], w['w_out'])
        return hier_moe_residual(y, norm2_g[l], w['w_router'], w['w1'], w['w3'], w['w2'])

    def prompt_layer(x, l, w):
        m = b_p * t_p
        x2 = x.reshape(m, d)
        pg_, pm_, pn_, br = project(x2, l, w)
        pg3 = pg_.reshape(b_p, t_p, GDN_COLS)
        o_a, ssm = gdn_mixer(pg3, jnp.zeros((b_p, 8, GDN_CONV_DIM), F32),
                             jnp.zeros((b_p, GDN_HEADS, GDN_DK, GDN_DV), F32),
                             gdn_conv_w[l], gdn_a_log[l], gdn_dt_bias[l], gdn_out_g[l],
                             chunk=GDN_CHUNK, t_valid=GDN_CHUNK)
        conv = pg3[:, t_p - (GDN_CONV - 1):, :GDN_CONV_DIM]
        q_cat, k_cat, ckv, krope = mla_prep(pm_, *mla_tab_p, w, tm=256)
        o_b = mla_flash(q_cat, k_cat, w['w_uv'], b_p)
        q8, kv_c, kv_s, kv_w, gates, kvc = nsa_prep(pn_, nsa_tab_p, w, tm=256, compress=True)
        o_c = nsa_prompt(q8, gates, kvc.reshape(b_p, t_p // NSA_CMP_BLOCK, NSA_KV_COLS),
                         kv_s.reshape(b_p, t_p, NSA_KV_COLS), kv_w.reshape(b_p, t_p, NSA_KV_COLS),
                         pool_p, w['gate_exp'], b_p)
        y = finish(x2, o_a.reshape(m, GDN_VW), o_b, o_c, br, l, w).reshape(b_p, t_p, d)
        kv5 = lambda a: a.reshape(b_p, t_p, 2, NSA_GROUPS, NSA_DH)
        kvw5 = kv5(kv_w)
        return y, (ssm, conv, ckv.reshape(b_p, t_p, MLA_KV_LORA), krope.reshape(b_p, t_p, MLA_ROPE),
                   kv5(kv_c), kv5(kv_s), kvw5[:, t_p - min(NSA_WINDOW, t_p):])

    def sample_layer(x, l, w):
        m = b_s * t_s
        x2 = x.reshape(m, d)
        pg_, pm_, pn_, br = project(x2, l, w)
        pad_t = lambda a: jnp.pad(a.reshape(b_s, t_s, -1), ((0, 0), (0, tp_s - t_s), (0, 0)))
        pg3 = pg_.reshape(b_s, t_s, GDN_COLS)
        hist8 = jnp.pad(state_gdn_conv[l], ((0, 0), (8 - (GDN_CONV - 1), 0), (0, 0)))
        o_a, ssm = gdn_mixer(pad_t(pg_), hist8, state_gdn_ssm[l], gdn_conv_w[l], gdn_a_log[l], gdn_dt_bias[l],
                             gdn_out_g[l], chunk=tp_s, t_valid=t_s)
        o_a = o_a[:, :t_s].reshape(m, GDN_VW)
        conv = pg3[:, t_s - (GDN_CONV - 1):, :GDN_CONV_DIM]

        q_cat, k_cat, ckv, krope = mla_prep(pm_, *mla_tab_s, w, tm=m)
        qc = q_cat.reshape(m, MLA_HEADS, MLA_QK)
        q_lat = qc[:, :, :MLA_KV_LORA]
        hsel = jnp.arange(MLA_HEADS)
        r1 = qc[:, :, MLA_KV_LORA:MLA_KV_LORA + LANES].reshape(m, MLA_HEADS, MLA_HEADS, MLA_HALF)[:, hsel, hsel]
        r2 = qc[:, :, MLA_KV_LORA + LANES:].reshape(m, MLA_HEADS, MLA_HEADS, MLA_HALF)[:, hsel, hsel]
        q_rope = jnp.concatenate([r1, r2], axis=-1)
        rows_of = lambda a: pad_t(a.reshape(m, -1)).reshape(b_s, tp_s * MLA_HEADS, -1)
        o_b = mla_decode(page_table, rows_of(q_lat), rows_of(q_rope), pad_t(ckv).astype(BF16),
                         pad_t(krope).astype(BF16), w['w_uv_all'], cache_mla_ckv, cache_kr_t, l, t_s=t_s)
        o_b = o_b[:, :t_s].reshape(m, MLA_HEADS * MLA_V)

        q8, kv_c, kv_s, kv_w, gates = nsa_prep(pn_, nsa_tab_s, w, tm=m, compress=False)
        ws = dict(w, pool_s=pool_s)
        o_c = nsa_decode(page_table, pad_t(q8), pad_t(gates), pad_t(kv_s), pad_t(kv_w), cache_cmp_t, cache_sel_t,
                         cache_win_t, ws, l, t_s=t_s)
        o_c = o_c[:, :t_s].reshape(m, NSA_QW)
        y = finish(x2, o_a, o_b, o_c, br, l, w).reshape(b_s, t_s, d)
        kv5 = lambda a: a.reshape(b_s, t_s, 2, NSA_GROUPS, NSA_DH)
        win_new = jnp.concatenate([cache_nsa_win[l], kv5(kv_w)], axis=1)[:, t_s:]
        return y, (ssm, conv, ckv.reshape(b_s, t_s, MLA_KV_LORA), krope.reshape(b_s, t_s, MLA_ROPE),
                   kv5(kv_c), kv5(kv_s), win_new)

    xp, xs = x_prompt, x_sample
    p_states, s_states = [], []
    for l in range(depth):
        w = _layer_weights(l, w_in, raw)
        xp, st = prompt_layer(xp, l, w)
        p_states.append(st)
        xs, st = sample_layer(xs, l, w)
        s_states.append(st)
    p_out = [jnp.stack([st[i] for st in p_states], axis=0) for i in range(7)]
    s_out = [jnp.stack([st[i] for st in s_states], axis=0) for i in range(7)]
    return (xp, xs, *p_out, *s_out)
```

```python
import functools
import math

import numpy as np
import jax
import jax.numpy as jnp
from jax import lax
from jax.experimental import pallas as pl
from jax.experimental.pallas import tpu as pltpu

F32 = jnp.float32
BF16 = jnp.bfloat16
HIGHEST = lax.Precision.HIGHEST

D_MODEL = 1024
ROPE_THETA = 500000.0
NORM_EPS = 1e-6
NEG = -1e30
LANES = 128

GDN_HEADS = 4
GDN_DK = 128
GDN_DV = 128
GDN_CONV = 4
GDN_CHUNK = 64
GDN_QK = GDN_HEADS * GDN_DK
GDN_VW = GDN_HEADS * GDN_DV
GDN_CONV_DIM = 2 * GDN_QK + GDN_VW

MLA_HEADS = 8
MLA_Q_LORA = 384
MLA_KV_LORA = 256
MLA_NOPE = 64
MLA_ROPE = 32
MLA_HALF = MLA_ROPE // 2
MLA_V = 64
MLA_SCALE = (MLA_NOPE + MLA_ROPE) ** -0.5
MLA_QK = MLA_KV_LORA + 2 * LANES

NSA_HEADS = 8
NSA_GROUPS = 2
NSA_HPG = NSA_HEADS // NSA_GROUPS
NSA_DH = 64
NSA_ROT = NSA_DH // 4
NSA_CMP_BLOCK = 32
NSA_SEL_BLOCK = 64
NSA_TOPK = 16
NSA_WINDOW = 512
NSA_FORCE = 1000.0
NSA_SCALE = NSA_DH ** -0.5
NSA_KV_COLS = 2 * NSA_GROUPS * NSA_DH
NSA_QW = NSA_HEADS * NSA_DH

MOE_GROUPS = 4
MOE_EPG = 8
MOE_EXPERTS = MOE_GROUPS * MOE_EPG
MOE_TOPK = 2
MOE_DFF = 512
MOE_BLOCK = 128
ROUTER_PAD = LANES

IN_SPLITS = (GDN_CONV_DIM, GDN_VW, GDN_HEADS, GDN_HEADS,
             MLA_Q_LORA, MLA_KV_LORA, MLA_ROPE,
             NSA_HEADS * NSA_DH, NSA_KV_COLS, NSA_KV_COLS, NSA_KV_COLS, 3 * NSA_HEADS,
             3 * D_MODEL)
IN_OFF = [0]
for _w in IN_SPLITS:
    IN_OFF.append(IN_OFF[-1] + _w)

GDN_COLS = GDN_CONV_DIM + GDN_VW + LANES
MLA_COLS = MLA_Q_LORA + MLA_KV_LORA + 2 * LANES + LANES
NSA_COLS = NSA_QW + 3 * NSA_KV_COLS + LANES

VMEM_LIMIT = 52 * 1024 * 1024

_NT = (((1,), (1,)), ((), ()))
_TN = (((0,), (0,)), ((), ()))


def _cparams(sem):
    return pltpu.CompilerParams(dimension_semantics=sem, vmem_limit_bytes=VMEM_LIMIT)


def _block_diag_ones(n, grp):
    i = np.arange(n)
    return jnp.asarray((i[:, None] // grp) == (i[None, :] // grp), BF16)


def _group_sum(sq, bd):
    hi = sq.astype(BF16)
    lo = (sq - hi.astype(F32)).astype(BF16)
    return jnp.dot(hi, bd, preferred_element_type=F32) + jnp.dot(lo, bd, preferred_element_type=F32)


def _softmax_rows(s, mask):
    s = jnp.where(mask, s, NEG)
    m = jnp.max(s, axis=-1, keepdims=True)
    p = jnp.where(mask, jnp.exp(s - m), 0.0)
    return p / jnp.maximum(jnp.sum(p, axis=-1, keepdims=True), 1e-30)


def _silu(x):
    return x * jax.nn.sigmoid(x)


def _norm_matmul_kernel(x_ref, g_ref, w_ref, o_ref, h_ref):
    @pl.when(pl.program_id(1) == 0)
    def _():
        x = x_ref[...]
        ms = jnp.mean(x * x, axis=-1, keepdims=True)
        h_ref[...] = (x * lax.rsqrt(ms + NORM_EPS) * g_ref[...]).astype(BF16)

    o_ref[...] = jnp.dot(h_ref[...], w_ref[...], preferred_element_type=F32)


def norm_matmul(x, g, w, *, tm=512, tn=None):
    m, k = x.shape
    n = w.shape[1]
    tm = min(tm, m)
    tn = n if tn is None else tn
    assert m % tm == 0 and n % tn == 0
    return pl.pallas_call(
        _norm_matmul_kernel,
        grid=(m // tm, n // tn),
        in_specs=[pl.BlockSpec((tm, k), lambda i, j: (i, 0)),
                  pl.BlockSpec((1, k), lambda i, j: (0, 0)),
                  pl.BlockSpec((k, tn), lambda i, j: (0, j))],
        out_specs=pl.BlockSpec((tm, tn), lambda i, j: (i, j)),
        out_shape=jax.ShapeDtypeStruct((m, n), F32),
        scratch_shapes=[pltpu.VMEM((tm, k), BF16)],
        compiler_params=_cparams(("parallel", "arbitrary")),
        name="norm_matmul",
    )(x, g.reshape(1, k), w)


def _gdn_kernel(p_ref, hist_ref, s0_ref, cw_ref, alog_ref, dtb_ref, og_ref, o_ref, sfin_ref,
                prev_ref, s_ref, *, chunk, t_valid, nb, cb):
    c = pl.program_id(1)
    n_steps = pl.num_programs(1)
    qk = GDN_QK
    span = cb * chunk
    dot32 = functools.partial(jnp.dot, preferred_element_type=F32, precision=HIGHEST)
    dotb = lambda a, b: jnp.dot(a.astype(BF16), b.astype(BF16), preferred_element_type=F32)

    @pl.when(c == 0)
    def _():
        prev_ref[...] = hist_ref[...]
        s_ref[...] = s0_ref[...]

    row = lax.broadcasted_iota(jnp.int32, (chunk, chunk), 0)
    col = lax.broadcasted_iota(jnp.int32, (chunk, chunk), 1)
    incl = row >= col
    strict = row > col
    eye_f = (row == col).astype(F32)
    incl_f = incl.astype(F32)
    live = (lax.broadcasted_iota(jnp.int32, (chunk, 1), 0) < t_valid) if t_valid < chunk else None
    n_fac = int(math.log2(chunk)) - 1

    chains = []
    raws, qkvs, betas, gcs, gcts = [], [], {}, {}, {}
    for s in range(nb):
        raw = p_ref[s]
        xcat = jnp.concatenate([prev_ref[s], raw[:, :GDN_CONV_DIM]], axis=0)
        prev_ref[s] = xcat[span:span + 8]
        acc = xcat[8:8 + span] * cw_ref[GDN_CONV - 1:GDN_CONV, :]
        for i in range(GDN_CONV - 1):
            sh = pltpu.roll(xcat, GDN_CONV - 1 - i, 0)
            acc = acc + sh[8:8 + span] * cw_ref[i:i + 1, :]
        raws.append(raw)
        qkvs.append(_silu(acc))
        ba = raw[:, GDN_CONV_DIM + GDN_VW:]
        xa = ba + dtb_ref[...]
        softplus = jnp.maximum(xa, 0.0) + jnp.log(1.0 + jnp.exp(-jnp.abs(xa)))
        beta_all = jax.nn.sigmoid(ba)
        g_all = -jnp.exp(alog_ref[...]) * softplus
        for ci in range(cb):
            bt = beta_all[ci * chunk:(ci + 1) * chunk]
            gt = g_all[ci * chunk:(ci + 1) * chunk]
            if live is not None:
                bt = jnp.where(live, bt, 0.0)
                gt = jnp.where(live, gt, 0.0)
            betas[(s, ci)] = bt
            gcs[(s, ci)] = dot32(incl_f, gt)
            gcts[(s, ci)] = lax.dot_general(gt, incl_f, (((0,), (1,)), ((), ())), preferred_element_type=F32,
                                            precision=HIGHEST)
    for ci in range(cb):
        for s in range(nb):
            for h in range(GDN_HEADS):
                chains.append((s, ci, h))
    nch = len(chains)

    def tile(s, ci, off):
        return qkvs[s][ci * chunk:(ci + 1) * chunk, off:off + GDN_DK]

    qs = [tile(s, ci, h * GDN_DK) for s, ci, h in chains]
    ks = [tile(s, ci, qk + h * GDN_DK) for s, ci, h in chains]
    vs = [tile(s, ci, 2 * qk + h * GDN_DV) for s, ci, h in chains]
    qs = [q * lax.rsqrt(jnp.sum(q * q, axis=-1, keepdims=True) + NORM_EPS) * (GDN_DK ** -0.5) for q in qs]
    ks = [k * lax.rsqrt(jnp.sum(k * k, axis=-1, keepdims=True) + NORM_EPS) for k in ks]
    beta = [betas[(s, ci)][:, h:h + 1] for s, ci, h in chains]
    gcol = [gcs[(s, ci)][:, GDN_HEADS + h:GDN_HEADS + h + 1] for s, ci, h in chains]
    grow = [gcts[(s, ci)][GDN_HEADS + h:GDN_HEADS + h + 1, :] for s, ci, h in chains]
    decay = [jnp.where(incl, jnp.exp(jnp.where(incl, gcol[i] - grow[i], 0.0)), 0.0) for i in range(nch)]
    kb = [ks[i] * beta[i] for i in range(nch)]
    kbf = [k.astype(BF16) for k in ks]
    a_kk = [jnp.where(strict, lax.dot_general(kb[i].astype(BF16), kbf[i], _NT, preferred_element_type=F32) * decay[i],
                      0.0) for i in range(nch)]
    inv = [eye_f - a for a in a_kk]
    pw = a_kk
    for _ in range(n_fac):
        pw = [dot32(p, p) for p in pw]
        inv = [inv[i] + dot32(inv[i], pw[i]) for i in range(nch)]
    egc = [jnp.exp(g) for g in gcol]
    sol = [dot32(inv[i], jnp.concatenate([vs[i] * beta[i], kb[i] * egc[i]], axis=1)) for i in range(nch)]
    a_qk = [lax.dot_general(qs[i].astype(BF16), kbf[i], _NT, preferred_element_type=F32) * decay[i]
            for i in range(nch)]
    qe = [(qs[i] * egc[i]).astype(BF16) for i in range(nch)]
    g_last = [g[chunk - 1:chunk, :] for g in gcol]
    kdec = [(ks[i] * jnp.exp(g_last[i] - gcol[i])).astype(BF16) for i in range(nch)]
    e_last = [jnp.exp(g) for g in g_last]

    per = nb * GDN_HEADS
    state = {(s, h): s_ref[s, h] for s in range(nb) for h in range(GDN_HEADS)}
    for ci in range(cb):
        idx = range(ci * per, (ci + 1) * per)
        sb = {i: state[(chains[i][0], chains[i][2])].astype(BF16) for i in idx}
        v_new = {i: sol[i][:, :GDN_DV] - jnp.dot(sol[i][:, GDN_DV:].astype(BF16), sb[i], preferred_element_type=F32)
                 for i in idx}
        o = {i: jnp.dot(qe[i], sb[i], preferred_element_type=F32) + dotb(a_qk[i], v_new[i]) for i in idx}
        for i in idx:
            s, _, h = chains[i]
            state[(s, h)] = state[(s, h)] * e_last[i] + lax.dot_general(kdec[i], v_new[i].astype(BF16), _TN,
                                                                        preferred_element_type=F32)
            on = o[i] * lax.rsqrt(jnp.mean(o[i] * o[i], axis=-1, keepdims=True) + NORM_EPS) * og_ref[...]
            z = raws[s][ci * chunk:(ci + 1) * chunk, GDN_CONV_DIM + h * GDN_DV:GDN_CONV_DIM + (h + 1) * GDN_DV]
            o_ref[s, ci * chunk:(ci + 1) * chunk, h * GDN_DV:(h + 1) * GDN_DV] = on * _silu(z)
    for (s, h), val in state.items():
        s_ref[s, h] = val

    @pl.when(c == n_steps - 1)
    def _():
        sfin_ref[...] = s_ref[...]


def gdn_mixer(p_gdn, hist8, s0, conv_w, a_log, dt_bias, out_g, *, chunk, t_valid, nb, cb):
    n, tp, _ = p_gdn.shape
    span = cb * chunk
    assert n % nb == 0 and tp % span == 0
    pad4 = jnp.zeros((GDN_HEADS,), F32)
    lane_row = lambda a: jnp.pad(jnp.concatenate([pad4, a]), (0, LANES - 2 * GDN_HEADS)).reshape(1, LANES)
    return pl.pallas_call(
        functools.partial(_gdn_kernel, chunk=chunk, t_valid=t_valid, nb=nb, cb=cb),
        grid=(n // nb, tp // span),
        in_specs=[pl.BlockSpec((nb, span, GDN_COLS), lambda b, c: (b, c, 0)),
                  pl.BlockSpec((nb, 8, GDN_CONV_DIM), lambda b, c: (b, 0, 0)),
                  pl.BlockSpec((nb, GDN_HEADS, GDN_DK, GDN_DV), lambda b, c: (b, 0, 0, 0)),
                  pl.BlockSpec((GDN_CONV, GDN_CONV_DIM), lambda b, c: (0, 0)),
                  pl.BlockSpec((1, LANES), lambda b, c: (0, 0)),
                  pl.BlockSpec((1, LANES), lambda b, c: (0, 0)),
                  pl.BlockSpec((1, GDN_DV), lambda b, c: (0, 0))],
        out_specs=[pl.BlockSpec((nb, span, GDN_VW), lambda b, c: (b, c, 0)),
                   pl.BlockSpec((nb, GDN_HEADS, GDN_DK, GDN_DV), lambda b, c: (b, 0, 0, 0))],
        out_shape=[jax.ShapeDtypeStruct((n, tp, GDN_VW), F32),
                   jax.ShapeDtypeStruct((n, GDN_HEADS, GDN_DK, GDN_DV), F32)],
        scratch_shapes=[pltpu.VMEM((nb, 8, GDN_CONV_DIM), F32), pltpu.VMEM((nb, GDN_HEADS, GDN_DK, GDN_DV), F32)],
        compiler_params=_cparams(("parallel", "arbitrary")),
        name="gdn_mixer",
    )(p_gdn, hist8, s0, conv_w, lane_row(a_log), lane_row(dt_bias), out_g.reshape(1, GDN_DV))


def _mla_prep_kernel(p_ref, cos_ref, sin_ref, cqg_ref, ckvg_ref, wuq_ref, qng_ref, qrg_ref, krg_ref, wuk_ref,
                     bd64_ref, bd16_ref, qcat_ref, kcat_ref, ckv_ref, krope_ref):
    p = p_ref[...]
    cos = cos_ref[...]
    sin = sin_ref[...]
    lane = lax.broadcasted_iota(jnp.int32, (1, LANES), 1)

    cq = p[:, :MLA_Q_LORA]
    cq = cq * lax.rsqrt(jnp.mean(cq * cq, axis=-1, keepdims=True) + NORM_EPS) * cqg_ref[...]
    q = jnp.dot(cq.astype(BF16), wuq_ref[...], preferred_element_type=F32)
    nw = MLA_HEADS * MLA_NOPE
    nope = q[:, :nw]
    nope = nope * lax.rsqrt(_group_sum(nope * nope, bd64_ref[...]) * (1.0 / MLA_NOPE) + NORM_EPS) * qng_ref[...]
    x1 = q[:, nw:nw + LANES]
    x2 = q[:, nw + LANES:]
    r = lax.rsqrt(_group_sum(x1 * x1 + x2 * x2, bd16_ref[...]) * (1.0 / MLA_ROPE) + NORM_EPS)
    x1 = x1 * r * qrg_ref[0:1, :]
    x2 = x2 * r * qrg_ref[1:2, :]
    r1 = x1 * cos - x2 * sin
    r2 = x2 * cos + x1 * sin
    for h in range(MLA_HEADS):
        tile = nope[:, (h // 2) * LANES:(h // 2 + 1) * LANES].astype(BF16)
        q_lat = jnp.dot(tile, wuk_ref[h], preferred_element_type=F32)
        mine = (lane // MLA_HALF) == h
        base = h * MLA_QK
        qcat_ref[:, base:base + MLA_KV_LORA] = q_lat.astype(BF16)
        qcat_ref[:, base + MLA_KV_LORA:base + MLA_KV_LORA + LANES] = jnp.where(mine, r1, 0.0).astype(BF16)
        qcat_ref[:, base + MLA_KV_LORA + LANES:base + MLA_QK] = jnp.where(mine, r2, 0.0).astype(BF16)

    ckv = p[:, MLA_Q_LORA:MLA_Q_LORA + MLA_KV_LORA]
    ckv = ckv * lax.rsqrt(jnp.mean(ckv * ckv, axis=-1, keepdims=True) + NORM_EPS) * ckvg_ref[...]
    ckv_ref[...] = ckv
    kr = p[:, MLA_Q_LORA + MLA_KV_LORA:MLA_Q_LORA + MLA_KV_LORA + 2 * LANES]
    ms = jnp.sum(kr * kr, axis=-1, keepdims=True) * (1.0 / (MLA_HEADS * MLA_ROPE))
    rk = lax.rsqrt(ms + NORM_EPS)
    k1 = kr[:, :LANES] * rk * krg_ref[0:1, :]
    k2 = kr[:, LANES:] * rk * krg_ref[1:2, :]
    k1r = k1 * cos - k2 * sin
    k2r = k2 * cos + k1 * sin
    kcat_ref[:, :MLA_KV_LORA] = ckv.astype(BF16)
    kcat_ref[:, MLA_KV_LORA:MLA_KV_LORA + LANES] = k1r.astype(BF16)
    kcat_ref[:, MLA_KV_LORA + LANES:] = k2r.astype(BF16)
    std = jnp.where(lane < MLA_HALF, k1r, pltpu.roll(k2r, MLA_HALF, 1))
    krope_ref[...] = std[:, :MLA_ROPE]


def mla_prep(p_mla, cos, sin, wts, *, tm):
    m = p_mla.shape[0]
    nt = cos.shape[0] // tm
    row = lambda w: pl.BlockSpec((tm, w), lambda i: (i, 0))
    res = lambda a: pl.BlockSpec(a.shape, lambda i: (0,) * a.ndim)
    tab = pl.BlockSpec((tm, LANES), lambda i: (i % nt, 0))
    consts = [wts['cq_g'], wts['ckv_g'], wts['w_uq'], wts['qn_g'], wts['qr_g'], wts['kr_g'], wts['w_uk'],
              wts['bd64_512'], wts['bd16']]
    return pl.pallas_call(
        _mla_prep_kernel,
        grid=(m // tm,),
        in_specs=[row(MLA_COLS), tab, tab] + [res(a) for a in consts],
        out_specs=[row(MLA_HEADS * MLA_QK), row(MLA_QK), row(MLA_KV_LORA), row(MLA_ROPE)],
        out_shape=[jax.ShapeDtypeStruct((m, MLA_HEADS * MLA_QK), BF16), jax.ShapeDtypeStruct((m, MLA_QK), BF16),
                   jax.ShapeDtypeStruct((m, MLA_KV_LORA), F32), jax.ShapeDtypeStruct((m, MLA_ROPE), F32)],
        compiler_params=_cparams(("parallel",)),
        name="mla_prep",
    )(p_mla, cos, sin, *consts)


def _mla_flash_kernel(qi_ref, kj_ref, q_ref, k_ref, wuv_ref, o_ref, m_ref, l_ref, acc_ref, *, tq, tk):
    p = pl.program_id(1)
    qi = qi_ref[p]
    kj = kj_ref[p]
    rows = MLA_HEADS * tq

    @pl.when(kj == 0)
    def _():
        m_ref[...] = jnp.full(m_ref.shape, NEG, F32)
        l_ref[...] = jnp.zeros(l_ref.shape, F32)
        acc_ref[...] = jnp.zeros(acc_ref.shape, F32)

    q = jnp.concatenate([q_ref[:, h * MLA_QK:(h + 1) * MLA_QK] for h in range(MLA_HEADS)], axis=0)
    k = k_ref[...]
    s = lax.dot_general(q, k, _NT, preferred_element_type=F32) * MLA_SCALE
    qpos = qi * tq + (lax.broadcasted_iota(jnp.int32, (rows, 1), 0) & (tq - 1))
    kpos = kj * tk + lax.broadcasted_iota(jnp.int32, (1, tk), 1)
    s = jnp.where(kpos <= qpos, s, NEG)
    m_prev = m_ref[...]
    m_new = jnp.maximum(m_prev, jnp.max(s, axis=1, keepdims=True))
    alpha = jnp.exp(m_prev - m_new)
    pexp = jnp.exp(s - m_new)
    l_ref[...] = alpha * l_ref[...] + jnp.sum(pexp, axis=1, keepdims=True)
    acc_ref[...] = alpha * acc_ref[...] + jnp.dot(pexp.astype(BF16), k[:, :MLA_KV_LORA],
                                                  preferred_element_type=F32)
    m_ref[...] = m_new

    @pl.when(kj == (qi * tq + tq - 1) // tk)
    def _():
        lat = (acc_ref[...] / l_ref[...]).astype(BF16)
        outs = [jnp.dot(lat[h * tq:(h + 1) * tq], wuv_ref[h], preferred_element_type=F32)
                for h in range(MLA_HEADS)]
        o_ref[...] = jnp.concatenate(outs, axis=-1)


def mla_flash(q_cat, k_cat, w_uv, n, *, tq=128, tk=512):
    t = q_cat.shape[0] // n
    assert t % tq == 0 and t % tk == 0 and (tq & (tq - 1)) == 0
    pairs = [(i, j) for i in range(t // tq) for j in range((i * tq + tq - 1) // tk + 1)]
    qi_tab = jnp.asarray([a for a, _ in pairs], jnp.int32)
    kj_tab = jnp.asarray([b for _, b in pairs], jnp.int32)
    rows = MLA_HEADS * tq
    nq, nk = t // tq, t // tk
    grid_spec = pltpu.PrefetchScalarGridSpec(
        num_scalar_prefetch=2,
        grid=(n, len(pairs)),
        in_specs=[pl.BlockSpec((tq, MLA_HEADS * MLA_QK), lambda b, p, qi, kj: (b * nq + qi[p], 0)),
                  pl.BlockSpec((tk, MLA_QK), lambda b, p, qi, kj: (b * nk + kj[p], 0)),
                  pl.BlockSpec((MLA_HEADS, MLA_KV_LORA, MLA_V), lambda b, p, qi, kj: (0, 0, 0))],
        out_specs=pl.BlockSpec((tq, MLA_HEADS * MLA_V), lambda b, p, qi, kj: (b * nq + qi[p], 0)),
        scratch_shapes=[pltpu.VMEM((rows, 1), F32), pltpu.VMEM((rows, 1), F32),
                        pltpu.VMEM((rows, MLA_KV_LORA), F32)])
    return pl.pallas_call(
        functools.partial(_mla_flash_kernel, tq=tq, tk=tk),
        grid_spec=grid_spec,
        out_shape=jax.ShapeDtypeStruct((n * t, MLA_HEADS * MLA_V), F32),
        compiler_params=_cparams(("parallel", "arbitrary")),
        name="mla_flash",
    )(qi_tab, kj_tab, q_cat, k_cat, w_uv)


def _mla_decode_kernel(pt_ref, ql_ref, qr_ref, cnew_ref, rnew_ref, wuv_ref, *refs, pg, t_s):
    ck_refs = refs[:pg]
    kr_refs = refs[pg:2 * pg]
    o_ref, m_ref, l_ref, acc_ref = refs[2 * pg:]
    j = pl.program_id(1)
    rows = ql_ref.shape[1]
    tp = rows // MLA_HEADS

    @pl.when(j == 0)
    def _():
        m_ref[...] = jnp.full(m_ref.shape, NEG, F32)
        l_ref[...] = jnp.zeros(l_ref.shape, F32)
        acc_ref[...] = jnp.zeros(acc_ref.shape, F32)

    ql = ql_ref[0]
    qr = qr_ref[0]
    cks, ss = [], []
    for i in range(pg):
        ck = ck_refs[i][0, 0].astype(BF16)
        kr_t = kr_refs[i][0, 0].astype(BF16)
        cks.append(ck)
        ss.append((lax.dot_general(ql, ck, _NT, preferred_element_type=F32)
                   + jnp.dot(qr, kr_t, preferred_element_type=F32)) * MLA_SCALE)
    s = jnp.concatenate(ss, axis=1)
    m_prev = m_ref[...]
    m_new = jnp.maximum(m_prev, jnp.max(s, axis=1, keepdims=True))
    alpha = jnp.exp(m_prev - m_new)
    pexp = jnp.exp(s - m_new)
    page = s.shape[1] // pg
    pv = jnp.dot(pexp[:, :page].astype(BF16), cks[0], preferred_element_type=F32)
    for i in range(1, pg):
        pv = pv + jnp.dot(pexp[:, i * page:(i + 1) * page].astype(BF16), cks[i], preferred_element_type=F32)
    l_ref[...] = alpha * l_ref[...] + jnp.sum(pexp, axis=1, keepdims=True)
    acc_ref[...] = alpha * acc_ref[...] + pv
    m_ref[...] = m_new

    @pl.when(j == pl.num_programs(1) - 1)
    def _():
        cn = cnew_ref[0]
        rn = rnew_ref[0]
        sn = (lax.dot_general(ql, cn, _NT, preferred_element_type=F32)
              + lax.dot_general(qr, rn, _NT, preferred_element_type=F32)) * MLA_SCALE
        tq = lax.broadcasted_iota(jnp.int32, (rows, 1), 0) // MLA_HEADS
        tk = lax.broadcasted_iota(jnp.int32, (1, tp), 1)
        sn = jnp.where((tk <= tq) & (tk < t_s), sn, NEG)
        m2 = jnp.maximum(m_ref[...], jnp.max(sn, axis=1, keepdims=True))
        a2 = jnp.exp(m_ref[...] - m2)
        pn = jnp.exp(sn - m2)
        l2 = a2 * l_ref[...] + jnp.sum(pn, axis=1, keepdims=True)
        acc2 = a2 * acc_ref[...] + jnp.dot(pn.astype(BF16), cn, preferred_element_type=F32)
        lat = (acc2 / l2).astype(BF16)
        full = jnp.dot(lat, wuv_ref[...], preferred_element_type=F32)
        hrow = lax.broadcasted_iota(jnp.int32, (rows, 1), 0) % MLA_HEADS
        hcol = lax.broadcasted_iota(jnp.int32, (1, MLA_HEADS * MLA_V), 1) // MLA_V
        full = jnp.where(hrow == hcol, full, 0.0)
        o_ref[0] = jnp.sum(full.reshape(tp, MLA_HEADS, MLA_HEADS * MLA_V), axis=1)


def mla_decode(page_table, q_lat, q_rope, ckv_new, kr_new, w_uv_all, cache_ckv, cache_kr_t, l, *, t_s, pg=16):
    b, rows, _ = q_lat.shape
    tp = rows // MLA_HEADS
    n_pages = page_table.shape[1]
    page = cache_ckv.shape[2]
    assert n_pages % pg == 0
    pt = page_table.reshape(-1)

    def page_spec(shape, i):
        return pl.BlockSpec((1, 1) + shape, lambda s, j, pt_ref: (l, pt_ref[s * n_pages + j * pg + i], 0, 0))

    seq = lambda a: pl.BlockSpec((1,) + a.shape[1:], lambda s, j, pt_ref: (s, 0, 0))
    grid_spec = pltpu.PrefetchScalarGridSpec(
        num_scalar_prefetch=1,
        grid=(b, n_pages // pg),
        in_specs=[seq(q_lat), seq(q_rope), seq(ckv_new), seq(kr_new),
                  pl.BlockSpec(w_uv_all.shape, lambda s, j, pt_ref: (0, 0))]
                 + [page_spec((page, MLA_KV_LORA), i) for i in range(pg)]
                 + [page_spec((MLA_ROPE, page), i) for i in range(pg)],
        out_specs=pl.BlockSpec((1, tp, MLA_HEADS * MLA_V), lambda s, j, pt_ref: (s, 0, 0)),
        scratch_shapes=[pltpu.VMEM((rows, 1), F32), pltpu.VMEM((rows, 1), F32),
                        pltpu.VMEM((rows, MLA_KV_LORA), F32)])
    return pl.pallas_call(
        functools.partial(_mla_decode_kernel, pg=pg, t_s=t_s),
        grid_spec=grid_spec,
        out_shape=jax.ShapeDtypeStruct((b, tp, MLA_HEADS * MLA_V), F32),
        compiler_params=_cparams(("parallel", "arbitrary")),
        name="mla_decode",
    )(pt, q_lat, q_rope, ckv_new, kr_new, w_uv_all, *([cache_ckv] * pg), *([cache_kr_t] * pg))


def _rope_lanes(x, c, s1, s2):
    n = x.shape[1]
    half = NSA_ROT // 2
    reps = n // LANES
    tile = (lambda a: jnp.concatenate([a] * reps, axis=1)) if reps > 1 else (lambda a: a)
    return x * tile(c) + pltpu.roll(x, n - half, 1) * tile(s1) + pltpu.roll(x, half, 1) * tile(s2)


def _nsa_prep_kernel(p_ref, c_ref, s1_ref, s2_ref, qng_ref, kng_ref, bdq_ref, bdk_ref, wt_ref, pbd_ref,
                     q_ref, kvc_ref, kvs_ref, kvw_ref, gate_ref, *cmp_out, compress):
    p = p_ref[...]
    c, s1, s2 = c_ref[...], s1_ref[...], s2_ref[...]
    q = p[:, :NSA_QW]
    q = q * lax.rsqrt(_group_sum(q * q, bdq_ref[...]) * (1.0 / NSA_DH) + NORM_EPS) * qng_ref[...]
    q_ref[...] = (_rope_lanes(q, c, s1, s2) * NSA_SCALE).astype(BF16)
    outs = (kvc_ref, kvs_ref, kvw_ref)
    kc = None
    for j in range(3):
        base = NSA_QW + j * NSA_KV_COLS
        k = p[:, base:base + LANES]
        k = k * lax.rsqrt(_group_sum(k * k, bdk_ref[...]) * (1.0 / NSA_DH) + NORM_EPS) * kng_ref[j:j + 1, :]
        k = _rope_lanes(k, c, s1, s2)
        outs[j][:, :LANES] = k
        outs[j][:, LANES:] = p[:, base + LANES:base + NSA_KV_COLS]
        if j == 0:
            kc = k
    gate_ref[...] = jax.nn.sigmoid(p[:, NSA_QW + 3 * NSA_KV_COLS:])
    if compress:
        tm = p.shape[0]
        rows = jnp.concatenate([kc, p[:, NSA_QW + LANES:NSA_QW + NSA_KV_COLS]], axis=1)
        pooled = jnp.sum((rows * wt_ref[...]).reshape(tm // NSA_CMP_BLOCK, NSA_CMP_BLOCK, NSA_KV_COLS), axis=1)
        cmp_out[0][...] = jnp.dot(pooled.astype(BF16), pbd_ref[...], preferred_element_type=F32)


def nsa_prep(p_nsa, tabs, wts, *, tm, compress):
    m = p_nsa.shape[0]
    nt = tabs[0].shape[0] // tm
    row = lambda w: pl.BlockSpec((tm, w), lambda i: (i, 0))
    res = lambda a: pl.BlockSpec(a.shape, lambda i: (0,) * a.ndim)
    tab = pl.BlockSpec((tm, LANES), lambda i: (i % nt, 0))
    consts = [wts['nsa_qn_g'], wts['nsa_kn_g'], wts['bd64_512'], wts['bd64_128'], wts['cmp_w'][:tm], wts['cmp_proj']]
    out_specs = [row(NSA_QW), row(NSA_KV_COLS), row(NSA_KV_COLS), row(NSA_KV_COLS), row(LANES)]
    out_shape = [jax.ShapeDtypeStruct((m, NSA_QW), BF16)] + [jax.ShapeDtypeStruct((m, NSA_KV_COLS), F32)] * 3 \
        + [jax.ShapeDtypeStruct((m, LANES), F32)]
    if compress:
        out_specs.append(pl.BlockSpec((tm // NSA_CMP_BLOCK, NSA_KV_COLS), lambda i: (i, 0)))
        out_shape.append(jax.ShapeDtypeStruct((m // NSA_CMP_BLOCK, NSA_KV_COLS), F32))
    return pl.pallas_call(
        functools.partial(_nsa_prep_kernel, compress=compress),
        grid=(m // tm,),
        in_specs=[row(NSA_COLS), tab, tab, tab] + [res(a) for a in consts],
        out_specs=out_specs,
        out_shape=out_shape,
        compiler_params=_cparams(("parallel",)),
        name="nsa_prep",
    )(p_nsa, *tabs, *consts)


def _group_rows(q, g):
    lane = lax.broadcasted_iota(jnp.int32, (1, LANES), 1)
    in_half = (lane // NSA_DH) == g
    zero = jnp.zeros((), q.dtype)
    parts = []
    for pidx in range(NSA_HPG):
        h = g * NSA_HPG + pidx
        tile = q[:, (h // 2) * LANES:(h // 2 + 1) * LANES]
        if h % 2 != g:
            tile = jnp.concatenate([tile[:, NSA_DH:], tile[:, :NSA_DH]], axis=1)
        parts.append(jnp.where(in_half, tile, zero))
    return jnp.concatenate(parts, axis=0)


def _place_heads(o_groups, t):
    lane = lax.broadcasted_iota(jnp.int32, (1, LANES), 1)
    low = lane < NSA_DH
    tiles = []
    for j in range(NSA_HEADS // 2):
        pair = []
        for h in (2 * j, 2 * j + 1):
            g, pidx = divmod(h, NSA_HPG)
            x = o_groups[g][pidx * t:(pidx + 1) * t]
            if h % 2 != g:
                x = jnp.concatenate([x[:, NSA_DH:], x[:, :NSA_DH]], axis=1)
            pair.append(x)
        tiles.append(jnp.where(low, pair[0], pair[1]))
    return jnp.concatenate(tiles, axis=1)


def _expand_gates(gs, exp_ref, j):
    hi = gs.astype(BF16)
    lo = (gs - hi.astype(F32)).astype(BF16)
    e = exp_ref[j]
    return jnp.dot(hi, e, preferred_element_type=F32) + jnp.dot(lo, e, preferred_element_type=F32)


def _top_blocks_t(score_t, blk_row_f, n_rows):
    sel_t = jnp.zeros(score_t.shape, F32)
    for _ in range(NSA_TOPK):
        mx = jnp.max(score_t, axis=0, keepdims=True)
        first = jnp.min(jnp.where(score_t == mx, blk_row_f, float(n_rows)), axis=0, keepdims=True)
        pick = blk_row_f == first
        sel_t = jnp.where(pick, 1.0, sel_t)
        score_t = jnp.where(pick, -3e38, score_t)
    return sel_t


def _nsa_prompt_kernel(q_ref, gate_ref, kvc_ref, kvs_ref, kvw_ref, pool_ref, exp_ref, o_ref, *, tq, tk, nc, n_sel):
    start = pl.program_id(1) * tq
    grows = NSA_HPG * tq
    q = q_ref[...]
    tokpos = start + lax.broadcasted_iota(jnp.int32, (tq, 1), 0)
    qpos = start + (lax.broadcasted_iota(jnp.int32, (grows, 1), 0) & (tq - 1))
    c_end = lax.broadcasted_iota(jnp.int32, (1, nc), 1) * NSA_CMP_BLOCK + (NSA_CMP_BLOCK - 1)
    blk_lane = lax.broadcasted_iota(jnp.int32, (1, LANES), 1)
    blk_row_f = lax.broadcasted_iota(jnp.int32, (LANES, tq), 0).astype(F32)
    cur = tokpos // NSA_SEL_BLOCK
    n_kt = (start + tq - 1) // tk + 1
    lo = jnp.maximum(start - NSA_WINDOW, 0)
    wlen = NSA_WINDOW + tq
    kvc = kvc_ref[0]
    kc = kvc[:, :LANES].astype(BF16)
    vc = kvc[:, LANES:].astype(BF16)
    o_cmp, o_sel, o_win = [], [], []

    for g in range(NSA_GROUPS):
        qg = _group_rows(q, g)

        sc = lax.dot_general(qg, kc, _NT, preferred_element_type=F32)
        pc = _softmax_rows(sc, c_end <= qpos)
        o_cmp.append(jnp.dot(pc.astype(BF16), vc, preferred_element_type=F32))

        psum = pc[0:tq]
        for h in range(1, NSA_HPG):
            psum = psum + pc[h * tq:(h + 1) * tq]
        imp = _group_sum(psum, pool_ref[...])
        forced = (blk_lane == 0) | (blk_lane == cur) | (blk_lane == cur - 1)
        score = jnp.where(blk_lane > cur, -1e9, imp + NSA_FORCE * forced.astype(F32))
        score = jnp.where(blk_lane < n_sel, score, -3e38)
        sel = _top_blocks_t(score.T, blk_row_f, LANES).T.astype(BF16)

        def sel_step(kt, carry):
            m_prev, l_prev, acc = carry
            off = pl.multiple_of(kt * tk, tk)
            kv = kvs_ref[0, pl.ds(off, tk), :]
            k = kv[:, :LANES].astype(BF16)
            v = kv[:, LANES:].astype(BF16)
            s = lax.dot_general(qg, k, _NT, preferred_element_type=F32)
            keyi = kt * tk + lax.broadcasted_iota(jnp.int32, (1, tk), 1)
            expand = (lax.broadcasted_iota(jnp.int32, (LANES, tk), 0) == keyi // NSA_SEL_BLOCK).astype(BF16)
            selm = jnp.dot(sel, expand, preferred_element_type=F32)
            valid = (selm > 0.5) & (keyi <= tokpos)
            s3 = jnp.where(valid[None], s.reshape(NSA_HPG, tq, tk), NEG)
            m_new = jnp.maximum(m_prev, jnp.max(s3, axis=-1, keepdims=True))
            alpha = jnp.exp(m_prev - m_new)
            p3 = jnp.exp(s3 - m_new)
            l_new = alpha * l_prev + jnp.sum(p3, axis=-1, keepdims=True)
            pv = jnp.dot(p3.reshape(grows, tk).astype(BF16), v, preferred_element_type=F32)
            return m_new, l_new, alpha * acc + pv.reshape(NSA_HPG, tq, LANES)

        init = (jnp.full((NSA_HPG, tq, 1), NEG, F32), jnp.zeros((NSA_HPG, tq, 1), F32),
                jnp.zeros((NSA_HPG, tq, LANES), F32))
        _, l_fin, acc = lax.fori_loop(0, n_kt, sel_step, init)
        o_sel.append((acc / l_fin).reshape(grows, LANES))

        woff = pl.multiple_of(lo, tq)
        kvw = kvw_ref[0, pl.ds(woff, wlen), :]
        sw = lax.dot_general(qg, kvw[:, :LANES].astype(BF16), _NT, preferred_element_type=F32)
        dpos = tokpos - (lo + lax.broadcasted_iota(jnp.int32, (1, wlen), 1))
        validw = (dpos >= 0) & (dpos < NSA_WINDOW)
        sw3 = sw.reshape(NSA_HPG, tq, wlen)
        pw = _softmax_rows(sw3, jnp.broadcast_to(validw[None], sw3.shape))
        o_win.append(jnp.dot(pw.reshape(grows, wlen).astype(BF16), kvw[:, LANES:].astype(BF16),
                             preferred_element_type=F32))

    gs = gate_ref[...]
    o_ref[...] = (_expand_gates(gs, exp_ref, 0) * _place_heads(o_cmp, tq)
                  + _expand_gates(gs, exp_ref, 1) * _place_heads(o_sel, tq)
                  + _expand_gates(gs, exp_ref, 2) * _place_heads(o_win, tq))


def nsa_prompt(q8, gates, kvc, kv_s, kv_w, pool, gate_exp, n, *, tq=128, tk=512):
    t = q8.shape[0] // n
    nc = kvc.shape[1]
    n_sel = t // NSA_SEL_BLOCK
    assert nc == LANES and 2 * n_sel == nc and t % tk == 0 and t >= NSA_WINDOW + tq
    nq = t // tq
    full = lambda a: pl.BlockSpec((1,) + a.shape[1:], lambda b, i: (b, 0, 0))
    res = lambda a: pl.BlockSpec(a.shape, lambda b, i: (0,) * a.ndim)
    return pl.pallas_call(
        functools.partial(_nsa_prompt_kernel, tq=tq, tk=tk, nc=nc, n_sel=n_sel),
        grid=(n, nq),
        in_specs=[pl.BlockSpec((tq, NSA_QW), lambda b, i: (b * nq + i, 0)),
                  pl.BlockSpec((tq, LANES), lambda b, i: (b * nq + i, 0)),
                  full(kvc), full(kv_s), full(kv_w), res(pool), res(gate_exp)],
        out_specs=pl.BlockSpec((tq, NSA_QW), lambda b, i: (b * nq + i, 0)),
        out_shape=jax.ShapeDtypeStruct((n * t, NSA_QW), F32),
        compiler_params=_cparams(("parallel", "arbitrary")),
        name="nsa_prompt",
    )(q8, gates, kvc, kv_s, kv_w, pool, gate_exp)


def _nsa_decode_kernel(pt_ref, q_ref, gate_ref, snew_ref, wnew_ref, win_ref, wt_ref, pbd_ref, pool_ref, exp_ref,
                       e0_ref, *refs, pg, n_steps, t_s, past):
    cmp_refs = refs[:pg]
    sel_refs = refs[pg:2 * pg]
    o_ref = refs[2 * pg]
    pooled_ref, selstep_ref, ocmp_ref, m_ref, l_ref, acc_ref = refs[2 * pg + 1:]
    j = pl.program_id(1)
    tp = q_ref.shape[1]
    grows = NSA_HPG * tp
    rows = NSA_GROUPS * grows
    page = cmp_refs[0].shape[3]
    nc = past // NSA_CMP_BLOCK
    nblk = pool_ref.shape[1]
    bps = pg * page // NSA_SEL_BLOCK
    q = q_ref[0]
    qs = jnp.concatenate([_group_rows(q, g) for g in range(NSA_GROUPS)], axis=0)
    tokpos = past + lax.broadcasted_iota(jnp.int32, (tp, 1), 0)
    rowtok = lax.broadcasted_iota(jnp.int32, (rows, 1), 0) & (tp - 1)

    cps = pg * page // NSA_CMP_BLOCK

    @pl.when(j < n_steps)
    def _():
        xw = jnp.concatenate([r[0, 0] * wt_ref[...] for r in cmp_refs], axis=1)
        pooled_ref[jnp.minimum(j, n_steps - 1)] = _group_sum(xw, e0_ref[...])

    @pl.when(j == n_steps - 1)
    def _():
        pooled = jnp.concatenate([pooled_ref[s][:, :cps] for s in range(n_steps)], axis=1)
        kvc_t = jnp.dot(pbd_ref[...], pooled.astype(BF16), preferred_element_type=F32)
        kc_t = kvc_t[:LANES].astype(BF16)
        vc_t = kvc_t[LANES:].astype(BF16)
        c_end = lax.broadcasted_iota(jnp.int32, (1, nc), 1) * NSA_CMP_BLOCK + (NSA_CMP_BLOCK - 1)
        sc = jnp.dot(qs, kc_t, preferred_element_type=F32)
        pc = _softmax_rows(sc, c_end <= past + rowtok)
        ocmp_ref[...] = lax.dot_general(pc.astype(BF16), vc_t, _NT, preferred_element_type=F32)
        blk_lane = lax.broadcasted_iota(jnp.int32, (1, nblk), 1)
        blk_lane_f = blk_lane.astype(F32)
        cur = tokpos // NSA_SEL_BLOCK
        n_sel = -(-(past + t_s) // NSA_SEL_BLOCK)
        forced = (blk_lane == 0) | (blk_lane == cur) | (blk_lane == cur - 1)
        sels = []
        for g in range(NSA_GROUPS):
            psum = pc[g * grows:g * grows + tp]
            for h in range(1, NSA_HPG):
                psum = psum + pc[g * grows + h * tp:g * grows + (h + 1) * tp]
            imp = _group_sum(psum, pool_ref[...])
            score = jnp.where(blk_lane > cur, -1e9, imp + NSA_FORCE * forced.astype(F32))
            score = jnp.where(blk_lane < n_sel, score, -3e38)
            sel = jnp.zeros((tp, nblk), F32)
            for _ in range(NSA_TOPK):
                mx = jnp.max(score, axis=1, keepdims=True)
                first = jnp.min(jnp.where(score == mx, blk_lane_f, float(nblk)), axis=1, keepdims=True)
                pick = blk_lane_f == first
                sel = jnp.where(pick, 1.0, sel)
                score = jnp.where(pick, -3e38, score)
            sels.append(jnp.concatenate([sel] * NSA_HPG, axis=0))
        sel_rows = jnp.concatenate(sels, axis=0)
        zpad = jnp.zeros((rows, LANES - bps), F32)
        for s in range(nblk // bps):
            selstep_ref[s] = jnp.concatenate([sel_rows[:, s * bps:(s + 1) * bps], zpad], axis=1)
        m_ref[...] = jnp.full(m_ref.shape, NEG, F32)
        l_ref[...] = jnp.zeros(l_ref.shape, F32)
        acc_ref[...] = jnp.zeros(acc_ref.shape, F32)

    def online(s, valid):
        s = jnp.where(valid, s, NEG)
        m_prev = m_ref[...]
        m_new = jnp.maximum(m_prev, jnp.max(s, axis=-1, keepdims=True))
        alpha = jnp.exp(m_prev - m_new)
        p = jnp.where(valid, jnp.exp(s - m_new), 0.0)
        l_ref[...] = alpha * l_ref[...] + jnp.sum(p, axis=-1, keepdims=True)
        m_ref[...] = m_new
        return alpha, p.astype(BF16)

    @pl.when(j >= n_steps)
    def _():
        jj = j - n_steps
        tiles = [r[0, 0] for r in sel_refs]
        s = jnp.concatenate([jnp.dot(qs, x[:LANES].astype(BF16), preferred_element_type=F32) for x in tiles], axis=1)
        tile = selstep_ref[jj]
        lane = lax.broadcasted_iota(jnp.int32, (1, page), 1)
        bpp = page // NSA_SEL_BLOCK
        masks = []
        for i in range(pg):
            m_i = tile[:, i * bpp:i * bpp + 1]
            for b in range(1, bpp):
                m_i = jnp.where(lane >= b * NSA_SEL_BLOCK, tile[:, i * bpp + b:i * bpp + b + 1], m_i)
            masks.append(jnp.broadcast_to(m_i, (rows, page)))
        valid = jnp.concatenate(masks, axis=1) > 0.5
        alpha, p = online(s, valid)
        pv = None
        for i, x in enumerate(tiles):
            term = lax.dot_general(p[:, i * page:(i + 1) * page], x[LANES:].astype(BF16), _NT,
                                   preferred_element_type=F32)
            pv = term if pv is None else pv + term
        acc_ref[...] = alpha * acc_ref[...] + pv

    @pl.when(j == 2 * n_steps - 1)
    def _():
        snew = snew_ref[0]
        wnew = wnew_ref[0]
        tk = lax.broadcasted_iota(jnp.int32, (1, tp), 1)
        causal_new = (tk <= rowtok) & (tk < t_s)
        new_blk = past // NSA_SEL_BLOCK
        lane = lax.broadcasted_iota(jnp.int32, (1, LANES), 1)
        last = selstep_ref[new_blk // bps]
        picked = jnp.sum(jnp.where(lane == new_blk % bps, last, 0.0), axis=1, keepdims=True) > 0.5
        s = lax.dot_general(qs, snew[:, :LANES].astype(BF16), _NT, preferred_element_type=F32)
        alpha, p = online(s, causal_new & picked)
        acc = alpha * acc_ref[...] + jnp.dot(p, snew[:, LANES:].astype(BF16), preferred_element_type=F32)
        o_sel = acc / jnp.maximum(l_ref[...], 1e-30)

        win = win_ref[0, 0]
        win_s = win.shape[1]
        wi = lax.broadcasted_iota(jnp.int32, (1, win_s), 1)
        dpos_c = (past + rowtok) - (past - win_s + wi)
        valid_c = (dpos_c >= 0) & (dpos_c < NSA_WINDOW)
        sc = jnp.dot(qs, win[:LANES].astype(BF16), preferred_element_type=F32)
        sn = lax.dot_general(qs, wnew[:, :LANES].astype(BF16), _NT, preferred_element_type=F32)
        sc = jnp.where(valid_c, sc, NEG)
        sn = jnp.where(causal_new, sn, NEG)
        mw = jnp.maximum(jnp.max(sc, axis=-1, keepdims=True), jnp.max(sn, axis=-1, keepdims=True))
        pcw = jnp.where(valid_c, jnp.exp(sc - mw), 0.0)
        pnw = jnp.where(causal_new, jnp.exp(sn - mw), 0.0)
        den = jnp.maximum(jnp.sum(pcw, axis=-1, keepdims=True) + jnp.sum(pnw, axis=-1, keepdims=True), 1e-30)
        o_win = (lax.dot_general(pcw.astype(BF16), win[LANES:].astype(BF16), _NT, preferred_element_type=F32)
                 + jnp.dot(pnw.astype(BF16), wnew[:, LANES:].astype(BF16), preferred_element_type=F32)) / den
        split = lambda a: [a[g * grows:(g + 1) * grows] for g in range(NSA_GROUPS)]
        gs = gate_ref[0]
        o_ref[0] = (_expand_gates(gs, exp_ref, 0) * _place_heads(split(ocmp_ref[...]), tp)
                    + _expand_gates(gs, exp_ref, 1) * _place_heads(split(o_sel), tp)
                    + _expand_gates(gs, exp_ref, 2) * _place_heads(split(o_win), tp))


def nsa_decode(page_table, q8, gates, kvs_new, kvw_new, cache_cmp_t, cache_sel_t, cache_win_t, wts, l, *, t_s, pg=16):
    b, tp, _ = q8.shape
    n_pages = page_table.shape[1]
    page = cache_cmp_t.shape[3]
    past = n_pages * page
    n_steps = n_pages // pg
    nc = past // NSA_CMP_BLOCK
    assert (past + t_s) // NSA_CMP_BLOCK == nc
    nblk = wts['pool_s'].shape[1]
    bps = pg * page // NSA_SEL_BLOCK
    assert nblk % bps == 0 and bps <= LANES and (past // NSA_SEL_BLOCK) // bps < nblk // bps
    rows = NSA_HEADS * tp
    pt = page_table.reshape(-1)
    assert pg * page // NSA_CMP_BLOCK <= LANES
    e0 = jnp.asarray((np.arange(pg * page)[:, None] // NSA_CMP_BLOCK) == np.arange(LANES)[None, :], BF16)

    def cmp_spec(i):
        return pl.BlockSpec((1, 1, NSA_KV_COLS, page),
                            lambda s, j, pt_ref: (l, pt_ref[s * n_pages + jnp.minimum(j, n_steps - 1) * pg + i], 0, 0))

    def sel_spec(i):
        return pl.BlockSpec((1, 1, NSA_KV_COLS, page),
                            lambda s, j, pt_ref: (l, pt_ref[s * n_pages + jnp.maximum(j - n_steps, 0) * pg + i], 0, 0))

    seq = lambda a: pl.BlockSpec((1,) + a.shape[1:], lambda s, j, pt_ref: (s, 0, 0))
    res = lambda a: pl.BlockSpec(a.shape, lambda s, j, pt_ref: (0,) * a.ndim)
    consts = [wts['cmp_w_t'], wts['cmp_proj_t'], wts['pool_s'], wts['gate_exp'], e0]
    grid_spec = pltpu.PrefetchScalarGridSpec(
        num_scalar_prefetch=1,
        grid=(b, 2 * n_steps),
        in_specs=[seq(q8), seq(gates), seq(kvs_new), seq(kvw_new),
                  pl.BlockSpec((1, 1) + cache_win_t.shape[2:], lambda s, j, pt_ref: (l, s, 0, 0))]
                 + [res(a) for a in consts]
                 + [cmp_spec(i) for i in range(pg)] + [sel_spec(i) for i in range(pg)],
        out_specs=pl.BlockSpec((1, tp, NSA_QW), lambda s, j, pt_ref: (s, 0, 0)),
        scratch_shapes=[pltpu.VMEM((n_steps, NSA_KV_COLS, LANES), F32),
                        pltpu.VMEM((nblk // bps, rows, LANES), F32),
                        pltpu.VMEM((rows, LANES), F32),
                        pltpu.VMEM((rows, 1), F32),
                        pltpu.VMEM((rows, 1), F32),
                        pltpu.VMEM((rows, LANES), F32)])
    return pl.pallas_call(
        functools.partial(_nsa_decode_kernel, pg=pg, n_steps=n_steps, t_s=t_s, past=past),
        grid_spec=grid_spec,
        out_shape=jax.ShapeDtypeStruct((b, tp, NSA_QW), F32),
        compiler_params=_cparams(("parallel", "arbitrary")),
        name="nsa_decode",
    )(pt, q8, gates, kvs_new, kvw_new, cache_win_t, *consts, *([cache_cmp_t] * pg), *([cache_sel_t] * pg))


def _merge_kernel(x_ref, oa_ref, ob_ref, oc_ref, br_ref, pa_ref, pb_ref, pc_ref, wo_ref, y_ref):
    d = D_MODEL
    br = br_ref[...]
    u = (jax.nn.sigmoid(br[:, 0:d]) * jnp.dot(oa_ref[...].astype(BF16), pa_ref[...], preferred_element_type=F32)
         + jax.nn.sigmoid(br[:, d:2 * d]) * jnp.dot(ob_ref[...].astype(BF16), pb_ref[...], preferred_element_type=F32)
         + jax.nn.sigmoid(br[:, 2 * d:3 * d]) * jnp.dot(oc_ref[...].astype(BF16), pc_ref[...], preferred_element_type=F32))
    y_ref[...] = x_ref[...] + jnp.dot(u.astype(BF16), wo_ref[...], preferred_element_type=F32)


def merge(x, o_a, o_b, o_c, br, p_a, p_b, p_c, w_out, *, tm=256):
    m, d = x.shape
    tm = min(tm, m)
    bw = o_a.shape[1]
    row = lambda w: pl.BlockSpec((tm, w), lambda i: (i, 0))
    res = lambda a: pl.BlockSpec(a.shape, lambda i: (0, 0))
    return pl.pallas_call(
        _merge_kernel,
        grid=(m // tm,),
        in_specs=[row(d), row(bw), row(bw), row(bw), row(3 * d), res(p_a), res(p_b), res(p_c), res(w_out)],
        out_specs=row(d),
        out_shape=jax.ShapeDtypeStruct((m, d), F32),
        compiler_params=_cparams(("parallel",)),
        name="merge",
    )(x, o_a, o_b, o_c, br, p_a, p_b, p_c, w_out)


def _router_kernel(x_ref, g_ref, wr_ref, h_ref, route_ref):
    x = x_ref[...]
    ms = jnp.mean(x * x, axis=-1, keepdims=True)
    hb = (x * lax.rsqrt(ms + NORM_EPS) * g_ref[...]).astype(BF16)
    h_ref[...] = hb
    logits = jnp.dot(hb, wr_ref[...], preferred_element_type=F32)
    lane = lax.broadcasted_iota(jnp.int32, (1, ROUTER_PAD), 1)
    lane_f = lane.astype(F32)
    big = float(ROUTER_PAD)

    def top1(vals):
        mx = jnp.max(vals, axis=-1, keepdims=True)
        return mx, jnp.min(jnp.where(vals == mx, lane_f, big), axis=-1, keepdims=True)

    gl = jnp.where(lane < MOE_GROUPS, logits, NEG)
    gmax, gidx = top1(gl)
    g_w = 1.0 / jnp.sum(jnp.where(lane < MOE_GROUPS, jnp.exp(gl - gmax), 0.0), axis=-1, keepdims=True)
    lo = MOE_GROUPS + MOE_EPG * gidx
    el = jnp.where((lane_f >= lo) & (lane_f < lo + MOE_EPG), logits, NEG)
    e1, i1 = top1(el)
    e2, i2 = top1(jnp.where(lane_f == i1, NEG, el))
    z = jnp.exp(e2 - e1)
    w1 = g_w / (1.0 + z)
    w2 = g_w * z / (1.0 + z)
    route_ref[...] = jnp.where(lane == 0, i1 - MOE_GROUPS,
                               jnp.where(lane == 1, i2 - MOE_GROUPS,
                                         jnp.where(lane == 2, w1, jnp.where(lane == 3, w2, 0.0))))


def norm_router(x, g, w_router, *, tm=512):
    m, d = x.shape
    tm = min(tm, m)
    return pl.pallas_call(
        _router_kernel,
        grid=(m // tm,),
        in_specs=[pl.BlockSpec((tm, d), lambda i: (i, 0)),
                  pl.BlockSpec((1, d), lambda i: (0, 0)),
                  pl.BlockSpec((d, ROUTER_PAD), lambda i: (0, 0))],
        out_specs=[pl.BlockSpec((tm, d), lambda i: (i, 0)),
                   pl.BlockSpec((tm, ROUTER_PAD), lambda i: (i, 0))],
        out_shape=[jax.ShapeDtypeStruct((m, d), BF16), jax.ShapeDtypeStruct((m, ROUTER_PAD), F32)],
        compiler_params=_cparams(("parallel",)),
        name="norm_router",
    )(x, g.reshape(1, d), w_router)


def _expert_kernel(be_ref, x_ref, rw_ref, w1_ref, w3_ref, w2_ref, o_ref):
    x = x_ref[...]
    a = jnp.dot(x, w1_ref[0], preferred_element_type=F32)
    b = jnp.dot(x, w3_ref[0], preferred_element_type=F32)
    hmid = _silu(a) * b
    y = jnp.dot(hmid.astype(BF16), w2_ref[0], preferred_element_type=F32)
    o_ref[...] = y * rw_ref[...]


def grouped_experts_blocks(xb, row_w, blk_e, w1, w3, w2):
    rows, d = xb.shape
    n_blk = rows // MOE_BLOCK
    grid_spec = pltpu.PrefetchScalarGridSpec(
        num_scalar_prefetch=1,
        grid=(n_blk,),
        in_specs=[pl.BlockSpec((MOE_BLOCK, d), lambda b, be: (b, 0)),
                  pl.BlockSpec((MOE_BLOCK, 1), lambda b, be: (b, 0)),
                  pl.BlockSpec((1, d, MOE_DFF), lambda b, be: (be[b], 0, 0)),
                  pl.BlockSpec((1, d, MOE_DFF), lambda b, be: (be[b], 0, 0)),
                  pl.BlockSpec((1, MOE_DFF, d), lambda b, be: (be[b], 0, 0))],
        out_specs=pl.BlockSpec((MOE_BLOCK, d), lambda b, be: (b, 0)))
    return pl.pallas_call(
        _expert_kernel,
        grid_spec=grid_spec,
        out_shape=jax.ShapeDtypeStruct((rows, d), F32),
        compiler_params=_cparams(("arbitrary",)),
        name="moe_experts",
    )(blk_e, xb, row_w.reshape(rows, 1), w1, w3, w2)


def hier_moe_residual(x, norm_g, w_router, w1, w3, w2):
    m, d = x.shape
    h, route = norm_router(x, norm_g, w_router)
    eid = route[:, :MOE_TOPK].astype(jnp.int32)
    ew = route[:, MOE_TOPK:2 * MOE_TOPK]

    n_assign = m * MOE_TOPK
    fe = eid.reshape(-1).astype(jnp.int32)
    onehot = (fe[:, None] == jnp.arange(MOE_EXPERTS, dtype=jnp.int32)[None, :]).astype(jnp.int32)
    csum = jnp.cumsum(onehot, axis=0)
    rank = jnp.sum((csum - onehot) * onehot, axis=1)
    counts = csum[-1]
    padded = (counts + MOE_BLOCK - 1) // MOE_BLOCK * MOE_BLOCK
    ends = jnp.cumsum(padded)
    dest = (ends - padded)[fe] + rank
    n_blk = -(-(n_assign + MOE_EXPERTS * (MOE_BLOCK - 1)) // MOE_BLOCK)
    rows = n_blk * MOE_BLOCK
    tok = jnp.arange(n_assign, dtype=jnp.int32) // MOE_TOPK
    row_tok = jnp.full((rows,), m, jnp.int32).at[dest].set(tok)
    row_w = jnp.zeros((rows,), F32).at[dest].set(ew.reshape(-1))
    blk_e = jnp.minimum(jnp.searchsorted(ends, jnp.arange(n_blk, dtype=jnp.int32) * MOE_BLOCK, side='right'),
                        MOE_EXPERTS - 1).astype(jnp.int32)
    xb = jnp.concatenate([h, jnp.zeros((1, d), h.dtype)], axis=0)[row_tok]
    yb = grouped_experts_blocks(xb, row_w, blk_e, w1, w3, w2)
    slot = dest.reshape(m, MOE_TOPK)
    y = yb[slot[:, 0]]
    for k in range(1, MOE_TOPK):
        y = y + yb[slot[:, k]]
    return x + y


def _rope_tables(pos):
    posf = pos.astype(F32)[:, None]
    inv_m = ROPE_THETA ** (-(jnp.arange(MLA_HALF, dtype=F32) * 2.0 / MLA_ROPE))
    ang_m = posf * inv_m[None, :]
    cos_m = jnp.tile(jnp.cos(ang_m), (1, LANES // MLA_HALF))
    sin_m = jnp.tile(jnp.sin(ang_m), (1, LANES // MLA_HALF))
    half = NSA_ROT // 2
    inv_n = ROPE_THETA ** (-(jnp.arange(half, dtype=F32) * 2.0 / NSA_ROT))
    ang_n = posf * inv_n[None, :]
    cn, sn = jnp.cos(ang_n), jnp.sin(ang_n)
    t = pos.shape[0]
    one = jnp.ones((t, NSA_DH - NSA_ROT), F32)
    zero = jnp.zeros((t, NSA_DH - NSA_ROT), F32)
    z8 = jnp.zeros((t, half), F32)
    c = jnp.concatenate([cn, cn, one], axis=1)
    s1 = jnp.concatenate([-sn, z8, zero], axis=1)
    s2 = jnp.concatenate([z8, sn, zero], axis=1)
    rep = lambda a: jnp.tile(a, (1, LANES // NSA_DH))
    return (cos_m, sin_m), (rep(c), rep(s1), rep(s2))


def _layer_weights(l, w_in, raw):
    w = {}
    wl = w_in[l]
    d = wl.shape[0]
    col = lambda i: wl[:, IN_OFF[i]:IN_OFF[i + 1]]
    zeros = lambda n: jnp.zeros((d, n), F32)
    w['w_gdn'] = jnp.concatenate([col(0), col(1), col(2), col(3), zeros(LANES - 2 * GDN_HEADS)], axis=1).astype(BF16)
    kr = col(6)
    kr_rep = jnp.concatenate([jnp.tile(kr[:, :MLA_HALF], (1, MLA_HEADS)), jnp.tile(kr[:, MLA_HALF:], (1, MLA_HEADS))],
                             axis=1)
    w['w_mla'] = jnp.concatenate([col(4), col(5), kr_rep, zeros(LANES)], axis=1).astype(BF16)
    w['w_nsa'] = jnp.concatenate([col(7), col(8), col(9), col(10), col(11), zeros(LANES - 3 * NSA_HEADS)],
                                 axis=1).astype(BF16)
    w['w_br'] = col(12).astype(BF16)

    uq = raw['mla_w_uq'][l].reshape(MLA_Q_LORA, MLA_HEADS, MLA_NOPE + MLA_ROPE)
    w['w_uq'] = jnp.concatenate([uq[:, :, :MLA_NOPE].reshape(MLA_Q_LORA, -1),
                                 uq[:, :, MLA_NOPE:MLA_NOPE + MLA_HALF].reshape(MLA_Q_LORA, -1),
                                 uq[:, :, MLA_NOPE + MLA_HALF:].reshape(MLA_Q_LORA, -1)], axis=1).astype(BF16)
    uk = jnp.transpose(raw['mla_w_uk'][l], (0, 2, 1))
    zk = jnp.zeros_like(uk)
    even = (jnp.arange(MLA_HEADS) % 2 == 0)[:, None, None]
    w['w_uk'] = jnp.concatenate([jnp.where(even, uk, zk), jnp.where(even, zk, uk)], axis=1).astype(BF16)
    w['w_uv'] = raw['mla_w_uv'][l].astype(BF16)
    w['w_uv_all'] = jnp.transpose(raw['mla_w_uv'][l], (1, 0, 2)).reshape(MLA_KV_LORA, MLA_HEADS * MLA_V).astype(BF16)
    w['cq_g'] = raw['mla_cq_g'][l].reshape(1, -1)
    w['ckv_g'] = raw['mla_ckv_g'][l].reshape(1, -1)
    w['qn_g'] = jnp.tile(raw['mla_qn_g'][l], MLA_HEADS).reshape(1, -1)
    split_tile = lambda g: jnp.stack([jnp.tile(g[:MLA_HALF], MLA_HEADS), jnp.tile(g[MLA_HALF:], MLA_HEADS)])
    w['qr_g'] = split_tile(raw['mla_qr_g'][l])
    w['kr_g'] = split_tile(raw['mla_kr_g'][l])
    w['bd64_512'] = _block_diag_ones(NSA_QW, NSA_DH)
    w['bd64_128'] = _block_diag_ones(LANES, NSA_DH)
    w['bd16'] = _block_diag_ones(LANES, MLA_HALF)

    w['nsa_qn_g'] = jnp.tile(raw['nsa_qn_g'][l], NSA_HEADS).reshape(1, -1)
    w['nsa_kn_g'] = jnp.tile(raw['nsa_kn_g'][l], (1, NSA_GROUPS))
    wpos = raw['nsa_cmp_wpos'][l]
    wrow = jnp.concatenate([jnp.tile(wpos[0], (1, NSA_GROUPS)), jnp.tile(wpos[1], (1, NSA_GROUPS))], axis=1)
    w['cmp_w'] = jnp.tile(wrow, (512 // NSA_CMP_BLOCK, 1))
    proj = raw['nsa_cmp_proj'][l]
    pbd = jnp.zeros((NSA_KV_COLS, NSA_KV_COLS), F32)
    for kv in range(2):
        for g in range(NSA_GROUPS):
            o = (kv * NSA_GROUPS + g) * NSA_DH
            pbd = pbd.at[o:o + NSA_DH, o:o + NSA_DH].set(proj[kv])
    w['cmp_proj'] = pbd.astype(BF16)
    w['cmp_w_t'] = jnp.tile(wrow.T, (1, LANES // NSA_CMP_BLOCK))
    w['cmp_proj_t'] = pbd.T.astype(BF16)
    ge = np.zeros((3, LANES, NSA_QW), np.float32)
    for j in range(3):
        for h in range(NSA_HEADS):
            ge[j, 3 * h + j, h * NSA_DH:(h + 1) * NSA_DH] = 1.0
    w['gate_exp'] = jnp.asarray(ge, BF16)

    w['p_gdn'] = raw['p_gdn'][l].astype(BF16)
    w['p_mla'] = raw['p_mla'][l].astype(BF16)
    w['p_nsa'] = raw['p_nsa'][l].astype(BF16)
    w['w_out'] = raw['w_out'][l].astype(BF16)
    w['w_router'] = jnp.pad(jnp.concatenate([raw['moe_w_group'][l], raw['moe_w_expert'][l]], axis=1),
                            ((0, 0), (0, ROUTER_PAD - MOE_GROUPS - MOE_EXPERTS))).astype(BF16)
    w['w1'] = raw['moe_w1'][l].astype(BF16)
    w['w3'] = raw['moe_w3'][l].astype(BF16)
    w['w2'] = raw['moe_w2'][l].astype(BF16)
    return w


def _pool_matrix(n_in, n_out):
    pm = np.zeros((n_in, n_out), np.float32)
    ratio = NSA_SEL_BLOCK // NSA_CMP_BLOCK
    for c in range(n_in):
        if c // ratio < n_out:
            pm[c, c // ratio] = 1.0
    return jnp.asarray(pm, BF16)


def kernel(x_prompt, x_sample, cache_mla_ckv, cache_mla_krope, cache_nsa_cmp, cache_nsa_sel, cache_nsa_win, state_gdn_ssm, state_gdn_conv, page_table, norm1_g, w_in, gdn_conv_w, gdn_a_log, gdn_dt_bias, gdn_out_g, mla_cq_g, mla_ckv_g, mla_w_uq, mla_qn_g, mla_qr_g, mla_kr_g, mla_w_uk, mla_w_uv, nsa_qn_g, nsa_kn_g, nsa_cmp_wpos, nsa_cmp_proj, p_gdn, p_mla, p_nsa, w_out, norm2_g, moe_w_group, moe_w_expert, moe_w1, moe_w3, moe_w2):
    depth = w_in.shape[0]
    b_p, t_p, d = x_prompt.shape
    b_s, t_s, _ = x_sample.shape
    n_pool, page = cache_mla_ckv.shape[1:3]
    past = page_table.shape[1] * page
    win_s = cache_nsa_win.shape[2]
    assert t_p >= GDN_CONV - 1 and t_s >= GDN_CONV - 1 and t_p % GDN_CHUNK == 0
    tp_s = 8
    assert t_s <= tp_s
    raw = dict(mla_w_uq=mla_w_uq, mla_w_uk=mla_w_uk, mla_w_uv=mla_w_uv, mla_cq_g=mla_cq_g, mla_ckv_g=mla_ckv_g,
               mla_qn_g=mla_qn_g, mla_qr_g=mla_qr_g, mla_kr_g=mla_kr_g, nsa_qn_g=nsa_qn_g, nsa_kn_g=nsa_kn_g,
               nsa_cmp_wpos=nsa_cmp_wpos, nsa_cmp_proj=nsa_cmp_proj, p_gdn=p_gdn, p_mla=p_mla, p_nsa=p_nsa,
               w_out=w_out, moe_w_group=moe_w_group, moe_w_expert=moe_w_expert, moe_w1=moe_w1, moe_w3=moe_w3,
               moe_w2=moe_w2)
    pos_p = jnp.arange(t_p, dtype=jnp.int32)
    pos_s = jnp.tile(past + jnp.arange(t_s, dtype=jnp.int32), b_s)
    mla_tab_p, nsa_tab_p = _rope_tables(pos_p)
    mla_tab_s, nsa_tab_s = _rope_tables(pos_s)
    pool_p = _pool_matrix(LANES, LANES)
    nc_s = past // NSA_CMP_BLOCK
    pool_s = _pool_matrix(nc_s, nc_s)
    feat_major = lambda a: jnp.transpose(a, (0, 1, 3, 4, 5, 2)).reshape(a.shape[0], a.shape[1], NSA_KV_COLS, a.shape[2])
    cache_cmp_t = feat_major(cache_nsa_cmp)
    cache_sel_t = feat_major(cache_nsa_sel)
    cache_win_t = feat_major(cache_nsa_win)
    cache_kr_t = jnp.transpose(cache_mla_krope, (0, 1, 3, 2))

    def project(x2, l, w):
        g = norm1_g[l]
        return (norm_matmul(x2, g, w['w_gdn']), norm_matmul(x2, g, w['w_mla']),
                norm_matmul(x2, g, w['w_nsa']), norm_matmul(x2, g, w['w_br']))

    def finish(x2, o_a, o_b, o_c, br, l, w):
        y = merge(x2, o_a, o_b, o_c, br, w['p_gdn'], w['p_mla'], w['p_nsa'], w['w_out'])
        return hier_moe_residual(y, norm2_g[l], w['w_router'], w['w1'], w['w3'], w['w2'])

    def prompt_layer(x, l, w):
        m = b_p * t_p
        x2 = x.reshape(m, d)
        pg_, pm_, pn_, br = project(x2, l, w)
        pg3 = pg_.reshape(b_p, t_p, GDN_COLS)
        o_a, ssm = gdn_mixer(pg3, jnp.zeros((b_p, 8, GDN_CONV_DIM), F32),
                             jnp.zeros((b_p, GDN_HEADS, GDN_DK, GDN_DV), F32),
                             gdn_conv_w[l], gdn_a_log[l], gdn_dt_bias[l], gdn_out_g[l],
                             chunk=GDN_CHUNK, t_valid=GDN_CHUNK, nb=1, cb=4)
        conv = pg3[:, t_p - (GDN_CONV - 1):, :GDN_CONV_DIM]
        q_cat, k_cat, ckv, krope = mla_prep(pm_, *mla_tab_p, w, tm=256)
        o_b = mla_flash(q_cat, k_cat, w['w_uv'], b_p)
        q8, kv_c, kv_s, kv_w, gates, kvc = nsa_prep(pn_, nsa_tab_p, w, tm=256, compress=True)
        o_c = nsa_prompt(q8, gates, kvc.reshape(b_p, t_p // NSA_CMP_BLOCK, NSA_KV_COLS),
                         kv_s.reshape(b_p, t_p, NSA_KV_COLS), kv_w.reshape(b_p, t_p, NSA_KV_COLS),
                         pool_p, w['gate_exp'], b_p)
        y = finish(x2, o_a.reshape(m, GDN_VW), o_b, o_c, br, l, w).reshape(b_p, t_p, d)
        kv5 = lambda a: a.reshape(b_p, t_p, 2, NSA_GROUPS, NSA_DH)
        kvw5 = kv5(kv_w)
        return y, (ssm, conv, ckv.reshape(b_p, t_p, MLA_KV_LORA), krope.reshape(b_p, t_p, MLA_ROPE),
                   kv5(kv_c), kv5(kv_s), kvw5[:, t_p - min(NSA_WINDOW, t_p):])

    def sample_layer(x, l, w):
        m = b_s * t_s
        x2 = x.reshape(m, d)
        pg_, pm_, pn_, br = project(x2, l, w)
        pad_t = lambda a: jnp.pad(a.reshape(b_s, t_s, -1), ((0, 0), (0, tp_s - t_s), (0, 0)))
        pg3 = pg_.reshape(b_s, t_s, GDN_COLS)
        hist8 = jnp.pad(state_gdn_conv[l], ((0, 0), (8 - (GDN_CONV - 1), 0), (0, 0)))
        o_a, ssm = gdn_mixer(pad_t(pg_), hist8, state_gdn_ssm[l], gdn_conv_w[l], gdn_a_log[l], gdn_dt_bias[l],
                             gdn_out_g[l], chunk=tp_s, t_valid=t_s, nb=4, cb=1)
        o_a = o_a[:, :t_s].reshape(m, GDN_VW)
        conv = pg3[:, t_s - (GDN_CONV - 1):, :GDN_CONV_DIM]

        q_cat, k_cat, ckv, krope = mla_prep(pm_, *mla_tab_s, w, tm=m)
        qc = q_cat.reshape(m, MLA_HEADS, MLA_QK)
        q_lat = qc[:, :, :MLA_KV_LORA]
        hsel = jnp.arange(MLA_HEADS)
        r1 = qc[:, :, MLA_KV_LORA:MLA_KV_LORA + LANES].reshape(m, MLA_HEADS, MLA_HEADS, MLA_HALF)[:, hsel, hsel]
        r2 = qc[:, :, MLA_KV_LORA + LANES:].reshape(m, MLA_HEADS, MLA_HEADS, MLA_HALF)[:, hsel, hsel]
        q_rope = jnp.concatenate([r1, r2], axis=-1)
        rows_of = lambda a: pad_t(a.reshape(m, -1)).reshape(b_s, tp_s * MLA_HEADS, -1)
        o_b = mla_decode(page_table, rows_of(q_lat), rows_of(q_rope), pad_t(ckv).astype(BF16),
                         pad_t(krope).astype(BF16), w['w_uv_all'], cache_mla_ckv, cache_kr_t, l, t_s=t_s)
        o_b = o_b[:, :t_s].reshape(m, MLA_HEADS * MLA_V)

        q8, kv_c, kv_s, kv_w, gates = nsa_prep(pn_, nsa_tab_s, w, tm=m, compress=False)
        ws = dict(w, pool_s=pool_s)
        o_c = nsa_decode(page_table, pad_t(q8), pad_t(gates), pad_t(kv_s), pad_t(kv_w), cache_cmp_t, cache_sel_t,
                         cache_win_t, ws, l, t_s=t_s)
        o_c = o_c[:, :t_s].reshape(m, NSA_QW)
        y = finish(x2, o_a, o_b, o_c, br, l, w).reshape(b_s, t_s, d)
        kv5 = lambda a: a.reshape(b_s, t_s, 2, NSA_GROUPS, NSA_DH)
        win_new = jnp.concatenate([cache_nsa_win[l], kv5(kv_w)], axis=1)[:, t_s:]
        return y, (ssm, conv, ckv.reshape(b_s, t_s, MLA_KV_LORA), krope.reshape(b_s, t_s, MLA_ROPE),
                   kv5(kv_c), kv5(kv_s), win_new)

    xp, xs = x_prompt, x_sample
    p_states, s_states = [], []
    for l in range(depth):
        w = _layer_weights(l, w_in, raw)
        xp, st = prompt_layer(xp, l, w)
        p_states.append(st)
        xs, st = sample_layer(xs, l, w)
        s_states.append(st)
    p_out = [jnp.stack([st[i] for st in p_states], axis=0) for i in range(7)]
    s_out = [jnp.stack([st[i] for st in s_states], axis=0) for i in range(7)]
    return (xp, xs, *p_out, *s_out)
```

```python
import functools
import math

import numpy as np
import jax
import jax.numpy as jnp
from jax import lax
from jax.experimental import pallas as pl
from jax.experimental.pallas import tpu as pltpu

F32 = jnp.float32
BF16 = jnp.bfloat16
HIGHEST = lax.Precision.HIGHEST

D_MODEL = 1024
ROPE_THETA = 500000.0
NORM_EPS = 1e-6
NEG = -1e30
LANES = 128

GDN_HEADS = 4
GDN_DK = 128
GDN_DV = 128
GDN_CONV = 4
GDN_CHUNK = 64
GDN_QK = GDN_HEADS * GDN_DK
GDN_VW = GDN_HEADS * GDN_DV
GDN_CONV_DIM = 2 * GDN_QK + GDN_VW

MLA_HEADS = 8
MLA_Q_LORA = 384
MLA_KV_LORA = 256
MLA_NOPE = 64
MLA_ROPE = 32
MLA_HALF = MLA_ROPE // 2
MLA_V = 64
MLA_SCALE = (MLA_NOPE + MLA_ROPE) ** -0.5
MLA_QK = MLA_KV_LORA + 2 * LANES
FLASH_HEADS_PER_BLOCK = 2
DECODE_PAGE_GROUPS = 4

NSA_HEADS = 8
NSA_GROUPS = 2
NSA_HPG = NSA_HEADS // NSA_GROUPS
NSA_DH = 64
NSA_ROT = NSA_DH // 4
NSA_CMP_BLOCK = 32
NSA_SEL_BLOCK = 64
NSA_TOPK = 16
NSA_WINDOW = 512
NSA_FORCE = 1000.0
NSA_SCALE = NSA_DH ** -0.5
NSA_KV_COLS = 2 * NSA_GROUPS * NSA_DH
NSA_QW = NSA_HEADS * NSA_DH

MOE_GROUPS = 4
MOE_EPG = 8
MOE_EXPERTS = MOE_GROUPS * MOE_EPG
MOE_TOPK = 2
MOE_DFF = 512
MOE_BLOCK = 128
ROUTER_PAD = LANES

IN_SPLITS = (GDN_CONV_DIM, GDN_VW, GDN_HEADS, GDN_HEADS,
             MLA_Q_LORA, MLA_KV_LORA, MLA_ROPE,
             NSA_HEADS * NSA_DH, NSA_KV_COLS, NSA_KV_COLS, NSA_KV_COLS, 3 * NSA_HEADS,
             3 * D_MODEL)
IN_OFF = [0]
for _w in IN_SPLITS:
    IN_OFF.append(IN_OFF[-1] + _w)

GDN_COLS = GDN_CONV_DIM + GDN_VW + LANES
MLA_COLS = MLA_Q_LORA + MLA_KV_LORA + 2 * LANES + LANES
NSA_COLS = NSA_QW + 3 * NSA_KV_COLS + LANES

VMEM_LIMIT = 52 * 1024 * 1024

_NT = (((1,), (1,)), ((), ()))
_TN = (((0,), (0,)), ((), ()))


def _cparams(sem):
    return pltpu.CompilerParams(dimension_semantics=sem, vmem_limit_bytes=VMEM_LIMIT)


def _block_diag_ones(n, grp):
    i = np.arange(n)
    return jnp.asarray((i[:, None] // grp) == (i[None, :] // grp), BF16)


def _group_sum(sq, bd):
    hi = sq.astype(BF16)
    lo = (sq - hi.astype(F32)).astype(BF16)
    return jnp.dot(hi, bd, preferred_element_type=F32) + jnp.dot(lo, bd, preferred_element_type=F32)


def _softmax_rows(s, mask):
    s = jnp.where(mask, s, NEG)
    m = jnp.max(s, axis=-1, keepdims=True)
    p = jnp.where(mask, jnp.exp(s - m), 0.0)
    return p / jnp.maximum(jnp.sum(p, axis=-1, keepdims=True), 1e-30)


def _silu(x):
    return x * jax.nn.sigmoid(x)


def _norm_matmul_kernel(x_ref, g_ref, w_ref, o_ref, h_ref):
    @pl.when(pl.program_id(1) == 0)
    def _():
        x = x_ref[...]
        ms = jnp.mean(x * x, axis=-1, keepdims=True)
        h_ref[...] = (x * lax.rsqrt(ms + NORM_EPS) * g_ref[...]).astype(BF16)

    o_ref[...] = jnp.dot(h_ref[...], w_ref[...], preferred_element_type=F32)


def norm_matmul(x, g, w, *, tm=512, tn=None):
    m, k = x.shape
    n = w.shape[1]
    tm = min(tm, m)
    tn = n if tn is None else tn
    assert m % tm == 0 and n % tn == 0
    return pl.pallas_call(
        _norm_matmul_kernel,
        grid=(m // tm, n // tn),
        in_specs=[pl.BlockSpec((tm, k), lambda i, j: (i, 0)),
                  pl.BlockSpec((1, k), lambda i, j: (0, 0)),
                  pl.BlockSpec((k, tn), lambda i, j: (0, j))],
        out_specs=pl.BlockSpec((tm, tn), lambda i, j: (i, j)),
        out_shape=jax.ShapeDtypeStruct((m, n), F32),
        scratch_shapes=[pltpu.VMEM((tm, k), BF16)],
        compiler_params=_cparams(("parallel", "arbitrary")),
        name="norm_matmul",
    )(x, g.reshape(1, k), w)


def _gdn_kernel(p_ref, hist_ref, s0_ref, cw_ref, alog_ref, dtb_ref, og_ref, o_ref, sfin_ref,
                prev_ref, s_ref, *, chunk, t_valid, nb, cb):
    c = pl.program_id(1)
    n_steps = pl.num_programs(1)
    qk = GDN_QK
    span = cb * chunk
    dot32 = functools.partial(jnp.dot, preferred_element_type=F32, precision=HIGHEST)
    dotb = lambda a, b: jnp.dot(a.astype(BF16), b.astype(BF16), preferred_element_type=F32)

    @pl.when(c == 0)
    def _():
        prev_ref[...] = hist_ref[...]
        s_ref[...] = s0_ref[...]

    row = lax.broadcasted_iota(jnp.int32, (chunk, chunk), 0)
    col = lax.broadcasted_iota(jnp.int32, (chunk, chunk), 1)
    incl = row >= col
    strict = row > col
    eye_f = (row == col).astype(F32)
    incl_f = incl.astype(F32)
    live = (lax.broadcasted_iota(jnp.int32, (chunk, 1), 0) < t_valid) if t_valid < chunk else None
    n_fac = int(math.log2(chunk)) - 1

    chains = []
    raws, qkvs, betas, gcs, gcts = [], [], {}, {}, {}
    for s in range(nb):
        raw = p_ref[s]
        xcat = jnp.concatenate([prev_ref[s], raw[:, :GDN_CONV_DIM]], axis=0)
        prev_ref[s] = xcat[span:span + 8]
        acc = xcat[8:8 + span] * cw_ref[GDN_CONV - 1:GDN_CONV, :]
        for i in range(GDN_CONV - 1):
            sh = pltpu.roll(xcat, GDN_CONV - 1 - i, 0)
            acc = acc + sh[8:8 + span] * cw_ref[i:i + 1, :]
        raws.append(raw)
        qkvs.append(_silu(acc))
        ba = raw[:, GDN_CONV_DIM + GDN_VW:]
        xa = ba + dtb_ref[...]
        softplus = jnp.maximum(xa, 0.0) + jnp.log(1.0 + jnp.exp(-jnp.abs(xa)))
        beta_all = jax.nn.sigmoid(ba)
        g_all = -jnp.exp(alog_ref[...]) * softplus
        for ci in range(cb):
            bt = beta_all[ci * chunk:(ci + 1) * chunk]
            gt = g_all[ci * chunk:(ci + 1) * chunk]
            if live is not None:
                bt = jnp.where(live, bt, 0.0)
                gt = jnp.where(live, gt, 0.0)
            betas[(s, ci)] = bt
            gcs[(s, ci)] = dot32(incl_f, gt)
            gcts[(s, ci)] = lax.dot_general(gt, incl_f, (((0,), (1,)), ((), ())), preferred_element_type=F32,
                                            precision=HIGHEST)
    for ci in range(cb):
        for s in range(nb):
            for h in range(GDN_HEADS):
                chains.append((s, ci, h))
    nch = len(chains)

    def tile(s, ci, off):
        return qkvs[s][ci * chunk:(ci + 1) * chunk, off:off + GDN_DK]

    qs = [tile(s, ci, h * GDN_DK) for s, ci, h in chains]
    ks = [tile(s, ci, qk + h * GDN_DK) for s, ci, h in chains]
    vs = [tile(s, ci, 2 * qk + h * GDN_DV) for s, ci, h in chains]
    qs = [q * lax.rsqrt(jnp.sum(q * q, axis=-1, keepdims=True) + NORM_EPS) * (GDN_DK ** -0.5) for q in qs]
    ks = [k * lax.rsqrt(jnp.sum(k * k, axis=-1, keepdims=True) + NORM_EPS) for k in ks]
    beta = [betas[(s, ci)][:, h:h + 1] for s, ci, h in chains]
    gcol = [gcs[(s, ci)][:, GDN_HEADS + h:GDN_HEADS + h + 1] for s, ci, h in chains]
    grow = [gcts[(s, ci)][GDN_HEADS + h:GDN_HEADS + h + 1, :] for s, ci, h in chains]
    decay = [jnp.where(incl, jnp.exp(jnp.where(incl, gcol[i] - grow[i], 0.0)), 0.0) for i in range(nch)]
    kb = [ks[i] * beta[i] for i in range(nch)]
    kbf = [k.astype(BF16) for k in ks]
    a_kk = [jnp.where(strict, lax.dot_general(kb[i].astype(BF16), kbf[i], _NT, preferred_element_type=F32) * decay[i],
                      0.0) for i in range(nch)]
    inv = [eye_f - a for a in a_kk]
    pw = a_kk
    for _ in range(n_fac):
        pw = [dot32(p, p) for p in pw]
        inv = [inv[i] + dot32(inv[i], pw[i]) for i in range(nch)]
    egc = [jnp.exp(g) for g in gcol]
    sol = [dot32(inv[i], jnp.concatenate([vs[i] * beta[i], kb[i] * egc[i]], axis=1)) for i in range(nch)]
    a_qk = [lax.dot_general(qs[i].astype(BF16), kbf[i], _NT, preferred_element_type=F32) * decay[i]
            for i in range(nch)]
    qe = [(qs[i] * egc[i]).astype(BF16) for i in range(nch)]
    g_last = [g[chunk - 1:chunk, :] for g in gcol]
    kdec = [(ks[i] * jnp.exp(g_last[i] - gcol[i])).astype(BF16) for i in range(nch)]
    e_last = [jnp.exp(g) for g in g_last]

    per = nb * GDN_HEADS
    state = {(s, h): s_ref[s, h] for s in range(nb) for h in range(GDN_HEADS)}
    for ci in range(cb):
        idx = range(ci * per, (ci + 1) * per)
        sb = {i: state[(chains[i][0], chains[i][2])].astype(BF16) for i in idx}
        v_new = {i: sol[i][:, :GDN_DV] - jnp.dot(sol[i][:, GDN_DV:].astype(BF16), sb[i], preferred_element_type=F32)
                 for i in idx}
        o = {i: jnp.dot(qe[i], sb[i], preferred_element_type=F32) + dotb(a_qk[i], v_new[i]) for i in idx}
        for i in idx:
            s, _, h = chains[i]
            state[(s, h)] = state[(s, h)] * e_last[i] + lax.dot_general(kdec[i], v_new[i].astype(BF16), _TN,
                                                                        preferred_element_type=F32)
            on = o[i] * lax.rsqrt(jnp.mean(o[i] * o[i], axis=-1, keepdims=True) + NORM_EPS) * og_ref[...]
            z = raws[s][ci * chunk:(ci + 1) * chunk, GDN_CONV_DIM + h * GDN_DV:GDN_CONV_DIM + (h + 1) * GDN_DV]
            o_ref[s, ci * chunk:(ci + 1) * chunk, h * GDN_DV:(h + 1) * GDN_DV] = on * _silu(z)
    for (s, h), val in state.items():
        s_ref[s, h] = val

    @pl.when(c == n_steps - 1)
    def _():
        sfin_ref[...] = s_ref[...]


def gdn_mixer(p_gdn, hist8, s0, conv_w, a_log, dt_bias, out_g, *, chunk, t_valid, nb, cb):
    n, tp, _ = p_gdn.shape
    span = cb * chunk
    assert n % nb == 0 and tp % span == 0
    pad4 = jnp.zeros((GDN_HEADS,), F32)
    lane_row = lambda a: jnp.pad(jnp.concatenate([pad4, a]), (0, LANES - 2 * GDN_HEADS)).reshape(1, LANES)
    return pl.pallas_call(
        functools.partial(_gdn_kernel, chunk=chunk, t_valid=t_valid, nb=nb, cb=cb),
        grid=(n // nb, tp // span),
        in_specs=[pl.BlockSpec((nb, span, GDN_COLS), lambda b, c: (b, c, 0)),
                  pl.BlockSpec((nb, 8, GDN_CONV_DIM), lambda b, c: (b, 0, 0)),
                  pl.BlockSpec((nb, GDN_HEADS, GDN_DK, GDN_DV), lambda b, c: (b, 0, 0, 0)),
                  pl.BlockSpec((GDN_CONV, GDN_CONV_DIM), lambda b, c: (0, 0)),
                  pl.BlockSpec((1, LANES), lambda b, c: (0, 0)),
                  pl.BlockSpec((1, LANES), lambda b, c: (0, 0)),
                  pl.BlockSpec((1, GDN_DV), lambda b, c: (0, 0))],
        out_specs=[pl.BlockSpec((nb, span, GDN_VW), lambda b, c: (b, c, 0)),
                   pl.BlockSpec((nb, GDN_HEADS, GDN_DK, GDN_DV), lambda b, c: (b, 0, 0, 0))],
        out_shape=[jax.ShapeDtypeStruct((n, tp, GDN_VW), F32),
                   jax.ShapeDtypeStruct((n, GDN_HEADS, GDN_DK, GDN_DV), F32)],
        scratch_shapes=[pltpu.VMEM((nb, 8, GDN_CONV_DIM), F32), pltpu.VMEM((nb, GDN_HEADS, GDN_DK, GDN_DV), F32)],
        compiler_params=_cparams(("parallel", "arbitrary")),
        name="gdn_mixer",
    )(p_gdn, hist8, s0, conv_w, lane_row(a_log), lane_row(dt_bias), out_g.reshape(1, GDN_DV))


def _mla_prep_kernel(p_ref, cos_ref, sin_ref, cqg_ref, ckvg_ref, wuq_ref, qng_ref, qrg_ref, krg_ref, wuk_ref,
                     bd64_ref, bd16_ref, qcat_ref, kcat_ref, ckv_ref, krope_ref):
    p = p_ref[...]
    cos = cos_ref[...]
    sin = sin_ref[...]
    lane = lax.broadcasted_iota(jnp.int32, (1, LANES), 1)

    cq = p[:, :MLA_Q_LORA]
    cq = cq * lax.rsqrt(jnp.mean(cq * cq, axis=-1, keepdims=True) + NORM_EPS) * cqg_ref[...]
    q = jnp.dot(cq.astype(BF16), wuq_ref[...], preferred_element_type=F32)
    nw = MLA_HEADS * MLA_NOPE
    nope = q[:, :nw]
    nope = nope * lax.rsqrt(_group_sum(nope * nope, bd64_ref[...]) * (1.0 / MLA_NOPE) + NORM_EPS) * qng_ref[...]
    x1 = q[:, nw:nw + LANES]
    x2 = q[:, nw + LANES:]
    r = lax.rsqrt(_group_sum(x1 * x1 + x2 * x2, bd16_ref[...]) * (1.0 / MLA_ROPE) + NORM_EPS)
    x1 = x1 * r * qrg_ref[0:1, :]
    x2 = x2 * r * qrg_ref[1:2, :]
    r1 = x1 * cos - x2 * sin
    r2 = x2 * cos + x1 * sin
    for h in range(MLA_HEADS):
        tile = nope[:, (h // 2) * LANES:(h // 2 + 1) * LANES].astype(BF16)
        q_lat = jnp.dot(tile, wuk_ref[h], preferred_element_type=F32)
        mine = (lane // MLA_HALF) == h
        base = h * MLA_QK
        qcat_ref[:, base:base + MLA_KV_LORA] = q_lat.astype(BF16)
        qcat_ref[:, base + MLA_KV_LORA:base + MLA_KV_LORA + LANES] = jnp.where(mine, r1, 0.0).astype(BF16)
        qcat_ref[:, base + MLA_KV_LORA + LANES:base + MLA_QK] = jnp.where(mine, r2, 0.0).astype(BF16)

    ckv = p[:, MLA_Q_LORA:MLA_Q_LORA + MLA_KV_LORA]
    ckv = ckv * lax.rsqrt(jnp.mean(ckv * ckv, axis=-1, keepdims=True) + NORM_EPS) * ckvg_ref[...]
    ckv_ref[...] = ckv
    kr = p[:, MLA_Q_LORA + MLA_KV_LORA:MLA_Q_LORA + MLA_KV_LORA + 2 * LANES]
    ms = jnp.sum(kr * kr, axis=-1, keepdims=True) * (1.0 / (MLA_HEADS * MLA_ROPE))
    rk = lax.rsqrt(ms + NORM_EPS)
    k1 = kr[:, :LANES] * rk * krg_ref[0:1, :]
    k2 = kr[:, LANES:] * rk * krg_ref[1:2, :]
    k1r = k1 * cos - k2 * sin
    k2r = k2 * cos + k1 * sin
    kcat_ref[:, :MLA_KV_LORA] = ckv.astype(BF16)
    kcat_ref[:, MLA_KV_LORA:MLA_KV_LORA + LANES] = k1r.astype(BF16)
    kcat_ref[:, MLA_KV_LORA + LANES:] = k2r.astype(BF16)
    std = jnp.where(lane < MLA_HALF, k1r, pltpu.roll(k2r, MLA_HALF, 1))
    krope_ref[...] = std[:, :MLA_ROPE]


def mla_prep(p_mla, cos, sin, wts, *, tm):
    m = p_mla.shape[0]
    nt = cos.shape[0] // tm
    row = lambda w: pl.BlockSpec((tm, w), lambda i: (i, 0))
    res = lambda a: pl.BlockSpec(a.shape, lambda i: (0,) * a.ndim)
    tab = pl.BlockSpec((tm, LANES), lambda i: (i % nt, 0))
    consts = [wts['cq_g'], wts['ckv_g'], wts['w_uq'], wts['qn_g'], wts['qr_g'], wts['kr_g'], wts['w_uk'],
              wts['bd64_512'], wts['bd16']]
    return pl.pallas_call(
        _mla_prep_kernel,
        grid=(m // tm,),
        in_specs=[row(MLA_COLS), tab, tab] + [res(a) for a in consts],
        out_specs=[row(MLA_HEADS * MLA_QK), row(MLA_QK), row(MLA_KV_LORA), row(MLA_ROPE)],
        out_shape=[jax.ShapeDtypeStruct((m, MLA_HEADS * MLA_QK), BF16), jax.ShapeDtypeStruct((m, MLA_QK), BF16),
                   jax.ShapeDtypeStruct((m, MLA_KV_LORA), F32), jax.ShapeDtypeStruct((m, MLA_ROPE), F32)],
        compiler_params=_cparams(("parallel",)),
        name="mla_prep",
    )(p_mla, cos, sin, *consts)


def _mla_flash_kernel(qi_ref, kj_ref, q_ref, k_ref, wuv_ref, o_ref, m_ref, l_ref, acc_ref, *, tq, tk):
    p = pl.program_id(1)
    qi = qi_ref[p]
    kj = kj_ref[p]
    rows = MLA_HEADS * tq

    @pl.when(kj == 0)
    def _():
        m_ref[...] = jnp.full(m_ref.shape, NEG, F32)
        l_ref[...] = jnp.zeros(l_ref.shape, F32)
        acc_ref[...] = jnp.zeros(acc_ref.shape, F32)

    k = k_ref[...]
    kv = k[:, :MLA_KV_LORA]
    kpos = kj * tk + lax.broadcasted_iota(jnp.int32, (1, tk), 1)
    for h0 in range(0, MLA_HEADS, FLASH_HEADS_PER_BLOCK):
        sub = FLASH_HEADS_PER_BLOCK * tq
        rs = slice(h0 * tq, h0 * tq + sub)
        q = jnp.concatenate([q_ref[:, h * MLA_QK:(h + 1) * MLA_QK] for h in range(h0, h0 + FLASH_HEADS_PER_BLOCK)],
                            axis=0)
        s = lax.dot_general(q, k, _NT, preferred_element_type=F32) * MLA_SCALE
        qpos = qi * tq + (lax.broadcasted_iota(jnp.int32, (sub, 1), 0) & (tq - 1))
        s = jnp.where(kpos <= qpos, s, NEG)
        m_prev = m_ref[rs]
        m_new = jnp.maximum(m_prev, jnp.max(s, axis=1, keepdims=True))
        alpha = jnp.exp(m_prev - m_new)
        pexp = jnp.exp(s - m_new)
        l_ref[rs] = alpha * l_ref[rs] + jnp.sum(pexp, axis=1, keepdims=True)
        acc_ref[rs] = alpha * acc_ref[rs] + jnp.dot(pexp.astype(BF16), kv, preferred_element_type=F32)
        m_ref[rs] = m_new

    @pl.when(kj == (qi * tq + tq - 1) // tk)
    def _():
        lat = (acc_ref[...] / l_ref[...]).astype(BF16)
        outs = [jnp.dot(lat[h * tq:(h + 1) * tq], wuv_ref[h], preferred_element_type=F32)
                for h in range(MLA_HEADS)]
        o_ref[...] = jnp.concatenate(outs, axis=-1)


def mla_flash(q_cat, k_cat, w_uv, n, *, tq=128, tk=1024):
    t = q_cat.shape[0] // n
    assert t % tq == 0 and t % tk == 0 and (tq & (tq - 1)) == 0
    pairs = [(i, j) for i in range(t // tq) for j in range((i * tq + tq - 1) // tk + 1)]
    qi_tab = jnp.asarray([a for a, _ in pairs], jnp.int32)
    kj_tab = jnp.asarray([b for _, b in pairs], jnp.int32)
    rows = MLA_HEADS * tq
    nq, nk = t // tq, t // tk
    grid_spec = pltpu.PrefetchScalarGridSpec(
        num_scalar_prefetch=2,
        grid=(n, len(pairs)),
        in_specs=[pl.BlockSpec((tq, MLA_HEADS * MLA_QK), lambda b, p, qi, kj: (b * nq + qi[p], 0)),
                  pl.BlockSpec((tk, MLA_QK), lambda b, p, qi, kj: (b * nk + kj[p], 0)),
                  pl.BlockSpec((MLA_HEADS, MLA_KV_LORA, MLA_V), lambda b, p, qi, kj: (0, 0, 0))],
        out_specs=pl.BlockSpec((tq, MLA_HEADS * MLA_V), lambda b, p, qi, kj: (b * nq + qi[p], 0)),
        scratch_shapes=[pltpu.VMEM((rows, 1), F32), pltpu.VMEM((rows, 1), F32),
                        pltpu.VMEM((rows, MLA_KV_LORA), F32)])
    return pl.pallas_call(
        functools.partial(_mla_flash_kernel, tq=tq, tk=tk),
        grid_spec=grid_spec,
        out_shape=jax.ShapeDtypeStruct((n * t, MLA_HEADS * MLA_V), F32),
        compiler_params=_cparams(("parallel", "arbitrary")),
        name="mla_flash",
    )(qi_tab, kj_tab, q_cat, k_cat, w_uv)


def _mla_decode_kernel(pt_ref, ql_ref, qr_ref, cnew_ref, rnew_ref, wuv_ref, *refs, pg, t_s):
    ck_refs = refs[:pg]
    kr_refs = refs[pg:2 * pg]
    o_ref, m_ref, l_ref, acc_ref = refs[2 * pg:]
    j = pl.program_id(1)
    rows = ql_ref.shape[1]
    tp = rows // MLA_HEADS

    @pl.when(j == 0)
    def _():
        m_ref[...] = jnp.full(m_ref.shape, NEG, F32)
        l_ref[...] = jnp.zeros(l_ref.shape, F32)
        acc_ref[...] = jnp.zeros(acc_ref.shape, F32)

    ql = ql_ref[0]
    qr = qr_ref[0]
    cks, ss = [], []
    for i in range(pg):
        ck = ck_refs[i][0, 0].astype(BF16)
        kr_t = kr_refs[i][0, 0].astype(BF16)
        cks.append(ck)
        ss.append((lax.dot_general(ql, ck, _NT, preferred_element_type=F32)
                   + jnp.dot(qr, kr_t, preferred_element_type=F32)) * MLA_SCALE)
    gsz = pg // DECODE_PAGE_GROUPS
    parts = []
    for g in range(DECODE_PAGE_GROUPS):
        sg = jnp.concatenate(ss[g * gsz:(g + 1) * gsz], axis=1)
        mg = jnp.max(sg, axis=1, keepdims=True)
        pg_exp = jnp.exp(sg - mg)
        page = sg.shape[1] // gsz
        pv = None
        for i in range(gsz):
            term = jnp.dot(pg_exp[:, i * page:(i + 1) * page].astype(BF16), cks[g * gsz + i],
                           preferred_element_type=F32)
            pv = term if pv is None else pv + term
        parts.append((mg, jnp.sum(pg_exp, axis=1, keepdims=True), pv))
    m_prev = m_ref[...]
    m_new = m_prev
    for mg, _, _ in parts:
        m_new = jnp.maximum(m_new, mg)
    alpha = jnp.exp(m_prev - m_new)
    l_new = alpha * l_ref[...]
    acc_new = alpha * acc_ref[...]
    for mg, lg, pv in parts:
        w = jnp.exp(mg - m_new)
        l_new = l_new + w * lg
        acc_new = acc_new + w * pv
    l_ref[...] = l_new
    acc_ref[...] = acc_new
    m_ref[...] = m_new

    @pl.when(j == pl.num_programs(1) - 1)
    def _():
        cn = cnew_ref[0]
        rn = rnew_ref[0]
        sn = (lax.dot_general(ql, cn, _NT, preferred_element_type=F32)
              + lax.dot_general(qr, rn, _NT, preferred_element_type=F32)) * MLA_SCALE
        tq = lax.broadcasted_iota(jnp.int32, (rows, 1), 0) // MLA_HEADS
        tk = lax.broadcasted_iota(jnp.int32, (1, tp), 1)
        sn = jnp.where((tk <= tq) & (tk < t_s), sn, NEG)
        m2 = jnp.maximum(m_ref[...], jnp.max(sn, axis=1, keepdims=True))
        a2 = jnp.exp(m_ref[...] - m2)
        pn = jnp.exp(sn - m2)
        l2 = a2 * l_ref[...] + jnp.sum(pn, axis=1, keepdims=True)
        acc2 = a2 * acc_ref[...] + jnp.dot(pn.astype(BF16), cn, preferred_element_type=F32)
        lat = (acc2 / l2).astype(BF16)
        full = jnp.dot(lat, wuv_ref[...], preferred_element_type=F32)
        hrow = lax.broadcasted_iota(jnp.int32, (rows, 1), 0) % MLA_HEADS
        hcol = lax.broadcasted_iota(jnp.int32, (1, MLA_HEADS * MLA_V), 1) // MLA_V
        full = jnp.where(hrow == hcol, full, 0.0)
        o_ref[0] = jnp.sum(full.reshape(tp, MLA_HEADS, MLA_HEADS * MLA_V), axis=1)


def mla_decode(page_table, q_lat, q_rope, ckv_new, kr_new, w_uv_all, cache_ckv, cache_kr_t, l, *, t_s, pg=16):
    b, rows, _ = q_lat.shape
    tp = rows // MLA_HEADS
    n_pages = page_table.shape[1]
    page = cache_ckv.shape[2]
    assert n_pages % pg == 0
    pt = page_table.reshape(-1)

    def page_spec(shape, i):
        return pl.BlockSpec((1, 1) + shape, lambda s, j, pt_ref: (l, pt_ref[s * n_pages + j * pg + i], 0, 0))

    seq = lambda a: pl.BlockSpec((1,) + a.shape[1:], lambda s, j, pt_ref: (s, 0, 0))
    grid_spec = pltpu.PrefetchScalarGridSpec(
        num_scalar_prefetch=1,
        grid=(b, n_pages // pg),
        in_specs=[seq(q_lat), seq(q_rope), seq(ckv_new), seq(kr_new),
                  pl.BlockSpec(w_uv_all.shape, lambda s, j, pt_ref: (0, 0))]
                 + [page_spec((page, MLA_KV_LORA), i) for i in range(pg)]
                 + [page_spec((MLA_ROPE, page), i) for i in range(pg)],
        out_specs=pl.BlockSpec((1, tp, MLA_HEADS * MLA_V), lambda s, j, pt_ref: (s, 0, 0)),
        scratch_shapes=[pltpu.VMEM((rows, 1), F32), pltpu.VMEM((rows, 1), F32),
                        pltpu.VMEM((rows, MLA_KV_LORA), F32)])
    return pl.pallas_call(
        functools.partial(_mla_decode_kernel, pg=pg, t_s=t_s),
        grid_spec=grid_spec,
        out_shape=jax.ShapeDtypeStruct((b, tp, MLA_HEADS * MLA_V), F32),
        compiler_params=_cparams(("parallel", "arbitrary")),
        name="mla_decode",
    )(pt, q_lat, q_rope, ckv_new, kr_new, w_uv_all, *([cache_ckv] * pg), *([cache_kr_t] * pg))


def _rope_lanes(x, c, s1, s2):
    n = x.shape[1]
    half = NSA_ROT // 2
    reps = n // LANES
    tile = (lambda a: jnp.concatenate([a] * reps, axis=1)) if reps > 1 else (lambda a: a)
    return x * tile(c) + pltpu.roll(x, n - half, 1) * tile(s1) + pltpu.roll(x, half, 1) * tile(s2)


def _nsa_prep_kernel(p_ref, c_ref, s1_ref, s2_ref, qng_ref, kng_ref, bdq_ref, bdk_ref, wt_ref, pbd_ref,
                     q_ref, kvc_ref, kvs_ref, kvw_ref, gate_ref, *cmp_out, compress):
    p = p_ref[...]
    c, s1, s2 = c_ref[...], s1_ref[...], s2_ref[...]
    q = p[:, :NSA_QW]
    q = q * lax.rsqrt(_group_sum(q * q, bdq_ref[...]) * (1.0 / NSA_DH) + NORM_EPS) * qng_ref[...]
    q_ref[...] = (_rope_lanes(q, c, s1, s2) * NSA_SCALE).astype(BF16)
    outs = (kvc_ref, kvs_ref, kvw_ref)
    kc = None
    for j in range(3):
        base = NSA_QW + j * NSA_KV_COLS
        k = p[:, base:base + LANES]
        k = k * lax.rsqrt(_group_sum(k * k, bdk_ref[...]) * (1.0 / NSA_DH) + NORM_EPS) * kng_ref[j:j + 1, :]
        k = _rope_lanes(k, c, s1, s2)
        outs[j][:, :LANES] = k
        outs[j][:, LANES:] = p[:, base + LANES:base + NSA_KV_COLS]
        if j == 0:
            kc = k
    gate_ref[...] = jax.nn.sigmoid(p[:, NSA_QW + 3 * NSA_KV_COLS:])
    if compress:
        tm = p.shape[0]
        rows = jnp.concatenate([kc, p[:, NSA_QW + LANES:NSA_QW + NSA_KV_COLS]], axis=1)
        pooled = jnp.sum((rows * wt_ref[...]).reshape(tm // NSA_CMP_BLOCK, NSA_CMP_BLOCK, NSA_KV_COLS), axis=1)
        cmp_out[0][...] = jnp.dot(pooled.astype(BF16), pbd_ref[...], preferred_element_type=F32)


def nsa_prep(p_nsa, tabs, wts, *, tm, compress):
    m = p_nsa.shape[0]
    nt = tabs[0].shape[0] // tm
    row = lambda w: pl.BlockSpec((tm, w), lambda i: (i, 0))
    res = lambda a: pl.BlockSpec(a.shape, lambda i: (0,) * a.ndim)
    tab = pl.BlockSpec((tm, LANES), lambda i: (i % nt, 0))
    consts = [wts['nsa_qn_g'], wts['nsa_kn_g'], wts['bd64_512'], wts['bd64_128'], wts['cmp_w'][:tm], wts['cmp_proj']]
    out_specs = [row(NSA_QW), row(NSA_KV_COLS), row(NSA_KV_COLS), row(NSA_KV_COLS), row(LANES)]
    out_shape = [jax.ShapeDtypeStruct((m, NSA_QW), BF16)] + [jax.ShapeDtypeStruct((m, NSA_KV_COLS), F32)] * 3 \
        + [jax.ShapeDtypeStruct((m, LANES), F32)]
    if compress:
        out_specs.append(pl.BlockSpec((tm // NSA_CMP_BLOCK, NSA_KV_COLS), lambda i: (i, 0)))
        out_shape.append(jax.ShapeDtypeStruct((m // NSA_CMP_BLOCK, NSA_KV_COLS), F32))
    return pl.pallas_call(
        functools.partial(_nsa_prep_kernel, compress=compress),
        grid=(m // tm,),
        in_specs=[row(NSA_COLS), tab, tab, tab] + [res(a) for a in consts],
        out_specs=out_specs,
        out_shape=out_shape,
        compiler_params=_cparams(("parallel",)),
        name="nsa_prep",
    )(p_nsa, *tabs, *consts)


def _group_rows(q, g):
    lane = lax.broadcasted_iota(jnp.int32, (1, LANES), 1)
    in_half = (lane // NSA_DH) == g
    zero = jnp.zeros((), q.dtype)
    parts = []
    for pidx in range(NSA_HPG):
        h = g * NSA_HPG + pidx
        tile = q[:, (h // 2) * LANES:(h // 2 + 1) * LANES]
        if h % 2 != g:
            tile = jnp.concatenate([tile[:, NSA_DH:], tile[:, :NSA_DH]], axis=1)
        parts.append(jnp.where(in_half, tile, zero))
    return jnp.concatenate(parts, axis=0)


def _place_heads(o_groups, t):
    lane = lax.broadcasted_iota(jnp.int32, (1, LANES), 1)
    low = lane < NSA_DH
    tiles = []
    for j in range(NSA_HEADS // 2):
        pair = []
        for h in (2 * j, 2 * j + 1):
            g, pidx = divmod(h, NSA_HPG)
            x = o_groups[g][pidx * t:(pidx + 1) * t]
            if h % 2 != g:
                x = jnp.concatenate([x[:, NSA_DH:], x[:, :NSA_DH]], axis=1)
            pair.append(x)
        tiles.append(jnp.where(low, pair[0], pair[1]))
    return jnp.concatenate(tiles, axis=1)


def _expand_gates(gs, exp_ref, j):
    hi = gs.astype(BF16)
    lo = (gs - hi.astype(F32)).astype(BF16)
    e = exp_ref[j]
    return jnp.dot(hi, e, preferred_element_type=F32) + jnp.dot(lo, e, preferred_element_type=F32)


def _top_blocks_t(score_t, blk_row_f, n_rows):
    sel_t = jnp.zeros(score_t.shape, F32)
    for _ in range(NSA_TOPK):
        mx = jnp.max(score_t, axis=0, keepdims=True)
        first = jnp.min(jnp.where(score_t == mx, blk_row_f, float(n_rows)), axis=0, keepdims=True)
        pick = blk_row_f == first
        sel_t = jnp.where(pick, 1.0, sel_t)
        score_t = jnp.where(pick, -3e38, score_t)
    return sel_t


def _nsa_prompt_kernel(q_ref, gate_ref, kvc_ref, kvs_ref, kvw_ref, pool_ref, exp_ref, o_ref, *, tq, tk, nc, n_sel):
    start = pl.program_id(1) * tq
    grows = NSA_HPG * tq
    q = q_ref[...]
    tokpos = start + lax.broadcasted_iota(jnp.int32, (tq, 1), 0)
    qpos = start + (lax.broadcasted_iota(jnp.int32, (grows, 1), 0) & (tq - 1))
    c_end = lax.broadcasted_iota(jnp.int32, (1, nc), 1) * NSA_CMP_BLOCK + (NSA_CMP_BLOCK - 1)
    blk_lane = lax.broadcasted_iota(jnp.int32, (1, LANES), 1)
    blk_row_f = lax.broadcasted_iota(jnp.int32, (LANES, tq), 0).astype(F32)
    cur = tokpos // NSA_SEL_BLOCK
    n_kt = (start + tq - 1) // tk + 1
    lo = jnp.maximum(start - NSA_WINDOW, 0)
    wlen = NSA_WINDOW + tq
    kvc = kvc_ref[0]
    kc = kvc[:, :LANES].astype(BF16)
    vc = kvc[:, LANES:].astype(BF16)
    o_cmp, o_sel, o_win = [], [], []

    for g in range(NSA_GROUPS):
        qg = _group_rows(q, g)

        sc = lax.dot_general(qg, kc, _NT, preferred_element_type=F32)
        pc = _softmax_rows(sc, c_end <= qpos)
        o_cmp.append(jnp.dot(pc.astype(BF16), vc, preferred_element_type=F32))

        psum = pc[0:tq]
        for h in range(1, NSA_HPG):
            psum = psum + pc[h * tq:(h + 1) * tq]
        imp = _group_sum(psum, pool_ref[...])
        forced = (blk_lane == 0) | (blk_lane == cur) | (blk_lane == cur - 1)
        score = jnp.where(blk_lane > cur, -1e9, imp + NSA_FORCE * forced.astype(F32))
        score = jnp.where(blk_lane < n_sel, score, -3e38)
        sel = _top_blocks_t(score.T, blk_row_f, LANES).T.astype(BF16)

        def sel_step(kt, carry):
            m_prev, l_prev, acc = carry
            off = pl.multiple_of(kt * tk, tk)
            kv = kvs_ref[0, pl.ds(off, tk), :]
            k = kv[:, :LANES].astype(BF16)
            v = kv[:, LANES:].astype(BF16)
            s = lax.dot_general(qg, k, _NT, preferred_element_type=F32)
            keyi = kt * tk + lax.broadcasted_iota(jnp.int32, (1, tk), 1)
            expand = (lax.broadcasted_iota(jnp.int32, (LANES, tk), 0) == keyi // NSA_SEL_BLOCK).astype(BF16)
            selm = jnp.dot(sel, expand, preferred_element_type=F32)
            valid = (selm > 0.5) & (keyi <= tokpos)
            s3 = jnp.where(valid[None], s.reshape(NSA_HPG, tq, tk), NEG)
            m_new = jnp.maximum(m_prev, jnp.max(s3, axis=-1, keepdims=True))
            alpha = jnp.exp(m_prev - m_new)
            p3 = jnp.exp(s3 - m_new)
            l_new = alpha * l_prev + jnp.sum(p3, axis=-1, keepdims=True)
            pv = jnp.dot(p3.reshape(grows, tk).astype(BF16), v, preferred_element_type=F32)
            return m_new, l_new, alpha * acc + pv.reshape(NSA_HPG, tq, LANES)

        init = (jnp.full((NSA_HPG, tq, 1), NEG, F32), jnp.zeros((NSA_HPG, tq, 1), F32),
                jnp.zeros((NSA_HPG, tq, LANES), F32))
        _, l_fin, acc = lax.fori_loop(0, n_kt, sel_step, init)
        o_sel.append((acc / l_fin).reshape(grows, LANES))

        woff = pl.multiple_of(lo, tq)
        kvw = kvw_ref[0, pl.ds(woff, wlen), :]
        sw = lax.dot_general(qg, kvw[:, :LANES].astype(BF16), _NT, preferred_element_type=F32)
        dpos = tokpos - (lo + lax.broadcasted_iota(jnp.int32, (1, wlen), 1))
        validw = (dpos >= 0) & (dpos < NSA_WINDOW)
        sw3 = sw.reshape(NSA_HPG, tq, wlen)
        pw = _softmax_rows(sw3, jnp.broadcast_to(validw[None], sw3.shape))
        o_win.append(jnp.dot(pw.reshape(grows, wlen).astype(BF16), kvw[:, LANES:].astype(BF16),
                             preferred_element_type=F32))

    gs = gate_ref[...]
    o_ref[...] = (_expand_gates(gs, exp_ref, 0) * _place_heads(o_cmp, tq)
                  + _expand_gates(gs, exp_ref, 1) * _place_heads(o_sel, tq)
                  + _expand_gates(gs, exp_ref, 2) * _place_heads(o_win, tq))


def nsa_prompt(q8, gates, kvc, kv_s, kv_w, pool, gate_exp, n, *, tq=128, tk=512):
    t = q8.shape[0] // n
    nc = kvc.shape[1]
    n_sel = t // NSA_SEL_BLOCK
    assert nc == LANES and 2 * n_sel == nc and t % tk == 0 and t >= NSA_WINDOW + tq
    nq = t // tq
    full = lambda a: pl.BlockSpec((1,) + a.shape[1:], lambda b, i: (b, 0, 0))
    res = lambda a: pl.BlockSpec(a.shape, lambda b, i: (0,) * a.ndim)
    return pl.pallas_call(
        functools.partial(_nsa_prompt_kernel, tq=tq, tk=tk, nc=nc, n_sel=n_sel),
        grid=(n, nq),
        in_specs=[pl.BlockSpec((tq, NSA_QW), lambda b, i: (b * nq + i, 0)),
                  pl.BlockSpec((tq, LANES), lambda b, i: (b * nq + i, 0)),
                  full(kvc), full(kv_s), full(kv_w), res(pool), res(gate_exp)],
        out_specs=pl.BlockSpec((tq, NSA_QW), lambda b, i: (b * nq + i, 0)),
        out_shape=jax.ShapeDtypeStruct((n * t, NSA_QW), F32),
        compiler_params=_cparams(("parallel", "arbitrary")),
        name="nsa_prompt",
    )(q8, gates, kvc, kv_s, kv_w, pool, gate_exp)


def _nsa_decode_kernel(pt_ref, q_ref, gate_ref, snew_ref, wnew_ref, win_ref, wt_ref, pbd_ref, pool_ref, exp_ref,
                       e0_ref, *refs, pg, n_steps, t_s, past):
    cmp_refs = refs[:pg]
    sel_refs = refs[pg:2 * pg]
    o_ref = refs[2 * pg]
    pooled_ref, selstep_ref, ocmp_ref, m_ref, l_ref, acc_ref = refs[2 * pg + 1:]
    j = pl.program_id(1)
    tp = q_ref.shape[1]
    grows = NSA_HPG * tp
    rows = NSA_GROUPS * grows
    page = cmp_refs[0].shape[3]
    nc = past // NSA_CMP_BLOCK
    nblk = pool_ref.shape[1]
    bps = pg * page // NSA_SEL_BLOCK
    q = q_ref[0]
    qs = jnp.concatenate([_group_rows(q, g) for g in range(NSA_GROUPS)], axis=0)
    tokpos = past + lax.broadcasted_iota(jnp.int32, (tp, 1), 0)
    rowtok = lax.broadcasted_iota(jnp.int32, (rows, 1), 0) & (tp - 1)

    cps = pg * page // NSA_CMP_BLOCK

    @pl.when(j < n_steps)
    def _():
        xw = jnp.concatenate([r[0, 0] * wt_ref[...] for r in cmp_refs], axis=1)
        pooled_ref[jnp.minimum(j, n_steps - 1)] = _group_sum(xw, e0_ref[...])

    @pl.when(j == n_steps - 1)
    def _():
        pooled = jnp.concatenate([pooled_ref[s][:, :cps] for s in range(n_steps)], axis=1)
        kvc_t = jnp.dot(pbd_ref[...], pooled.astype(BF16), preferred_element_type=F32)
        kc_t = kvc_t[:LANES].astype(BF16)
        vc_t = kvc_t[LANES:].astype(BF16)
        c_end = lax.broadcasted_iota(jnp.int32, (1, nc), 1) * NSA_CMP_BLOCK + (NSA_CMP_BLOCK - 1)
        sc = jnp.dot(qs, kc_t, preferred_element_type=F32)
        pc = _softmax_rows(sc, c_end <= past + rowtok)
        ocmp_ref[...] = lax.dot_general(pc.astype(BF16), vc_t, _NT, preferred_element_type=F32)
        blk_lane = lax.broadcasted_iota(jnp.int32, (1, nblk), 1)
        blk_lane_f = blk_lane.astype(F32)
        cur = tokpos // NSA_SEL_BLOCK
        n_sel = -(-(past + t_s) // NSA_SEL_BLOCK)
        forced = (blk_lane == 0) | (blk_lane == cur) | (blk_lane == cur - 1)
        sels = []
        for g in range(NSA_GROUPS):
            psum = pc[g * grows:g * grows + tp]
            for h in range(1, NSA_HPG):
                psum = psum + pc[g * grows + h * tp:g * grows + (h + 1) * tp]
            imp = _group_sum(psum, pool_ref[...])
            score = jnp.where(blk_lane > cur, -1e9, imp + NSA_FORCE * forced.astype(F32))
            score = jnp.where(blk_lane < n_sel, score, -3e38)
            sel = jnp.zeros((tp, nblk), F32)
            for _ in range(NSA_TOPK):
                mx = jnp.max(score, axis=1, keepdims=True)
                first = jnp.min(jnp.where(score == mx, blk_lane_f, float(nblk)), axis=1, keepdims=True)
                pick = blk_lane_f == first
                sel = jnp.where(pick, 1.0, sel)
                score = jnp.where(pick, -3e38, score)
            sels.append(jnp.concatenate([sel] * NSA_HPG, axis=0))
        sel_rows = jnp.concatenate(sels, axis=0)
        zpad = jnp.zeros((rows, LANES - bps), F32)
        for s in range(nblk // bps):
            selstep_ref[s] = jnp.concatenate([sel_rows[:, s * bps:(s + 1) * bps], zpad], axis=1)
        m_ref[...] = jnp.full(m_ref.shape, NEG, F32)
        l_ref[...] = jnp.zeros(l_ref.shape, F32)
        acc_ref[...] = jnp.zeros(acc_ref.shape, F32)

    def online(s, valid):
        s = jnp.where(valid, s, NEG)
        m_prev = m_ref[...]
        m_new = jnp.maximum(m_prev, jnp.max(s, axis=-1, keepdims=True))
        alpha = jnp.exp(m_prev - m_new)
        p = jnp.where(valid, jnp.exp(s - m_new), 0.0)
        l_ref[...] = alpha * l_ref[...] + jnp.sum(p, axis=-1, keepdims=True)
        m_ref[...] = m_new
        return alpha, p.astype(BF16)

    @pl.when(j >= n_steps)
    def _():
        jj = j - n_steps
        tiles = [r[0, 0] for r in sel_refs]
        s = jnp.concatenate([jnp.dot(qs, x[:LANES].astype(BF16), preferred_element_type=F32) for x in tiles], axis=1)
        tile = selstep_ref[jj]
        lane = lax.broadcasted_iota(jnp.int32, (1, page), 1)
        bpp = page // NSA_SEL_BLOCK
        masks = []
        for i in range(pg):
            m_i = tile[:, i * bpp:i * bpp + 1]
            for b in range(1, bpp):
                m_i = jnp.where(lane >= b * NSA_SEL_BLOCK, tile[:, i * bpp + b:i * bpp + b + 1], m_i)
            masks.append(jnp.broadcast_to(m_i, (rows, page)))
        valid = jnp.concatenate(masks, axis=1) > 0.5
        alpha, p = online(s, valid)
        pv = None
        for i, x in enumerate(tiles):
            term = lax.dot_general(p[:, i * page:(i + 1) * page], x[LANES:].astype(BF16), _NT,
                                   preferred_element_type=F32)
            pv = term if pv is None else pv + term
        acc_ref[...] = alpha * acc_ref[...] + pv

    @pl.when(j == 2 * n_steps - 1)
    def _():
        snew = snew_ref[0]
        wnew = wnew_ref[0]
        tk = lax.broadcasted_iota(jnp.int32, (1, tp), 1)
        causal_new = (tk <= rowtok) & (tk < t_s)
        new_blk = past // NSA_SEL_BLOCK
        lane = lax.broadcasted_iota(jnp.int32, (1, LANES), 1)
        last = selstep_ref[new_blk // bps]
        picked = jnp.sum(jnp.where(lane == new_blk % bps, last, 0.0), axis=1, keepdims=True) > 0.5
        s = lax.dot_general(qs, snew[:, :LANES].astype(BF16), _NT, preferred_element_type=F32)
        alpha, p = online(s, causal_new & picked)
        acc = alpha * acc_ref[...] + jnp.dot(p, snew[:, LANES:].astype(BF16), preferred_element_type=F32)
        o_sel = acc / jnp.maximum(l_ref[...], 1e-30)

        win = win_ref[0, 0]
        win_s = win.shape[1]
        wi = lax.broadcasted_iota(jnp.int32, (1, win_s), 1)
        dpos_c = (past + rowtok) - (past - win_s + wi)
        valid_c = (dpos_c >= 0) & (dpos_c < NSA_WINDOW)
        sc = jnp.dot(qs, win[:LANES].astype(BF16), preferred_element_type=F32)
        sn = lax.dot_general(qs, wnew[:, :LANES].astype(BF16), _NT, preferred_element_type=F32)
        sc = jnp.where(valid_c, sc, NEG)
        sn = jnp.where(causal_new, sn, NEG)
        mw = jnp.maximum(jnp.max(sc, axis=-1, keepdims=True), jnp.max(sn, axis=-1, keepdims=True))
        pcw = jnp.where(valid_c, jnp.exp(sc - mw), 0.0)
        pnw = jnp.where(causal_new, jnp.exp(sn - mw), 0.0)
        den = jnp.maximum(jnp.sum(pcw, axis=-1, keepdims=True) + jnp.sum(pnw, axis=-1, keepdims=True), 1e-30)
        o_win = (lax.dot_general(pcw.astype(BF16), win[LANES:].astype(BF16), _NT, preferred_element_type=F32)
                 + jnp.dot(pnw.astype(BF16), wnew[:, LANES:].astype(BF16), preferred_element_type=F32)) / den
        split = lambda a: [a[g * grows:(g + 1) * grows] for g in range(NSA_GROUPS)]
        gs = gate_ref[0]
        o_ref[0] = (_expand_gates(gs, exp_ref, 0) * _place_heads(split(ocmp_ref[...]), tp)
                    + _expand_gates(gs, exp_ref, 1) * _place_heads(split(o_sel), tp)
                    + _expand_gates(gs, exp_ref, 2) * _place_heads(split(o_win), tp))


def nsa_decode(page_table, q8, gates, kvs_new, kvw_new, cache_cmp_t, cache_sel_t, cache_win_t, wts, l, *, t_s, pg=16):
    b, tp, _ = q8.shape
    n_pages = page_table.shape[1]
    page = cache_cmp_t.shape[3]
    past = n_pages * page
    n_steps = n_pages // pg
    nc = past // NSA_CMP_BLOCK
    assert (past + t_s) // NSA_CMP_BLOCK == nc
    nblk = wts['pool_s'].shape[1]
    bps = pg * page // NSA_SEL_BLOCK
    assert nblk % bps == 0 and bps <= LANES and (past // NSA_SEL_BLOCK) // bps < nblk // bps
    rows = NSA_HEADS * tp
    pt = page_table.reshape(-1)
    assert pg * page // NSA_CMP_BLOCK <= LANES
    e0 = jnp.asarray((np.arange(pg * page)[:, None] // NSA_CMP_BLOCK) == np.arange(LANES)[None, :], BF16)

    def cmp_spec(i):
        return pl.BlockSpec((1, 1, NSA_KV_COLS, page),
                            lambda s, j, pt_ref: (l, pt_ref[s * n_pages + jnp.minimum(j, n_steps - 1) * pg + i], 0, 0))

    def sel_spec(i):
        return pl.BlockSpec((1, 1, NSA_KV_COLS, page),
                            lambda s, j, pt_ref: (l, pt_ref[s * n_pages + jnp.maximum(j - n_steps, 0) * pg + i], 0, 0))

    seq = lambda a: pl.BlockSpec((1,) + a.shape[1:], lambda s, j, pt_ref: (s, 0, 0))
    res = lambda a: pl.BlockSpec(a.shape, lambda s, j, pt_ref: (0,) * a.ndim)
    consts = [wts['cmp_w_t'], wts['cmp_proj_t'], wts['pool_s'], wts['gate_exp'], e0]
    grid_spec = pltpu.PrefetchScalarGridSpec(
        num_scalar_prefetch=1,
        grid=(b, 2 * n_steps),
        in_specs=[seq(q8), seq(gates), seq(kvs_new), seq(kvw_new),
                  pl.BlockSpec((1, 1) + cache_win_t.shape[2:], lambda s, j, pt_ref: (l, s, 0, 0))]
                 + [res(a) for a in consts]
                 + [cmp_spec(i) for i in range(pg)] + [sel_spec(i) for i in range(pg)],
        out_specs=pl.BlockSpec((1, tp, NSA_QW), lambda s, j, pt_ref: (s, 0, 0)),
        scratch_shapes=[pltpu.VMEM((n_steps, NSA_KV_COLS, LANES), F32),
                        pltpu.VMEM((nblk // bps, rows, LANES), F32),
                        pltpu.VMEM((rows, LANES), F32),
                        pltpu.VMEM((rows, 1), F32),
                        pltpu.VMEM((rows, 1), F32),
                        pltpu.VMEM((rows, LANES), F32)])
    return pl.pallas_call(
        functools.partial(_nsa_decode_kernel, pg=pg, n_steps=n_steps, t_s=t_s, past=past),
        grid_spec=grid_spec,
        out_shape=jax.ShapeDtypeStruct((b, tp, NSA_QW), F32),
        compiler_params=_cparams(("parallel", "arbitrary")),
        name="nsa_decode",
    )(pt, q8, gates, kvs_new, kvw_new, cache_win_t, *consts, *([cache_cmp_t] * pg), *([cache_sel_t] * pg))


def _merge_kernel(x_ref, oa_ref, ob_ref, oc_ref, br_ref, pa_ref, pb_ref, pc_ref, wo_ref, y_ref):
    d = D_MODEL
    br = br_ref[...]
    u = (jax.nn.sigmoid(br[:, 0:d]) * jnp.dot(oa_ref[...].astype(BF16), pa_ref[...], preferred_element_type=F32)
         + jax.nn.sigmoid(br[:, d:2 * d]) * jnp.dot(ob_ref[...].astype(BF16), pb_ref[...], preferred_element_type=F32)
         + jax.nn.sigmoid(br[:, 2 * d:3 * d]) * jnp.dot(oc_ref[...].astype(BF16), pc_ref[...], preferred_element_type=F32))
    y_ref[...] = x_ref[...] + jnp.dot(u.astype(BF16), wo_ref[...], preferred_element_type=F32)


def merge(x, o_a, o_b, o_c, br, p_a, p_b, p_c, w_out, *, tm=256):
    m, d = x.shape
    tm = min(tm, m)
    bw = o_a.shape[1]
    row = lambda w: pl.BlockSpec((tm, w), lambda i: (i, 0))
    res = lambda a: pl.BlockSpec(a.shape, lambda i: (0, 0))
    return pl.pallas_call(
        _merge_kernel,
        grid=(m // tm,),
        in_specs=[row(d), row(bw), row(bw), row(bw), row(3 * d), res(p_a), res(p_b), res(p_c), res(w_out)],
        out_specs=row(d),
        out_shape=jax.ShapeDtypeStruct((m, d), F32),
        compiler_params=_cparams(("parallel",)),
        name="merge",
    )(x, o_a, o_b, o_c, br, p_a, p_b, p_c, w_out)


def _router_kernel(x_ref, g_ref, wr_ref, h_ref, route_ref):
    x = x_ref[...]
    ms = jnp.mean(x * x, axis=-1, keepdims=True)
    hb = (x * lax.rsqrt(ms + NORM_EPS) * g_ref[...]).astype(BF16)
    h_ref[...] = hb
    logits = jnp.dot(hb, wr_ref[...], preferred_element_type=F32)
    lane = lax.broadcasted_iota(jnp.int32, (1, ROUTER_PAD), 1)
    lane_f = lane.astype(F32)
    big = float(ROUTER_PAD)

    def top1(vals):
        mx = jnp.max(vals, axis=-1, keepdims=True)
        return mx, jnp.min(jnp.where(vals == mx, lane_f, big), axis=-1, keepdims=True)

    gl = jnp.where(lane < MOE_GROUPS, logits, NEG)
    gmax, gidx = top1(gl)
    g_w = 1.0 / jnp.sum(jnp.where(lane < MOE_GROUPS, jnp.exp(gl - gmax), 0.0), axis=-1, keepdims=True)
    lo = MOE_GROUPS + MOE_EPG * gidx
    el = jnp.where((lane_f >= lo) & (lane_f < lo + MOE_EPG), logits, NEG)
    e1, i1 = top1(el)
    e2, i2 = top1(jnp.where(lane_f == i1, NEG, el))
    z = jnp.exp(e2 - e1)
    w1 = g_w / (1.0 + z)
    w2 = g_w * z / (1.0 + z)
    route_ref[...] = jnp.where(lane == 0, i1 - MOE_GROUPS,
                               jnp.where(lane == 1, i2 - MOE_GROUPS,
                                         jnp.where(lane == 2, w1, jnp.where(lane == 3, w2, 0.0))))


def norm_router(x, g, w_router, *, tm=512):
    m, d = x.shape
    tm = min(tm, m)
    return pl.pallas_call(
        _router_kernel,
        grid=(m // tm,),
        in_specs=[pl.BlockSpec((tm, d), lambda i: (i, 0)),
                  pl.BlockSpec((1, d), lambda i: (0, 0)),
                  pl.BlockSpec((d, ROUTER_PAD), lambda i: (0, 0))],
        out_specs=[pl.BlockSpec((tm, d), lambda i: (i, 0)),
                   pl.BlockSpec((tm, ROUTER_PAD), lambda i: (i, 0))],
        out_shape=[jax.ShapeDtypeStruct((m, d), BF16), jax.ShapeDtypeStruct((m, ROUTER_PAD), F32)],
        compiler_params=_cparams(("parallel",)),
        name="norm_router",
    )(x, g.reshape(1, d), w_router)


def _expert_kernel(be_ref, x_ref, rw_ref, w1_ref, w3_ref, w2_ref, o_ref):
    x = x_ref[...]
    a = jnp.dot(x, w1_ref[0], preferred_element_type=F32)
    b = jnp.dot(x, w3_ref[0], preferred_element_type=F32)
    hmid = _silu(a) * b
    y = jnp.dot(hmid.astype(BF16), w2_ref[0], preferred_element_type=F32)
    o_ref[...] = y * rw_ref[...]


def grouped_experts_blocks(xb, row_w, blk_e, w1, w3, w2):
    rows, d = xb.shape
    n_blk = rows // MOE_BLOCK
    grid_spec = pltpu.PrefetchScalarGridSpec(
        num_scalar_prefetch=1,
        grid=(n_blk,),
        in_specs=[pl.BlockSpec((MOE_BLOCK, d), lambda b, be: (b, 0)),
                  pl.BlockSpec((MOE_BLOCK, 1), lambda b, be: (b, 0)),
                  pl.BlockSpec((1, d, MOE_DFF), lambda b, be: (be[b], 0, 0)),
                  pl.BlockSpec((1, d, MOE_DFF), lambda b, be: (be[b], 0, 0)),
                  pl.BlockSpec((1, MOE_DFF, d), lambda b, be: (be[b], 0, 0))],
        out_specs=pl.BlockSpec((MOE_BLOCK, d), lambda b, be: (b, 0)))
    return pl.pallas_call(
        _expert_kernel,
        grid_spec=grid_spec,
        out_shape=jax.ShapeDtypeStruct((rows, d), F32),
        compiler_params=_cparams(("arbitrary",)),
        name="moe_experts",
    )(blk_e, xb, row_w.reshape(rows, 1), w1, w3, w2)


def hier_moe_residual(x, norm_g, w_router, w1, w3, w2):
    m, d = x.shape
    h, route = norm_router(x, norm_g, w_router)
    eid = route[:, :MOE_TOPK].astype(jnp.int32)
    ew = route[:, MOE_TOPK:2 * MOE_TOPK]

    n_assign = m * MOE_TOPK
    fe = eid.reshape(-1).astype(jnp.int32)
    onehot = (fe[:, None] == jnp.arange(MOE_EXPERTS, dtype=jnp.int32)[None, :]).astype(jnp.int32)
    csum = jnp.cumsum(onehot, axis=0)
    rank = jnp.sum((csum - onehot) * onehot, axis=1)
    counts = csum[-1]
    padded = (counts + MOE_BLOCK - 1) // MOE_BLOCK * MOE_BLOCK
    ends = jnp.cumsum(padded)
    dest = (ends - padded)[fe] + rank
    n_blk = -(-(n_assign + MOE_EXPERTS * (MOE_BLOCK - 1)) // MOE_BLOCK)
    rows = n_blk * MOE_BLOCK
    tok = jnp.arange(n_assign, dtype=jnp.int32) // MOE_TOPK
    row_tok = jnp.full((rows,), m, jnp.int32).at[dest].set(tok)
    row_w = jnp.zeros((rows,), F32).at[dest].set(ew.reshape(-1))
    blk_e = jnp.minimum(jnp.searchsorted(ends, jnp.arange(n_blk, dtype=jnp.int32) * MOE_BLOCK, side='right'),
                        MOE_EXPERTS - 1).astype(jnp.int32)
    xb = jnp.concatenate([h, jnp.zeros((1, d), h.dtype)], axis=0)[row_tok]
    yb = grouped_experts_blocks(xb, row_w, blk_e, w1, w3, w2)
    slot = dest.reshape(m, MOE_TOPK)
    y = yb[slot[:, 0]]
    for k in range(1, MOE_TOPK):
        y = y + yb[slot[:, k]]
    return x + y


def _rope_tables(pos):
    posf = pos.astype(F32)[:, None]
    inv_m = ROPE_THETA ** (-(jnp.arange(MLA_HALF, dtype=F32) * 2.0 / MLA_ROPE))
    ang_m = posf * inv_m[None, :]
    cos_m = jnp.tile(jnp.cos(ang_m), (1, LANES // MLA_HALF))
    sin_m = jnp.tile(jnp.sin(ang_m), (1, LANES // MLA_HALF))
    half = NSA_ROT // 2
    inv_n = ROPE_THETA ** (-(jnp.arange(half, dtype=F32) * 2.0 / NSA_ROT))
    ang_n = posf * inv_n[None, :]
    cn, sn = jnp.cos(ang_n), jnp.sin(ang_n)
    t = pos.shape[0]
    one = jnp.ones((t, NSA_DH - NSA_ROT), F32)
    zero = jnp.zeros((t, NSA_DH - NSA_ROT), F32)
    z8 = jnp.zeros((t, half), F32)
    c = jnp.concatenate([cn, cn, one], axis=1)
    s1 = jnp.concatenate([-sn, z8, zero], axis=1)
    s2 = jnp.concatenate([z8, sn, zero], axis=1)
    rep = lambda a: jnp.tile(a, (1, LANES // NSA_DH))
    return (cos_m, sin_m), (rep(c), rep(s1), rep(s2))


def _layer_weights(l, w_in, raw):
    w = {}
    wl = w_in[l]
    d = wl.shape[0]
    col = lambda i: wl[:, IN_OFF[i]:IN_OFF[i + 1]]
    zeros = lambda n: jnp.zeros((d, n), F32)
    w['w_gdn'] = jnp.concatenate([col(0), col(1), col(2), col(3), zeros(LANES - 2 * GDN_HEADS)], axis=1).astype(BF16)
    kr = col(6)
    kr_rep = jnp.concatenate([jnp.tile(kr[:, :MLA_HALF], (1, MLA_HEADS)), jnp.tile(kr[:, MLA_HALF:], (1, MLA_HEADS))],
                             axis=1)
    w['w_mla'] = jnp.concatenate([col(4), col(5), kr_rep, zeros(LANES)], axis=1).astype(BF16)
    w['w_nsa'] = jnp.concatenate([col(7), col(8), col(9), col(10), col(11), zeros(LANES - 3 * NSA_HEADS)],
                                 axis=1).astype(BF16)
    w['w_br'] = col(12).astype(BF16)

    uq = raw['mla_w_uq'][l].reshape(MLA_Q_LORA, MLA_HEADS, MLA_NOPE + MLA_ROPE)
    w['w_uq'] = jnp.concatenate([uq[:, :, :MLA_NOPE].reshape(MLA_Q_LORA, -1),
                                 uq[:, :, MLA_NOPE:MLA_NOPE + MLA_HALF].reshape(MLA_Q_LORA, -1),
                                 uq[:, :, MLA_NOPE + MLA_HALF:].reshape(MLA_Q_LORA, -1)], axis=1).astype(BF16)
    uk = jnp.transpose(raw['mla_w_uk'][l], (0, 2, 1))
    zk = jnp.zeros_like(uk)
    even = (jnp.arange(MLA_HEADS) % 2 == 0)[:, None, None]
    w['w_uk'] = jnp.concatenate([jnp.where(even, uk, zk), jnp.where(even, zk, uk)], axis=1).astype(BF16)
    w['w_uv'] = raw['mla_w_uv'][l].astype(BF16)
    w['w_uv_all'] = jnp.transpose(raw['mla_w_uv'][l], (1, 0, 2)).reshape(MLA_KV_LORA, MLA_HEADS * MLA_V).astype(BF16)
    w['cq_g'] = raw['mla_cq_g'][l].reshape(1, -1)
    w['ckv_g'] = raw['mla_ckv_g'][l].reshape(1, -1)
    w['qn_g'] = jnp.tile(raw['mla_qn_g'][l], MLA_HEADS).reshape(1, -1)
    split_tile = lambda g: jnp.stack([jnp.tile(g[:MLA_HALF], MLA_HEADS), jnp.tile(g[MLA_HALF:], MLA_HEADS)])
    w['qr_g'] = split_tile(raw['mla_qr_g'][l])
    w['kr_g'] = split_tile(raw['mla_kr_g'][l])
    w['bd64_512'] = _block_diag_ones(NSA_QW, NSA_DH)
    w['bd64_128'] = _block_diag_ones(LANES, NSA_DH)
    w['bd16'] = _block_diag_ones(LANES, MLA_HALF)

    w['nsa_qn_g'] = jnp.tile(raw['nsa_qn_g'][l], NSA_HEADS).reshape(1, -1)
    w['nsa_kn_g'] = jnp.tile(raw['nsa_kn_g'][l], (1, NSA_GROUPS))
    wpos = raw['nsa_cmp_wpos'][l]
    wrow = jnp.concatenate([jnp.tile(wpos[0], (1, NSA_GROUPS)), jnp.tile(wpos[1], (1, NSA_GROUPS))], axis=1)
    w['cmp_w'] = jnp.tile(wrow, (512 // NSA_CMP_BLOCK, 1))
    proj = raw['nsa_cmp_proj'][l]
    pbd = jnp.zeros((NSA_KV_COLS, NSA_KV_COLS), F32)
    for kv in range(2):
        for g in range(NSA_GROUPS):
            o = (kv * NSA_GROUPS + g) * NSA_DH
            pbd = pbd.at[o:o + NSA_DH, o:o + NSA_DH].set(proj[kv])
    w['cmp_proj'] = pbd.astype(BF16)
    w['cmp_w_t'] = jnp.tile(wrow.T, (1, LANES // NSA_CMP_BLOCK))
    w['cmp_proj_t'] = pbd.T.astype(BF16)
    ge = np.zeros((3, LANES, NSA_QW), np.float32)
    for j in range(3):
        for h in range(NSA_HEADS):
            ge[j, 3 * h + j, h * NSA_DH:(h + 1) * NSA_DH] = 1.0
    w['gate_exp'] = jnp.asarray(ge, BF16)

    w['p_gdn'] = raw['p_gdn'][l].astype(BF16)
    w['p_mla'] = raw['p_mla'][l].astype(BF16)
    w['p_nsa'] = raw['p_nsa'][l].astype(BF16)
    w['w_out'] = raw['w_out'][l].astype(BF16)
    w['w_router'] = jnp.pad(jnp.concatenate([raw['moe_w_group'][l], raw['moe_w_expert'][l]], axis=1),
                            ((0, 0), (0, ROUTER_PAD - MOE_GROUPS - MOE_EXPERTS))).astype(BF16)
    w['w1'] = raw['moe_w1'][l].astype(BF16)
    w['w3'] = raw['moe_w3'][l].astype(BF16)
    w['w2'] = raw['moe_w2'][l].astype(BF16)
    return w


def _pool_matrix(n_in, n_out):
    pm = np.zeros((n_in, n_out), np.float32)
    ratio = NSA_SEL_BLOCK // NSA_CMP_BLOCK
    for c in range(n_in):
        if c // ratio < n_out:
            pm[c, c // ratio] = 1.0
    return jnp.asarray(pm, BF16)


def kernel(x_prompt, x_sample, cache_mla_ckv, cache_mla_krope, cache_nsa_cmp, cache_nsa_sel, cache_nsa_win, state_gdn_ssm, state_gdn_conv, page_table, norm1_g, w_in, gdn_conv_w, gdn_a_log, gdn_dt_bias, gdn_out_g, mla_cq_g, mla_ckv_g, mla_w_uq, mla_qn_g, mla_qr_g, mla_kr_g, mla_w_uk, mla_w_uv, nsa_qn_g, nsa_kn_g, nsa_cmp_wpos, nsa_cmp_proj, p_gdn, p_mla, p_nsa, w_out, norm2_g, moe_w_group, moe_w_expert, moe_w1, moe_w3, moe_w2):
    depth = w_in.shape[0]
    b_p, t_p, d = x_prompt.shape
    b_s, t_s, _ = x_sample.shape
    n_pool, page = cache_mla_ckv.shape[1:3]
    past = page_table.shape[1] * page
    win_s = cache_nsa_win.shape[2]
    assert t_p >= GDN_CONV - 1 and t_s >= GDN_CONV - 1 and t_p % GDN_CHUNK == 0
    tp_s = 8
    assert t_s <= tp_s
    raw = dict(mla_w_uq=mla_w_uq, mla_w_uk=mla_w_uk, mla_w_uv=mla_w_uv, mla_cq_g=mla_cq_g, mla_ckv_g=mla_ckv_g,
               mla_qn_g=mla_qn_g, mla_qr_g=mla_qr_g, mla_kr_g=mla_kr_g, nsa_qn_g=nsa_qn_g, nsa_kn_g=nsa_kn_g,
               nsa_cmp_wpos=nsa_cmp_wpos, nsa_cmp_proj=nsa_cmp_proj, p_gdn=p_gdn, p_mla=p_mla, p_nsa=p_nsa,
               w_out=w_out, moe_w_group=moe_w_group, moe_w_expert=moe_w_expert, moe_w1=moe_w1, moe_w3=moe_w3,
               moe_w2=moe_w2)
    pos_p = jnp.arange(t_p, dtype=jnp.int32)
    pos_s = jnp.tile(past + jnp.arange(t_s, dtype=jnp.int32), b_s)
    mla_tab_p, nsa_tab_p = _rope_tables(pos_p)
    mla_tab_s, nsa_tab_s = _rope_tables(pos_s)
    pool_p = _pool_matrix(LANES, LANES)
    nc_s = past // NSA_CMP_BLOCK
    pool_s = _pool_matrix(nc_s, nc_s)
    feat_major = lambda a: jnp.transpose(a, (0, 1, 3, 4, 5, 2)).reshape(a.shape[0], a.shape[1], NSA_KV_COLS, a.shape[2])
    cache_cmp_t = feat_major(cache_nsa_cmp)
    cache_sel_t = feat_major(cache_nsa_sel)
    cache_win_t = feat_major(cache_nsa_win)
    cache_kr_t = jnp.transpose(cache_mla_krope, (0, 1, 3, 2))

    def project(x2, l, w):
        g = norm1_g[l]
        return (norm_matmul(x2, g, w['w_gdn']), norm_matmul(x2, g, w['w_mla']),
                norm_matmul(x2, g, w['w_nsa']), norm_matmul(x2, g, w['w_br']))

    def finish(x2, o_a, o_b, o_c, br, l, w):
        y = merge(x2, o_a, o_b, o_c, br, w['p_gdn'], w['p_mla'], w['p_nsa'], w['w_out'])
        return hier_moe_residual(y, norm2_g[l], w['w_router'], w['w1'], w['w3'], w['w2'])

    def prompt_layer(x, l, w):
        m = b_p * t_p
        x2 = x.reshape(m, d)
        pg_, pm_, pn_, br = project(x2, l, w)
        pg3 = pg_.reshape(b_p, t_p, GDN_COLS)
        o_a, ssm = gdn_mixer(pg3, jnp.zeros((b_p, 8, GDN_CONV_DIM), F32),
                             jnp.zeros((b_p, GDN_HEADS, GDN_DK, GDN_DV), F32),
                             gdn_conv_w[l], gdn_a_log[l], gdn_dt_bias[l], gdn_out_g[l],
                             chunk=GDN_CHUNK, t_valid=GDN_CHUNK, nb=1, cb=4)
        conv = pg3[:, t_p - (GDN_CONV - 1):, :GDN_CONV_DIM]
        q_cat, k_cat, ckv, krope = mla_prep(pm_, *mla_tab_p, w, tm=256)
        o_b = mla_flash(q_cat, k_cat, w['w_uv'], b_p)
        q8, kv_c, kv_s, kv_w, gates, kvc = nsa_prep(pn_, nsa_tab_p, w, tm=256, compress=True)
        o_c = nsa_prompt(q8, gates, kvc.reshape(b_p, t_p // NSA_CMP_BLOCK, NSA_KV_COLS),
                         kv_s.reshape(b_p, t_p, NSA_KV_COLS), kv_w.reshape(b_p, t_p, NSA_KV_COLS),
                         pool_p, w['gate_exp'], b_p)
        y = finish(x2, o_a.reshape(m, GDN_VW), o_b, o_c, br, l, w).reshape(b_p, t_p, d)
        kv5 = lambda a: a.reshape(b_p, t_p, 2, NSA_GROUPS, NSA_DH)
        kvw5 = kv5(kv_w)
        return y, (ssm, conv, ckv.reshape(b_p, t_p, MLA_KV_LORA), krope.reshape(b_p, t_p, MLA_ROPE),
                   kv5(kv_c), kv5(kv_s), kvw5[:, t_p - min(NSA_WINDOW, t_p):])

    def sample_layer(x, l, w):
        m = b_s * t_s
        x2 = x.reshape(m, d)
        pg_, pm_, pn_, br = project(x2, l, w)
        pad_t = lambda a: jnp.pad(a.reshape(b_s, t_s, -1), ((0, 0), (0, tp_s - t_s), (0, 0)))
        pg3 = pg_.reshape(b_s, t_s, GDN_COLS)
        hist8 = jnp.pad(state_gdn_conv[l], ((0, 0), (8 - (GDN_CONV - 1), 0), (0, 0)))
        o_a, ssm = gdn_mixer(pad_t(pg_), hist8, state_gdn_ssm[l], gdn_conv_w[l], gdn_a_log[l], gdn_dt_bias[l],
                             gdn_out_g[l], chunk=tp_s, t_valid=t_s, nb=4, cb=1)
        o_a = o_a[:, :t_s].reshape(m, GDN_VW)
        conv = pg3[:, t_s - (GDN_CONV - 1):, :GDN_CONV_DIM]

        q_cat, k_cat, ckv, krope = mla_prep(pm_, *mla_tab_s, w, tm=m)
        qc = q_cat.reshape(m, MLA_HEADS, MLA_QK)
        q_lat = qc[:, :, :MLA_KV_LORA]
        hsel = jnp.arange(MLA_HEADS)
        r1 = qc[:, :, MLA_KV_LORA:MLA_KV_LORA + LANES].reshape(m, MLA_HEADS, MLA_HEADS, MLA_HALF)[:, hsel, hsel]
        r2 = qc[:, :, MLA_KV_LORA + LANES:].reshape(m, MLA_HEADS, MLA_HEADS, MLA_HALF)[:, hsel, hsel]
        q_rope = jnp.concatenate([r1, r2], axis=-1)
        rows_of = lambda a: pad_t(a.reshape(m, -1)).reshape(b_s, tp_s * MLA_HEADS, -1)
        o_b = mla_decode(page_table, rows_of(q_lat), rows_of(q_rope), pad_t(ckv).astype(BF16),
                         pad_t(krope).astype(BF16), w['w_uv_all'], cache_mla_ckv, cache_kr_t, l, t_s=t_s)
        o_b = o_b[:, :t_s].reshape(m, MLA_HEADS * MLA_V)

        q8, kv_c, kv_s, kv_w, gates = nsa_prep(pn_, nsa_tab_s, w, tm=m, compress=False)
        ws = dict(w, pool_s=pool_s)
        o_c = nsa_decode(page_table, pad_t(q8), pad_t(gates), pad_t(kv_s), pad_t(kv_w), cache_cmp_t, cache_sel_t,
                         cache_win_t, ws, l, t_s=t_s)
        o_c = o_c[:, :t_s].reshape(m, NSA_QW)
        y = finish(x2, o_a, o_b, o_c, br, l, w).reshape(b_s, t_s, d)
        kv5 = lambda a: a.reshape(b_s, t_s, 2, NSA_GROUPS, NSA_DH)
        win_new = jnp.concatenate([cache_nsa_win[l], kv5(kv_w)], axis=1)[:, t_s:]
        return y, (ssm, conv, ckv.reshape(b_s, t_s, MLA_KV_LORA), krope.reshape(b_s, t_s, MLA_ROPE),
                   kv5(kv_c), kv5(kv_s), win_new)

    xp, xs = x_prompt, x_sample
    p_states, s_states = [], []
    for l in range(depth):
        w = _layer_weights(l, w_in, raw)
        xp, st = prompt_layer(xp, l, w)
        p_states.append(st)
        xs, st = sample_layer(xs, l, w)
        s_states.append(st)
    p_out = [jnp.stack([st[i] for st in p_states], axis=0) for i in range(7)]
    s_out = [jnp.stack([st[i] for st in s_states], axis=0) for i in range(7)]
    return (xp, xs, *p_out, *s_out)
```

```python
import functools
import math

import numpy as np
import jax
import jax.numpy as jnp
from jax import lax
from jax.experimental import pallas as pl
from jax.experimental.pallas import tpu as pltpu

F32 = jnp.float32
BF16 = jnp.bfloat16
HIGHEST = lax.Precision.HIGHEST

D_MODEL = 1024
ROPE_THETA = 500000.0
NORM_EPS = 1e-6
NEG = -1e30
LANES = 128

GDN_HEADS = 4
GDN_DK = 128
GDN_DV = 128
GDN_CONV = 4
GDN_CHUNK = 64
GDN_QK = GDN_HEADS * GDN_DK
GDN_VW = GDN_HEADS * GDN_DV
GDN_CONV_DIM = 2 * GDN_QK + GDN_VW

MLA_HEADS = 8
MLA_Q_LORA = 384
MLA_KV_LORA = 256
MLA_NOPE = 64
MLA_ROPE = 32
MLA_HALF = MLA_ROPE // 2
MLA_V = 64
MLA_SCALE = (MLA_NOPE + MLA_ROPE) ** -0.5
MLA_QK = MLA_KV_LORA + 2 * LANES
FLASH_HEADS_PER_BLOCK = 2
DECODE_PAGE_GROUPS = 4

NSA_HEADS = 8
NSA_GROUPS = 2
NSA_HPG = NSA_HEADS // NSA_GROUPS
NSA_DH = 64
NSA_ROT = NSA_DH // 4
NSA_CMP_BLOCK = 32
NSA_SEL_BLOCK = 64
NSA_TOPK = 16
NSA_WINDOW = 512
NSA_FORCE = 1000.0
NSA_SCALE = NSA_DH ** -0.5
NSA_KV_COLS = 2 * NSA_GROUPS * NSA_DH
NSA_QW = NSA_HEADS * NSA_DH

MOE_GROUPS = 4
MOE_EPG = 8
MOE_EXPERTS = MOE_GROUPS * MOE_EPG
MOE_TOPK = 2
MOE_DFF = 512
MOE_BLOCK = 128
ROUTER_PAD = LANES

IN_SPLITS = (GDN_CONV_DIM, GDN_VW, GDN_HEADS, GDN_HEADS,
             MLA_Q_LORA, MLA_KV_LORA, MLA_ROPE,
             NSA_HEADS * NSA_DH, NSA_KV_COLS, NSA_KV_COLS, NSA_KV_COLS, 3 * NSA_HEADS,
             3 * D_MODEL)
IN_OFF = [0]
for _w in IN_SPLITS:
    IN_OFF.append(IN_OFF[-1] + _w)

GDN_COLS = GDN_CONV_DIM + GDN_VW + LANES
MLA_COLS = MLA_Q_LORA + MLA_KV_LORA + 2 * LANES + LANES
NSA_COLS = NSA_QW + 3 * NSA_KV_COLS + LANES

VMEM_LIMIT = 52 * 1024 * 1024

_NT = (((1,), (1,)), ((), ()))
_TN = (((0,), (0,)), ((), ()))


def _cparams(sem):
    return pltpu.CompilerParams(dimension_semantics=sem, vmem_limit_bytes=VMEM_LIMIT)


def _block_diag_ones(n, grp):
    i = np.arange(n)
    return jnp.asarray((i[:, None] // grp) == (i[None, :] // grp), BF16)


def _group_sum(sq, bd):
    hi = sq.astype(BF16)
    lo = (sq - hi.astype(F32)).astype(BF16)
    return jnp.dot(hi, bd, preferred_element_type=F32) + jnp.dot(lo, bd, preferred_element_type=F32)


def _softmax_rows(s, mask):
    s = jnp.where(mask, s, NEG)
    m = jnp.max(s, axis=-1, keepdims=True)
    p = jnp.where(mask, jnp.exp(s - m), 0.0)
    return p / jnp.maximum(jnp.sum(p, axis=-1, keepdims=True), 1e-30)


def _silu(x):
    return x * jax.nn.sigmoid(x)


def _norm_matmul_kernel(x_ref, g_ref, w_ref, o_ref, h_ref):
    @pl.when(pl.program_id(1) == 0)
    def _():
        x = x_ref[...]
        ms = jnp.mean(x * x, axis=-1, keepdims=True)
        h_ref[...] = (x * lax.rsqrt(ms + NORM_EPS) * g_ref[...]).astype(BF16)

    o_ref[...] = jnp.dot(h_ref[...], w_ref[...], preferred_element_type=F32)


def norm_matmul(x, g, w, *, tm=512, tn=None):
    m, k = x.shape
    n = w.shape[1]
    tm = min(tm, m)
    tn = n if tn is None else tn
    assert m % tm == 0 and n % tn == 0
    return pl.pallas_call(
        _norm_matmul_kernel,
        grid=(m // tm, n // tn),
        in_specs=[pl.BlockSpec((tm, k), lambda i, j: (i, 0)),
                  pl.BlockSpec((1, k), lambda i, j: (0, 0)),
                  pl.BlockSpec((k, tn), lambda i, j: (0, j))],
        out_specs=pl.BlockSpec((tm, tn), lambda i, j: (i, j)),
        out_shape=jax.ShapeDtypeStruct((m, n), F32),
        scratch_shapes=[pltpu.VMEM((tm, k), BF16)],
        compiler_params=_cparams(("parallel", "arbitrary")),
        name="norm_matmul",
    )(x, g.reshape(1, k), w)


def _gdn_kernel(p_ref, hist_ref, s0_ref, cw_ref, alog_ref, dtb_ref, og_ref, o_ref, sfin_ref,
                prev_ref, s_ref, *, chunk, t_valid, nb, cb):
    c = pl.program_id(1)
    n_steps = pl.num_programs(1)
    qk = GDN_QK
    span = cb * chunk
    dot32 = functools.partial(jnp.dot, preferred_element_type=F32, precision=HIGHEST)
    dotb = lambda a, b: jnp.dot(a.astype(BF16), b.astype(BF16), preferred_element_type=F32)

    @pl.when(c == 0)
    def _():
        prev_ref[...] = hist_ref[...]
        s_ref[...] = s0_ref[...]

    row = lax.broadcasted_iota(jnp.int32, (chunk, chunk), 0)
    col = lax.broadcasted_iota(jnp.int32, (chunk, chunk), 1)
    incl = row >= col
    strict = row > col
    eye_f = (row == col).astype(F32)
    incl_f = incl.astype(F32)
    live = (lax.broadcasted_iota(jnp.int32, (chunk, 1), 0) < t_valid) if t_valid < chunk else None
    n_fac = int(math.log2(chunk)) - 1

    chains = []
    raws, qkvs, betas, gcs, gcts = [], [], {}, {}, {}
    for s in range(nb):
        raw = p_ref[s]
        xcat = jnp.concatenate([prev_ref[s], raw[:, :GDN_CONV_DIM]], axis=0)
        prev_ref[s] = xcat[span:span + 8]
        acc = xcat[8:8 + span] * cw_ref[GDN_CONV - 1:GDN_CONV, :]
        for i in range(GDN_CONV - 1):
            sh = pltpu.roll(xcat, GDN_CONV - 1 - i, 0)
            acc = acc + sh[8:8 + span] * cw_ref[i:i + 1, :]
        raws.append(raw)
        qkvs.append(_silu(acc))
        ba = raw[:, GDN_CONV_DIM + GDN_VW:]
        xa = ba + dtb_ref[...]
        softplus = jnp.maximum(xa, 0.0) + jnp.log(1.0 + jnp.exp(-jnp.abs(xa)))
        beta_all = jax.nn.sigmoid(ba)
        g_all = -jnp.exp(alog_ref[...]) * softplus
        for ci in range(cb):
            bt = beta_all[ci * chunk:(ci + 1) * chunk]
            gt = g_all[ci * chunk:(ci + 1) * chunk]
            if live is not None:
                bt = jnp.where(live, bt, 0.0)
                gt = jnp.where(live, gt, 0.0)
            betas[(s, ci)] = bt
            gcs[(s, ci)] = dot32(incl_f, gt)
            gcts[(s, ci)] = lax.dot_general(gt, incl_f, (((0,), (1,)), ((), ())), preferred_element_type=F32,
                                            precision=HIGHEST)
    for ci in range(cb):
        for s in range(nb):
            for h in range(GDN_HEADS):
                chains.append((s, ci, h))
    nch = len(chains)

    def tile(s, ci, off):
        return qkvs[s][ci * chunk:(ci + 1) * chunk, off:off + GDN_DK]

    qs = [tile(s, ci, h * GDN_DK) for s, ci, h in chains]
    ks = [tile(s, ci, qk + h * GDN_DK) for s, ci, h in chains]
    vs = [tile(s, ci, 2 * qk + h * GDN_DV) for s, ci, h in chains]
    qs = [q * lax.rsqrt(jnp.sum(q * q, axis=-1, keepdims=True) + NORM_EPS) * (GDN_DK ** -0.5) for q in qs]
    ks = [k * lax.rsqrt(jnp.sum(k * k, axis=-1, keepdims=True) + NORM_EPS) for k in ks]
    beta = [betas[(s, ci)][:, h:h + 1] for s, ci, h in chains]
    gcol = [gcs[(s, ci)][:, GDN_HEADS + h:GDN_HEADS + h + 1] for s, ci, h in chains]
    grow = [gcts[(s, ci)][GDN_HEADS + h:GDN_HEADS + h + 1, :] for s, ci, h in chains]
    decay = [jnp.where(incl, jnp.exp(jnp.where(incl, gcol[i] - grow[i], 0.0)), 0.0) for i in range(nch)]
    kb = [ks[i] * beta[i] for i in range(nch)]
    kbf = [k.astype(BF16) for k in ks]
    a_kk = [jnp.where(strict, lax.dot_general(kb[i].astype(BF16), kbf[i], _NT, preferred_element_type=F32) * decay[i],
                      0.0) for i in range(nch)]
    inv = [eye_f - a for a in a_kk]
    pw = a_kk
    for _ in range(n_fac):
        pw = [dot32(p, p) for p in pw]
        inv = [inv[i] + dot32(inv[i], pw[i]) for i in range(nch)]
    egc = [jnp.exp(g) for g in gcol]
    sol = [dot32(inv[i], jnp.concatenate([vs[i] * beta[i], kb[i] * egc[i]], axis=1)) for i in range(nch)]
    a_qk = [lax.dot_general(qs[i].astype(BF16), kbf[i], _NT, preferred_element_type=F32) * decay[i]
            for i in range(nch)]
    qe = [(qs[i] * egc[i]).astype(BF16) for i in range(nch)]
    g_last = [g[chunk - 1:chunk, :] for g in gcol]
    kdec = [(ks[i] * jnp.exp(g_last[i] - gcol[i])).astype(BF16) for i in range(nch)]
    e_last = [jnp.exp(g) for g in g_last]

    per = nb * GDN_HEADS
    state = {(s, h): s_ref[s, h] for s in range(nb) for h in range(GDN_HEADS)}
    for ci in range(cb):
        idx = range(ci * per, (ci + 1) * per)
        sb = {i: state[(chains[i][0], chains[i][2])].astype(BF16) for i in idx}
        v_new = {i: sol[i][:, :GDN_DV] - jnp.dot(sol[i][:, GDN_DV:].astype(BF16), sb[i], preferred_element_type=F32)
                 for i in idx}
        o = {i: jnp.dot(qe[i], sb[i], preferred_element_type=F32) + dotb(a_qk[i], v_new[i]) for i in idx}
        for i in idx:
            s, _, h = chains[i]
            state[(s, h)] = state[(s, h)] * e_last[i] + lax.dot_general(kdec[i], v_new[i].astype(BF16), _TN,
                                                                        preferred_element_type=F32)
            on = o[i] * lax.rsqrt(jnp.mean(o[i] * o[i], axis=-1, keepdims=True) + NORM_EPS) * og_ref[...]
            z = raws[s][ci * chunk:(ci + 1) * chunk, GDN_CONV_DIM + h * GDN_DV:GDN_CONV_DIM + (h + 1) * GDN_DV]
            o_ref[s, ci * chunk:(ci + 1) * chunk, h * GDN_DV:(h + 1) * GDN_DV] = on * _silu(z)
    for (s, h), val in state.items():
        s_ref[s, h] = val

    @pl.when(c == n_steps - 1)
    def _():
        sfin_ref[...] = s_ref[...]


def gdn_mixer(p_gdn, hist8, s0, conv_w, a_log, dt_bias, out_g, *, chunk, t_valid, nb, cb):
    n, tp, _ = p_gdn.shape
    span = cb * chunk
    assert n % nb == 0 and tp % span == 0
    pad4 = jnp.zeros((GDN_HEADS,), F32)
    lane_row = lambda a: jnp.pad(jnp.concatenate([pad4, a]), (0, LANES - 2 * GDN_HEADS)).reshape(1, LANES)
    return pl.pallas_call(
        functools.partial(_gdn_kernel, chunk=chunk, t_valid=t_valid, nb=nb, cb=cb),
        grid=(n // nb, tp // span),
        in_specs=[pl.BlockSpec((nb, span, GDN_COLS), lambda b, c: (b, c, 0)),
                  pl.BlockSpec((nb, 8, GDN_CONV_DIM), lambda b, c: (b, 0, 0)),
                  pl.BlockSpec((nb, GDN_HEADS, GDN_DK, GDN_DV), lambda b, c: (b, 0, 0, 0)),
                  pl.BlockSpec((GDN_CONV, GDN_CONV_DIM), lambda b, c: (0, 0)),
                  pl.BlockSpec((1, LANES), lambda b, c: (0, 0)),
                  pl.BlockSpec((1, LANES), lambda b, c: (0, 0)),
                  pl.BlockSpec((1, GDN_DV), lambda b, c: (0, 0))],
        out_specs=[pl.BlockSpec((nb, span, GDN_VW), lambda b, c: (b, c, 0)),
                   pl.BlockSpec((nb, GDN_HEADS, GDN_DK, GDN_DV), lambda b, c: (b, 0, 0, 0))],
        out_shape=[jax.ShapeDtypeStruct((n, tp, GDN_VW), F32),
                   jax.ShapeDtypeStruct((n, GDN_HEADS, GDN_DK, GDN_DV), F32)],
        scratch_shapes=[pltpu.VMEM((nb, 8, GDN_CONV_DIM), F32), pltpu.VMEM((nb, GDN_HEADS, GDN_DK, GDN_DV), F32)],
        compiler_params=_cparams(("parallel", "arbitrary")),
        name="gdn_mixer",
    )(p_gdn, hist8, s0, conv_w, lane_row(a_log), lane_row(dt_bias), out_g.reshape(1, GDN_DV))


def _mla_prep_kernel(p_ref, cos_ref, sin_ref, cqg_ref, ckvg_ref, wuq_ref, qng_ref, qrg_ref, krg_ref, wuk_ref,
                     bd64_ref, bd16_ref, qcat_ref, kcat_ref, ckv_ref, krope_ref):
    p = p_ref[...]
    cos = cos_ref[...]
    sin = sin_ref[...]
    lane = lax.broadcasted_iota(jnp.int32, (1, LANES), 1)

    cq = p[:, :MLA_Q_LORA]
    cq = cq * lax.rsqrt(jnp.mean(cq * cq, axis=-1, keepdims=True) + NORM_EPS) * cqg_ref[...]
    q = jnp.dot(cq.astype(BF16), wuq_ref[...], preferred_element_type=F32)
    nw = MLA_HEADS * MLA_NOPE
    nope = q[:, :nw]
    nope = nope * lax.rsqrt(_group_sum(nope * nope, bd64_ref[...]) * (1.0 / MLA_NOPE) + NORM_EPS) * qng_ref[...]
    x1 = q[:, nw:nw + LANES]
    x2 = q[:, nw + LANES:]
    r = lax.rsqrt(_group_sum(x1 * x1 + x2 * x2, bd16_ref[...]) * (1.0 / MLA_ROPE) + NORM_EPS)
    x1 = x1 * r * qrg_ref[0:1, :]
    x2 = x2 * r * qrg_ref[1:2, :]
    r1 = x1 * cos - x2 * sin
    r2 = x2 * cos + x1 * sin
    for h in range(MLA_HEADS):
        tile = nope[:, (h // 2) * LANES:(h // 2 + 1) * LANES].astype(BF16)
        q_lat = jnp.dot(tile, wuk_ref[h], preferred_element_type=F32)
        mine = (lane // MLA_HALF) == h
        base = h * MLA_QK
        qcat_ref[:, base:base + MLA_KV_LORA] = q_lat.astype(BF16)
        qcat_ref[:, base + MLA_KV_LORA:base + MLA_KV_LORA + LANES] = jnp.where(mine, r1, 0.0).astype(BF16)
        qcat_ref[:, base + MLA_KV_LORA + LANES:base + MLA_QK] = jnp.where(mine, r2, 0.0).astype(BF16)

    ckv = p[:, MLA_Q_LORA:MLA_Q_LORA + MLA_KV_LORA]
    ckv = ckv * lax.rsqrt(jnp.mean(ckv * ckv, axis=-1, keepdims=True) + NORM_EPS) * ckvg_ref[...]
    ckv_ref[...] = ckv
    kr = p[:, MLA_Q_LORA + MLA_KV_LORA:MLA_Q_LORA + MLA_KV_LORA + 2 * LANES]
    ms = jnp.sum(kr * kr, axis=-1, keepdims=True) * (1.0 / (MLA_HEADS * MLA_ROPE))
    rk = lax.rsqrt(ms + NORM_EPS)
    k1 = kr[:, :LANES] * rk * krg_ref[0:1, :]
    k2 = kr[:, LANES:] * rk * krg_ref[1:2, :]
    k1r = k1 * cos - k2 * sin
    k2r = k2 * cos + k1 * sin
    kcat_ref[:, :MLA_KV_LORA] = ckv.astype(BF16)
    kcat_ref[:, MLA_KV_LORA:MLA_KV_LORA + LANES] = k1r.astype(BF16)
    kcat_ref[:, MLA_KV_LORA + LANES:] = k2r.astype(BF16)
    std = jnp.where(lane < MLA_HALF, k1r, pltpu.roll(k2r, MLA_HALF, 1))
    krope_ref[...] = std[:, :MLA_ROPE]


def mla_prep(p_mla, cos, sin, wts, *, tm):
    m = p_mla.shape[0]
    nt = cos.shape[0] // tm
    row = lambda w: pl.BlockSpec((tm, w), lambda i: (i, 0))
    res = lambda a: pl.BlockSpec(a.shape, lambda i: (0,) * a.ndim)
    tab = pl.BlockSpec((tm, LANES), lambda i: (i % nt, 0))
    consts = [wts['cq_g'], wts['ckv_g'], wts['w_uq'], wts['qn_g'], wts['qr_g'], wts['kr_g'], wts['w_uk'],
              wts['bd64_512'], wts['bd16']]
    return pl.pallas_call(
        _mla_prep_kernel,
        grid=(m // tm,),
        in_specs=[row(MLA_COLS), tab, tab] + [res(a) for a in consts],
        out_specs=[row(MLA_HEADS * MLA_QK), row(MLA_QK), row(MLA_KV_LORA), row(MLA_ROPE)],
        out_shape=[jax.ShapeDtypeStruct((m, MLA_HEADS * MLA_QK), BF16), jax.ShapeDtypeStruct((m, MLA_QK), BF16),
                   jax.ShapeDtypeStruct((m, MLA_KV_LORA), F32), jax.ShapeDtypeStruct((m, MLA_ROPE), F32)],
        compiler_params=_cparams(("parallel",)),
        name="mla_prep",
    )(p_mla, cos, sin, *consts)


def _mla_flash_kernel(qi_ref, kj_ref, q_ref, k_ref, wuv_ref, o_ref, m_ref, l_ref, acc_ref, *, tq, tk):
    p = pl.program_id(1)
    qi = qi_ref[p]
    kj = kj_ref[p]
    rows = MLA_HEADS * tq

    @pl.when(kj == 0)
    def _():
        m_ref[...] = jnp.full(m_ref.shape, NEG, F32)
        l_ref[...] = jnp.zeros(l_ref.shape, F32)
        acc_ref[...] = jnp.zeros(acc_ref.shape, F32)

    k = k_ref[...]
    kv = k[:, :MLA_KV_LORA]
    kpos = kj * tk + lax.broadcasted_iota(jnp.int32, (1, tk), 1)
    for h0 in range(0, MLA_HEADS, FLASH_HEADS_PER_BLOCK):
        sub = FLASH_HEADS_PER_BLOCK * tq
        rs = slice(h0 * tq, h0 * tq + sub)
        q = jnp.concatenate([q_ref[:, h * MLA_QK:(h + 1) * MLA_QK] for h in range(h0, h0 + FLASH_HEADS_PER_BLOCK)],
                            axis=0)
        s = lax.dot_general(q, k, _NT, preferred_element_type=F32) * MLA_SCALE
        qpos = qi * tq + (lax.broadcasted_iota(jnp.int32, (sub, 1), 0) & (tq - 1))
        s = jnp.where(kpos <= qpos, s, NEG)
        m_prev = m_ref[rs]
        m_new = jnp.maximum(m_prev, jnp.max(s, axis=1, keepdims=True))
        alpha = jnp.exp(m_prev - m_new)
        pexp = jnp.exp(s - m_new)
        l_ref[rs] = alpha * l_ref[rs] + jnp.sum(pexp, axis=1, keepdims=True)
        acc_ref[rs] = alpha * acc_ref[rs] + jnp.dot(pexp.astype(BF16), kv, preferred_element_type=F32)
        m_ref[rs] = m_new

    @pl.when(kj == (qi * tq + tq - 1) // tk)
    def _():
        lat = (acc_ref[...] / l_ref[...]).astype(BF16)
        outs = [jnp.dot(lat[h * tq:(h + 1) * tq], wuv_ref[h], preferred_element_type=F32)
                for h in range(MLA_HEADS)]
        o_ref[...] = jnp.concatenate(outs, axis=-1)


def mla_flash(q_cat, k_cat, w_uv, n, *, tq=128, tk=1024):
    t = q_cat.shape[0] // n
    assert t % tq == 0 and t % tk == 0 and (tq & (tq - 1)) == 0
    pairs = [(i, j) for i in range(t // tq) for j in range((i * tq + tq - 1) // tk + 1)]
    qi_tab = jnp.asarray([a for a, _ in pairs], jnp.int32)
    kj_tab = jnp.asarray([b for _, b in pairs], jnp.int32)
    rows = MLA_HEADS * tq
    nq, nk = t // tq, t // tk
    grid_spec = pltpu.PrefetchScalarGridSpec(
        num_scalar_prefetch=2,
        grid=(n, len(pairs)),
        in_specs=[pl.BlockSpec((tq, MLA_HEADS * MLA_QK), lambda b, p, qi, kj: (b * nq + qi[p], 0)),
                  pl.BlockSpec((tk, MLA_QK), lambda b, p, qi, kj: (b * nk + kj[p], 0)),
                  pl.BlockSpec((MLA_HEADS, MLA_KV_LORA, MLA_V), lambda b, p, qi, kj: (0, 0, 0))],
        out_specs=pl.BlockSpec((tq, MLA_HEADS * MLA_V), lambda b, p, qi, kj: (b * nq + qi[p], 0)),
        scratch_shapes=[pltpu.VMEM((rows, 1), F32), pltpu.VMEM((rows, 1), F32),
                        pltpu.VMEM((rows, MLA_KV_LORA), F32)])
    return pl.pallas_call(
        functools.partial(_mla_flash_kernel, tq=tq, tk=tk),
        grid_spec=grid_spec,
        out_shape=jax.ShapeDtypeStruct((n * t, MLA_HEADS * MLA_V), F32),
        compiler_params=_cparams(("parallel", "arbitrary")),
        name="mla_flash",
    )(qi_tab, kj_tab, q_cat, k_cat, w_uv)


def _mla_decode_kernel(pt_ref, ql_ref, qr_ref, cnew_ref, rnew_ref, wuv_ref, *refs, pg, t_s):
    ck_refs = refs[:pg]
    kr_refs = refs[pg:2 * pg]
    o_ref, m_ref, l_ref, acc_ref = refs[2 * pg:]
    j = pl.program_id(1)
    rows = ql_ref.shape[1]
    tp = rows // MLA_HEADS

    @pl.when(j == 0)
    def _():
        m_ref[...] = jnp.full(m_ref.shape, NEG, F32)
        l_ref[...] = jnp.zeros(l_ref.shape, F32)
        acc_ref[...] = jnp.zeros(acc_ref.shape, F32)

    ql = ql_ref[0]
    qr = qr_ref[0]
    cks, ss = [], []
    for i in range(pg):
        ck = ck_refs[i][0, 0].astype(BF16)
        kr_t = kr_refs[i][0, 0].astype(BF16)
        cks.append(ck)
        ss.append((lax.dot_general(ql, ck, _NT, preferred_element_type=F32)
                   + jnp.dot(qr, kr_t, preferred_element_type=F32)) * MLA_SCALE)
    gsz = pg // DECODE_PAGE_GROUPS
    parts = []
    for g in range(DECODE_PAGE_GROUPS):
        sg = jnp.concatenate(ss[g * gsz:(g + 1) * gsz], axis=1)
        mg = jnp.max(sg, axis=1, keepdims=True)
        pg_exp = jnp.exp(sg - mg)
        page = sg.shape[1] // gsz
        pv = None
        for i in range(gsz):
            term = jnp.dot(pg_exp[:, i * page:(i + 1) * page].astype(BF16), cks[g * gsz + i],
                           preferred_element_type=F32)
            pv = term if pv is None else pv + term
        parts.append((mg, jnp.sum(pg_exp, axis=1, keepdims=True), pv))
    m_prev = m_ref[...]
    m_new = m_prev
    for mg, _, _ in parts:
        m_new = jnp.maximum(m_new, mg)
    alpha = jnp.exp(m_prev - m_new)
    l_new = alpha * l_ref[...]
    acc_new = alpha * acc_ref[...]
    for mg, lg, pv in parts:
        w = jnp.exp(mg - m_new)
        l_new = l_new + w * lg
        acc_new = acc_new + w * pv
    l_ref[...] = l_new
    acc_ref[...] = acc_new
    m_ref[...] = m_new

    @pl.when(j == pl.num_programs(1) - 1)
    def _():
        cn = cnew_ref[0]
        rn = rnew_ref[0]
        sn = (lax.dot_general(ql, cn, _NT, preferred_element_type=F32)
              + lax.dot_general(qr, rn, _NT, preferred_element_type=F32)) * MLA_SCALE
        tq = lax.broadcasted_iota(jnp.int32, (rows, 1), 0) // MLA_HEADS
        tk = lax.broadcasted_iota(jnp.int32, (1, tp), 1)
        sn = jnp.where((tk <= tq) & (tk < t_s), sn, NEG)
        m2 = jnp.maximum(m_ref[...], jnp.max(sn, axis=1, keepdims=True))
        a2 = jnp.exp(m_ref[...] - m2)
        pn = jnp.exp(sn - m2)
        l2 = a2 * l_ref[...] + jnp.sum(pn, axis=1, keepdims=True)
        acc2 = a2 * acc_ref[...] + jnp.dot(pn.astype(BF16), cn, preferred_element_type=F32)
        lat = (acc2 / l2).astype(BF16)
        full = jnp.dot(lat, wuv_ref[...], preferred_element_type=F32)
        hrow = lax.broadcasted_iota(jnp.int32, (rows, 1), 0) % MLA_HEADS
        hcol = lax.broadcasted_iota(jnp.int32, (1, MLA_HEADS * MLA_V), 1) // MLA_V
        full = jnp.where(hrow == hcol, full, 0.0)
        o_ref[0] = jnp.sum(full.reshape(tp, MLA_HEADS, MLA_HEADS * MLA_V), axis=1)


def mla_decode(page_table, q_lat, q_rope, ckv_new, kr_new, w_uv_all, cache_ckv, cache_kr_t, l, *, t_s, pg=16):
    b, rows, _ = q_lat.shape
    tp = rows // MLA_HEADS
    n_pages = page_table.shape[1]
    page = cache_ckv.shape[2]
    assert n_pages % pg == 0
    pt = page_table.reshape(-1)

    def page_spec(shape, i):
        return pl.BlockSpec((1, 1) + shape, lambda s, j, pt_ref: (l, pt_ref[s * n_pages + j * pg + i], 0, 0))

    seq = lambda a: pl.BlockSpec((1,) + a.shape[1:], lambda s, j, pt_ref: (s, 0, 0))
    grid_spec = pltpu.PrefetchScalarGridSpec(
        num_scalar_prefetch=1,
        grid=(b, n_pages // pg),
        in_specs=[seq(q_lat), seq(q_rope), seq(ckv_new), seq(kr_new),
                  pl.BlockSpec(w_uv_all.shape, lambda s, j, pt_ref: (0, 0))]
                 + [page_spec((page, MLA_KV_LORA), i) for i in range(pg)]
                 + [page_spec((MLA_ROPE, page), i) for i in range(pg)],
        out_specs=pl.BlockSpec((1, tp, MLA_HEADS * MLA_V), lambda s, j, pt_ref: (s, 0, 0)),
        scratch_shapes=[pltpu.VMEM((rows, 1), F32), pltpu.VMEM((rows, 1), F32),
                        pltpu.VMEM((rows, MLA_KV_LORA), F32)])
    return pl.pallas_call(
        functools.partial(_mla_decode_kernel, pg=pg, t_s=t_s),
        grid_spec=grid_spec,
        out_shape=jax.ShapeDtypeStruct((b, tp, MLA_HEADS * MLA_V), F32),
        compiler_params=_cparams(("parallel", "arbitrary")),
        name="mla_decode",
    )(pt, q_lat, q_rope, ckv_new, kr_new, w_uv_all, *([cache_ckv] * pg), *([cache_kr_t] * pg))


def _rope_lanes(x, c, s1, s2):
    n = x.shape[1]
    half = NSA_ROT // 2
    reps = n // LANES
    tile = (lambda a: jnp.concatenate([a] * reps, axis=1)) if reps > 1 else (lambda a: a)
    return x * tile(c) + pltpu.roll(x, n - half, 1) * tile(s1) + pltpu.roll(x, half, 1) * tile(s2)


def _nsa_prep_kernel(p_ref, c_ref, s1_ref, s2_ref, qng_ref, kng_ref, bdq_ref, bdk_ref, wt_ref, pbd_ref,
                     q_ref, kvc_ref, kvs_ref, kvw_ref, gate_ref, *cmp_out, compress):
    p = p_ref[...]
    c, s1, s2 = c_ref[...], s1_ref[...], s2_ref[...]
    q = p[:, :NSA_QW]
    q = q * lax.rsqrt(_group_sum(q * q, bdq_ref[...]) * (1.0 / NSA_DH) + NORM_EPS) * qng_ref[...]
    q_ref[...] = (_rope_lanes(q, c, s1, s2) * NSA_SCALE).astype(BF16)
    outs = (kvc_ref, kvs_ref, kvw_ref)
    kc = None
    for j in range(3):
        base = NSA_QW + j * NSA_KV_COLS
        k = p[:, base:base + LANES]
        k = k * lax.rsqrt(_group_sum(k * k, bdk_ref[...]) * (1.0 / NSA_DH) + NORM_EPS) * kng_ref[j:j + 1, :]
        k = _rope_lanes(k, c, s1, s2)
        outs[j][:, :LANES] = k
        outs[j][:, LANES:] = p[:, base + LANES:base + NSA_KV_COLS]
        if j == 0:
            kc = k
    gate_ref[...] = jax.nn.sigmoid(p[:, NSA_QW + 3 * NSA_KV_COLS:])
    if compress:
        tm = p.shape[0]
        rows = jnp.concatenate([kc, p[:, NSA_QW + LANES:NSA_QW + NSA_KV_COLS]], axis=1)
        pooled = jnp.sum((rows * wt_ref[...]).reshape(tm // NSA_CMP_BLOCK, NSA_CMP_BLOCK, NSA_KV_COLS), axis=1)
        cmp_out[0][...] = jnp.dot(pooled.astype(BF16), pbd_ref[...], preferred_element_type=F32)


def nsa_prep(p_nsa, tabs, wts, *, tm, compress):
    m = p_nsa.shape[0]
    nt = tabs[0].shape[0] // tm
    row = lambda w: pl.BlockSpec((tm, w), lambda i: (i, 0))
    res = lambda a: pl.BlockSpec(a.shape, lambda i: (0,) * a.ndim)
    tab = pl.BlockSpec((tm, LANES), lambda i: (i % nt, 0))
    consts = [wts['nsa_qn_g'], wts['nsa_kn_g'], wts['bd64_512'], wts['bd64_128'], wts['cmp_w'][:tm], wts['cmp_proj']]
    out_specs = [row(NSA_QW), row(NSA_KV_COLS), row(NSA_KV_COLS), row(NSA_KV_COLS), row(LANES)]
    out_shape = [jax.ShapeDtypeStruct((m, NSA_QW), BF16)] + [jax.ShapeDtypeStruct((m, NSA_KV_COLS), F32)] * 3 \
        + [jax.ShapeDtypeStruct((m, LANES), F32)]
    if compress:
        out_specs.append(pl.BlockSpec((tm // NSA_CMP_BLOCK, NSA_KV_COLS), lambda i: (i, 0)))
        out_shape.append(jax.ShapeDtypeStruct((m // NSA_CMP_BLOCK, NSA_KV_COLS), F32))
    return pl.pallas_call(
        functools.partial(_nsa_prep_kernel, compress=compress),
        grid=(m // tm,),
        in_specs=[row(NSA_COLS), tab, tab, tab] + [res(a) for a in consts],
        out_specs=out_specs,
        out_shape=out_shape,
        compiler_params=_cparams(("parallel",)),
        name="nsa_prep",
    )(p_nsa, *tabs, *consts)


def _group_rows(q, g):
    lane = lax.broadcasted_iota(jnp.int32, (1, LANES), 1)
    in_half = (lane // NSA_DH) == g
    zero = jnp.zeros((), q.dtype)
    parts = []
    for pidx in range(NSA_HPG):
        h = g * NSA_HPG + pidx
        tile = q[:, (h // 2) * LANES:(h // 2 + 1) * LANES]
        if h % 2 != g:
            tile = jnp.concatenate([tile[:, NSA_DH:], tile[:, :NSA_DH]], axis=1)
        parts.append(jnp.where(in_half, tile, zero))
    return jnp.concatenate(parts, axis=0)


def _place_heads(o_groups, t):
    lane = lax.broadcasted_iota(jnp.int32, (1, LANES), 1)
    low = lane < NSA_DH
    tiles = []
    for j in range(NSA_HEADS // 2):
        pair = []
        for h in (2 * j, 2 * j + 1):
            g, pidx = divmod(h, NSA_HPG)
            x = o_groups[g][pidx * t:(pidx + 1) * t]
            if h % 2 != g:
                x = jnp.concatenate([x[:, NSA_DH:], x[:, :NSA_DH]], axis=1)
            pair.append(x)
        tiles.append(jnp.where(low, pair[0], pair[1]))
    return jnp.concatenate(tiles, axis=1)


def _expand_gates(gs, exp_ref, j):
    hi = gs.astype(BF16)
    lo = (gs - hi.astype(F32)).astype(BF16)
    e = exp_ref[j]
    return jnp.dot(hi, e, preferred_element_type=F32) + jnp.dot(lo, e, preferred_element_type=F32)


def _top_blocks_t(score_t, blk_row_f, n_rows):
    sel_t = jnp.zeros(score_t.shape, F32)
    for _ in range(NSA_TOPK):
        mx = jnp.max(score_t, axis=0, keepdims=True)
        first = jnp.min(jnp.where(score_t == mx, blk_row_f, float(n_rows)), axis=0, keepdims=True)
        pick = blk_row_f == first
        sel_t = jnp.where(pick, 1.0, sel_t)
        score_t = jnp.where(pick, -3e38, score_t)
    return sel_t


def _nsa_prompt_kernel(q_ref, gate_ref, kvc_ref, kvs_ref, kvw_ref, pool_ref, exp_ref, o_ref, *, tq, tk, nc, n_sel):
    start = pl.program_id(1) * tq
    grows = NSA_HPG * tq
    q = q_ref[...]
    tokpos = start + lax.broadcasted_iota(jnp.int32, (tq, 1), 0)
    qpos = start + (lax.broadcasted_iota(jnp.int32, (grows, 1), 0) & (tq - 1))
    c_end = lax.broadcasted_iota(jnp.int32, (1, nc), 1) * NSA_CMP_BLOCK + (NSA_CMP_BLOCK - 1)
    blk_lane = lax.broadcasted_iota(jnp.int32, (1, LANES), 1)
    blk_row_f = lax.broadcasted_iota(jnp.int32, (LANES, tq), 0).astype(F32)
    cur = tokpos // NSA_SEL_BLOCK
    n_kt = (start + tq - 1) // tk + 1
    lo = jnp.maximum(start - NSA_WINDOW, 0)
    wlen = NSA_WINDOW + tq
    kvc = kvc_ref[0]
    kc = kvc[:, :LANES].astype(BF16)
    vc = kvc[:, LANES:].astype(BF16)
    o_cmp, o_sel, o_win = [], [], []

    for g in range(NSA_GROUPS):
        qg = _group_rows(q, g)

        sc = lax.dot_general(qg, kc, _NT, preferred_element_type=F32)
        pc = _softmax_rows(sc, c_end <= qpos)
        o_cmp.append(jnp.dot(pc.astype(BF16), vc, preferred_element_type=F32))

        psum = pc[0:tq]
        for h in range(1, NSA_HPG):
            psum = psum + pc[h * tq:(h + 1) * tq]
        imp = _group_sum(psum, pool_ref[...])
        forced = (blk_lane == 0) | (blk_lane == cur) | (blk_lane == cur - 1)
        score = jnp.where(blk_lane > cur, -1e9, imp + NSA_FORCE * forced.astype(F32))
        score = jnp.where(blk_lane < n_sel, score, -3e38)
        sel = _top_blocks_t(score.T, blk_row_f, LANES).T.astype(BF16)

        def sel_step(kt, carry):
            m_prev, l_prev, acc = carry
            off = pl.multiple_of(kt * tk, tk)
            kv = kvs_ref[0, pl.ds(off, tk), :]
            k = kv[:, :LANES].astype(BF16)
            v = kv[:, LANES:].astype(BF16)
            s = lax.dot_general(qg, k, _NT, preferred_element_type=F32)
            keyi = kt * tk + lax.broadcasted_iota(jnp.int32, (1, tk), 1)
            expand = (lax.broadcasted_iota(jnp.int32, (LANES, tk), 0) == keyi // NSA_SEL_BLOCK).astype(BF16)
            selm = jnp.dot(sel, expand, preferred_element_type=F32)
            valid = (selm > 0.5) & (keyi <= tokpos)
            s3 = jnp.where(valid[None], s.reshape(NSA_HPG, tq, tk), NEG)
            m_new = jnp.maximum(m_prev, jnp.max(s3, axis=-1, keepdims=True))
            alpha = jnp.exp(m_prev - m_new)
            p3 = jnp.exp(s3 - m_new)
            l_new = alpha * l_prev + jnp.sum(p3, axis=-1, keepdims=True)
            pv = jnp.dot(p3.reshape(grows, tk).astype(BF16), v, preferred_element_type=F32)
            return m_new, l_new, alpha * acc + pv.reshape(NSA_HPG, tq, LANES)

        init = (jnp.full((NSA_HPG, tq, 1), NEG, F32), jnp.zeros((NSA_HPG, tq, 1), F32),
                jnp.zeros((NSA_HPG, tq, LANES), F32))
        _, l_fin, acc = lax.fori_loop(0, n_kt, sel_step, init)
        o_sel.append((acc / l_fin).reshape(grows, LANES))

        woff = pl.multiple_of(lo, tq)
        kvw = kvw_ref[0, pl.ds(woff, wlen), :]
        sw = lax.dot_general(qg, kvw[:, :LANES].astype(BF16), _NT, preferred_element_type=F32)
        dpos = tokpos - (lo + lax.broadcasted_iota(jnp.int32, (1, wlen), 1))
        validw = (dpos >= 0) & (dpos < NSA_WINDOW)
        sw3 = sw.reshape(NSA_HPG, tq, wlen)
        pw = _softmax_rows(sw3, jnp.broadcast_to(validw[None], sw3.shape))
        o_win.append(jnp.dot(pw.reshape(grows, wlen).astype(BF16), kvw[:, LANES:].astype(BF16),
                             preferred_element_type=F32))

    gs = gate_ref[...]
    o_ref[...] = (_expand_gates(gs, exp_ref, 0) * _place_heads(o_cmp, tq)
                  + _expand_gates(gs, exp_ref, 1) * _place_heads(o_sel, tq)
                  + _expand_gates(gs, exp_ref, 2) * _place_heads(o_win, tq))


def nsa_prompt(q8, gates, kvc, kv_s, kv_w, pool, gate_exp, n, *, tq=128, tk=1024):
    t = q8.shape[0] // n
    nc = kvc.shape[1]
    n_sel = t // NSA_SEL_BLOCK
    assert nc == LANES and 2 * n_sel == nc and t % tk == 0 and t >= NSA_WINDOW + tq
    nq = t // tq
    full = lambda a: pl.BlockSpec((1,) + a.shape[1:], lambda b, i: (b, 0, 0))
    res = lambda a: pl.BlockSpec(a.shape, lambda b, i: (0,) * a.ndim)
    return pl.pallas_call(
        functools.partial(_nsa_prompt_kernel, tq=tq, tk=tk, nc=nc, n_sel=n_sel),
        grid=(n, nq),
        in_specs=[pl.BlockSpec((tq, NSA_QW), lambda b, i: (b * nq + i, 0)),
                  pl.BlockSpec((tq, LANES), lambda b, i: (b * nq + i, 0)),
                  full(kvc), full(kv_s), full(kv_w), res(pool), res(gate_exp)],
        out_specs=pl.BlockSpec((tq, NSA_QW), lambda b, i: (b * nq + i, 0)),
        out_shape=jax.ShapeDtypeStruct((n * t, NSA_QW), F32),
        compiler_params=_cparams(("parallel", "arbitrary")),
        name="nsa_prompt",
    )(q8, gates, kvc, kv_s, kv_w, pool, gate_exp)


def _nsa_decode_kernel(pt_ref, q_ref, gate_ref, snew_ref, wnew_ref, win_ref, wt_ref, pbd_ref, pool_ref, exp_ref,
                       e0_ref, *refs, pg, n_steps, t_s, past):
    cmp_refs = refs[:pg]
    sel_refs = refs[pg:2 * pg]
    o_ref = refs[2 * pg]
    pooled_ref, selstep_ref, ocmp_ref, m_ref, l_ref, acc_ref = refs[2 * pg + 1:]
    j = pl.program_id(1)
    tp = q_ref.shape[1]
    grows = NSA_HPG * tp
    rows = NSA_GROUPS * grows
    page = cmp_refs[0].shape[3]
    nc = past // NSA_CMP_BLOCK
    nblk = pool_ref.shape[1]
    bps = pg * page // NSA_SEL_BLOCK
    q = q_ref[0]
    qs = jnp.concatenate([_group_rows(q, g) for g in range(NSA_GROUPS)], axis=0)
    tokpos = past + lax.broadcasted_iota(jnp.int32, (tp, 1), 0)
    rowtok = lax.broadcasted_iota(jnp.int32, (rows, 1), 0) & (tp - 1)

    cps = pg * page // NSA_CMP_BLOCK

    @pl.when(j < n_steps)
    def _():
        xw = jnp.concatenate([r[0, 0] * wt_ref[...] for r in cmp_refs], axis=1)
        pooled_ref[jnp.minimum(j, n_steps - 1)] = _group_sum(xw, e0_ref[...])

    @pl.when(j == n_steps - 1)
    def _():
        pooled = jnp.concatenate([pooled_ref[s][:, :cps] for s in range(n_steps)], axis=1)
        kvc_t = jnp.dot(pbd_ref[...], pooled.astype(BF16), preferred_element_type=F32)
        kc_t = kvc_t[:LANES].astype(BF16)
        vc_t = kvc_t[LANES:].astype(BF16)
        c_end = lax.broadcasted_iota(jnp.int32, (1, nc), 1) * NSA_CMP_BLOCK + (NSA_CMP_BLOCK - 1)
        sc = jnp.dot(qs, kc_t, preferred_element_type=F32)
        pc = _softmax_rows(sc, c_end <= past + rowtok)
        ocmp_ref[...] = lax.dot_general(pc.astype(BF16), vc_t, _NT, preferred_element_type=F32)
        blk_lane = lax.broadcasted_iota(jnp.int32, (1, nblk), 1)
        blk_lane_f = blk_lane.astype(F32)
        cur = tokpos // NSA_SEL_BLOCK
        n_sel = -(-(past + t_s) // NSA_SEL_BLOCK)
        forced = (blk_lane == 0) | (blk_lane == cur) | (blk_lane == cur - 1)
        sels = []
        for g in range(NSA_GROUPS):
            psum = pc[g * grows:g * grows + tp]
            for h in range(1, NSA_HPG):
                psum = psum + pc[g * grows + h * tp:g * grows + (h + 1) * tp]
            imp = _group_sum(psum, pool_ref[...])
            score = jnp.where(blk_lane > cur, -1e9, imp + NSA_FORCE * forced.astype(F32))
            score = jnp.where(blk_lane < n_sel, score, -3e38)
            sel = jnp.zeros((tp, nblk), F32)
            for _ in range(NSA_TOPK):
                mx = jnp.max(score, axis=1, keepdims=True)
                first = jnp.min(jnp.where(score == mx, blk_lane_f, float(nblk)), axis=1, keepdims=True)
                pick = blk_lane_f == first
                sel = jnp.where(pick, 1.0, sel)
                score = jnp.where(pick, -3e38, score)
            sels.append(jnp.concatenate([sel] * NSA_HPG, axis=0))
        sel_rows = jnp.concatenate(sels, axis=0)
        zpad = jnp.zeros((rows, LANES - bps), F32)
        for s in range(nblk // bps):
            selstep_ref[s] = jnp.concatenate([sel_rows[:, s * bps:(s + 1) * bps], zpad], axis=1)
        m_ref[...] = jnp.full(m_ref.shape, NEG, F32)
        l_ref[...] = jnp.zeros(l_ref.shape, F32)
        acc_ref[...] = jnp.zeros(acc_ref.shape, F32)

    def online(s, valid):
        s = jnp.where(valid, s, NEG)
        m_prev = m_ref[...]
        m_new = jnp.maximum(m_prev, jnp.max(s, axis=-1, keepdims=True))
        alpha = jnp.exp(m_prev - m_new)
        p = jnp.where(valid, jnp.exp(s - m_new), 0.0)
        l_ref[...] = alpha * l_ref[...] + jnp.sum(p, axis=-1, keepdims=True)
        m_ref[...] = m_new
        return alpha, p.astype(BF16)

    @pl.when(j >= n_steps)
    def _():
        jj = j - n_steps
        tiles = [r[0, 0] for r in sel_refs]
        s = jnp.concatenate([jnp.dot(qs, x[:LANES].astype(BF16), preferred_element_type=F32) for x in tiles], axis=1)
        tile = selstep_ref[jj]
        lane = lax.broadcasted_iota(jnp.int32, (1, page), 1)
        bpp = page // NSA_SEL_BLOCK
        masks = []
        for i in range(pg):
            m_i = tile[:, i * bpp:i * bpp + 1]
            for b in range(1, bpp):
                m_i = jnp.where(lane >= b * NSA_SEL_BLOCK, tile[:, i * bpp + b:i * bpp + b + 1], m_i)
            masks.append(jnp.broadcast_to(m_i, (rows, page)))
        valid = jnp.concatenate(masks, axis=1) > 0.5
        alpha, p = online(s, valid)
        pv = None
        for i, x in enumerate(tiles):
            term = lax.dot_general(p[:, i * page:(i + 1) * page], x[LANES:].astype(BF16), _NT,
                                   preferred_element_type=F32)
            pv = term if pv is None else pv + term
        acc_ref[...] = alpha * acc_ref[...] + pv

    @pl.when(j == 2 * n_steps - 1)
    def _():
        snew = snew_ref[0]
        wnew = wnew_ref[0]
        tk = lax.broadcasted_iota(jnp.int32, (1, tp), 1)
        causal_new = (tk <= rowtok) & (tk < t_s)
        new_blk = past // NSA_SEL_BLOCK
        lane = lax.broadcasted_iota(jnp.int32, (1, LANES), 1)
        last = selstep_ref[new_blk // bps]
        picked = jnp.sum(jnp.where(lane == new_blk % bps, last, 0.0), axis=1, keepdims=True) > 0.5
        s = lax.dot_general(qs, snew[:, :LANES].astype(BF16), _NT, preferred_element_type=F32)
        alpha, p = online(s, causal_new & picked)
        acc = alpha * acc_ref[...] + jnp.dot(p, snew[:, LANES:].astype(BF16), preferred_element_type=F32)
        o_sel = acc / jnp.maximum(l_ref[...], 1e-30)

        win = win_ref[0, 0]
        win_s = win.shape[1]
        wi = lax.broadcasted_iota(jnp.int32, (1, win_s), 1)
        dpos_c = (past + rowtok) - (past - win_s + wi)
        valid_c = (dpos_c >= 0) & (dpos_c < NSA_WINDOW)
        sc = jnp.dot(qs, win[:LANES].astype(BF16), preferred_element_type=F32)
        sn = lax.dot_general(qs, wnew[:, :LANES].astype(BF16), _NT, preferred_element_type=F32)
        sc = jnp.where(valid_c, sc, NEG)
        sn = jnp.where(causal_new, sn, NEG)
        mw = jnp.maximum(jnp.max(sc, axis=-1, keepdims=True), jnp.max(sn, axis=-1, keepdims=True))
        pcw = jnp.where(valid_c, jnp.exp(sc - mw), 0.0)
        pnw = jnp.where(causal_new, jnp.exp(sn - mw), 0.0)
        den = jnp.maximum(jnp.sum(pcw, axis=-1, keepdims=True) + jnp.sum(pnw, axis=-1, keepdims=True), 1e-30)
        o_win = (lax.dot_general(pcw.astype(BF16), win[LANES:].astype(BF16), _NT, preferred_element_type=F32)
                 + jnp.dot(pnw.astype(BF16), wnew[:, LANES:].astype(BF16), preferred_element_type=F32)) / den
        split = lambda a: [a[g * grows:(g + 1) * grows] for g in range(NSA_GROUPS)]
        gs = gate_ref[0]
        o_ref[0] = (_expand_gates(gs, exp_ref, 0) * _place_heads(split(ocmp_ref[...]), tp)
                    + _expand_gates(gs, exp_ref, 1) * _place_heads(split(o_sel), tp)
                    + _expand_gates(gs, exp_ref, 2) * _place_heads(split(o_win), tp))


def nsa_decode(page_table, q8, gates, kvs_new, kvw_new, cache_cmp_t, cache_sel_t, cache_win_t, wts, l, *, t_s, pg=32):
    b, tp, _ = q8.shape
    n_pages = page_table.shape[1]
    page = cache_cmp_t.shape[3]
    past = n_pages * page
    n_steps = n_pages // pg
    nc = past // NSA_CMP_BLOCK
    assert (past + t_s) // NSA_CMP_BLOCK == nc
    nblk = wts['pool_s'].shape[1]
    bps = pg * page // NSA_SEL_BLOCK
    assert nblk % bps == 0 and bps <= LANES and (past // NSA_SEL_BLOCK) // bps < nblk // bps
    rows = NSA_HEADS * tp
    pt = page_table.reshape(-1)
    assert pg * page // NSA_CMP_BLOCK <= LANES
    e0 = jnp.asarray((np.arange(pg * page)[:, None] // NSA_CMP_BLOCK) == np.arange(LANES)[None, :], BF16)

    def cmp_spec(i):
        return pl.BlockSpec((1, 1, NSA_KV_COLS, page),
                            lambda s, j, pt_ref: (l, pt_ref[s * n_pages + jnp.minimum(j, n_steps - 1) * pg + i], 0, 0))

    def sel_spec(i):
        return pl.BlockSpec((1, 1, NSA_KV_COLS, page),
                            lambda s, j, pt_ref: (l, pt_ref[s * n_pages + jnp.maximum(j - n_steps, 0) * pg + i], 0, 0))

    seq = lambda a: pl.BlockSpec((1,) + a.shape[1:], lambda s, j, pt_ref: (s, 0, 0))
    res = lambda a: pl.BlockSpec(a.shape, lambda s, j, pt_ref: (0,) * a.ndim)
    consts = [wts['cmp_w_t'], wts['cmp_proj_t'], wts['pool_s'], wts['gate_exp'], e0]
    grid_spec = pltpu.PrefetchScalarGridSpec(
        num_scalar_prefetch=1,
        grid=(b, 2 * n_steps),
        in_specs=[seq(q8), seq(gates), seq(kvs_new), seq(kvw_new),
                  pl.BlockSpec((1, 1) + cache_win_t.shape[2:], lambda s, j, pt_ref: (l, s, 0, 0))]
                 + [res(a) for a in consts]
                 + [cmp_spec(i) for i in range(pg)] + [sel_spec(i) for i in range(pg)],
        out_specs=pl.BlockSpec((1, tp, NSA_QW), lambda s, j, pt_ref: (s, 0, 0)),
        scratch_shapes=[pltpu.VMEM((n_steps, NSA_KV_COLS, LANES), F32),
                        pltpu.VMEM((nblk // bps, rows, LANES), F32),
                        pltpu.VMEM((rows, LANES), F32),
                        pltpu.VMEM((rows, 1), F32),
                        pltpu.VMEM((rows, 1), F32),
                        pltpu.VMEM((rows, LANES), F32)])
    return pl.pallas_call(
        functools.partial(_nsa_decode_kernel, pg=pg, n_steps=n_steps, t_s=t_s, past=past),
        grid_spec=grid_spec,
        out_shape=jax.ShapeDtypeStruct((b, tp, NSA_QW), F32),
        compiler_params=_cparams(("parallel", "arbitrary")),
        name="nsa_decode",
    )(pt, q8, gates, kvs_new, kvw_new, cache_win_t, *consts, *([cache_cmp_t] * pg), *([cache_sel_t] * pg))


def _merge_kernel(x_ref, oa_ref, ob_ref, oc_ref, br_ref, pa_ref, pb_ref, pc_ref, wo_ref, y_ref):
    d = D_MODEL
    br = br_ref[...]
    u = (jax.nn.sigmoid(br[:, 0:d]) * jnp.dot(oa_ref[...].astype(BF16), pa_ref[...], preferred_element_type=F32)
         + jax.nn.sigmoid(br[:, d:2 * d]) * jnp.dot(ob_ref[...].astype(BF16), pb_ref[...], preferred_element_type=F32)
         + jax.nn.sigmoid(br[:, 2 * d:3 * d]) * jnp.dot(oc_ref[...].astype(BF16), pc_ref[...], preferred_element_type=F32))
    y_ref[...] = x_ref[...] + jnp.dot(u.astype(BF16), wo_ref[...], preferred_element_type=F32)


def merge(x, o_a, o_b, o_c, br, p_a, p_b, p_c, w_out, *, tm=256):
    m, d = x.shape
    tm = min(tm, m)
    bw = o_a.shape[1]
    row = lambda w: pl.BlockSpec((tm, w), lambda i: (i, 0))
    res = lambda a: pl.BlockSpec(a.shape, lambda i: (0, 0))
    return pl.pallas_call(
        _merge_kernel,
        grid=(m // tm,),
        in_specs=[row(d), row(bw), row(bw), row(bw), row(3 * d), res(p_a), res(p_b), res(p_c), res(w_out)],
        out_specs=row(d),
        out_shape=jax.ShapeDtypeStruct((m, d), F32),
        compiler_params=_cparams(("parallel",)),
        name="merge",
    )(x, o_a, o_b, o_c, br, p_a, p_b, p_c, w_out)


def _router_kernel(x_ref, g_ref, wr_ref, h_ref, route_ref):
    x = x_ref[...]
    ms = jnp.mean(x * x, axis=-1, keepdims=True)
    hb = (x * lax.rsqrt(ms + NORM_EPS) * g_ref[...]).astype(BF16)
    h_ref[...] = hb
    logits = jnp.dot(hb, wr_ref[...], preferred_element_type=F32)
    lane = lax.broadcasted_iota(jnp.int32, (1, ROUTER_PAD), 1)
    lane_f = lane.astype(F32)
    big = float(ROUTER_PAD)

    def top1(vals):
        mx = jnp.max(vals, axis=-1, keepdims=True)
        return mx, jnp.min(jnp.where(vals == mx, lane_f, big), axis=-1, keepdims=True)

    gl = jnp.where(lane < MOE_GROUPS, logits, NEG)
    gmax, gidx = top1(gl)
    g_w = 1.0 / jnp.sum(jnp.where(lane < MOE_GROUPS, jnp.exp(gl - gmax), 0.0), axis=-1, keepdims=True)
    lo = MOE_GROUPS + MOE_EPG * gidx
    el = jnp.where((lane_f >= lo) & (lane_f < lo + MOE_EPG), logits, NEG)
    e1, i1 = top1(el)
    e2, i2 = top1(jnp.where(lane_f == i1, NEG, el))
    z = jnp.exp(e2 - e1)
    w1 = g_w / (1.0 + z)
    w2 = g_w * z / (1.0 + z)
    route_ref[...] = jnp.where(lane == 0, i1 - MOE_GROUPS,
                               jnp.where(lane == 1, i2 - MOE_GROUPS,
                                         jnp.where(lane == 2, w1, jnp.where(lane == 3, w2, 0.0))))


def norm_router(x, g, w_router, *, tm=512):
    m, d = x.shape
    tm = min(tm, m)
    return pl.pallas_call(
        _router_kernel,
        grid=(m // tm,),
        in_specs=[pl.BlockSpec((tm, d), lambda i: (i, 0)),
                  pl.BlockSpec((1, d), lambda i: (0, 0)),
                  pl.BlockSpec((d, ROUTER_PAD), lambda i: (0, 0))],
        out_specs=[pl.BlockSpec((tm, d), lambda i: (i, 0)),
                   pl.BlockSpec((tm, ROUTER_PAD), lambda i: (i, 0))],
        out_shape=[jax.ShapeDtypeStruct((m, d), BF16), jax.ShapeDtypeStruct((m, ROUTER_PAD), F32)],
        compiler_params=_cparams(("parallel",)),
        name="norm_router",
    )(x, g.reshape(1, d), w_router)


def _expert_kernel(be_ref, x_ref, rw_ref, w1_ref, w3_ref, w2_ref, o_ref):
    x = x_ref[...]
    a = jnp.dot(x, w1_ref[0], preferred_element_type=F32)
    b = jnp.dot(x, w3_ref[0], preferred_element_type=F32)
    hmid = _silu(a) * b
    y = jnp.dot(hmid.astype(BF16), w2_ref[0], preferred_element_type=F32)
    o_ref[...] = y * rw_ref[...]


def grouped_experts_blocks(xb, row_w, blk_e, w1, w3, w2):
    rows, d = xb.shape
    n_blk = rows // MOE_BLOCK
    grid_spec = pltpu.PrefetchScalarGridSpec(
        num_scalar_prefetch=1,
        grid=(n_blk,),
        in_specs=[pl.BlockSpec((MOE_BLOCK, d), lambda b, be: (b, 0)),
                  pl.BlockSpec((MOE_BLOCK, 1), lambda b, be: (b, 0)),
                  pl.BlockSpec((1, d, MOE_DFF), lambda b, be: (be[b], 0, 0)),
                  pl.BlockSpec((1, d, MOE_DFF), lambda b, be: (be[b], 0, 0)),
                  pl.BlockSpec((1, MOE_DFF, d), lambda b, be: (be[b], 0, 0))],
        out_specs=pl.BlockSpec((MOE_BLOCK, d), lambda b, be: (b, 0)))
    return pl.pallas_call(
        _expert_kernel,
        grid_spec=grid_spec,
        out_shape=jax.ShapeDtypeStruct((rows, d), F32),
        compiler_params=_cparams(("arbitrary",)),
        name="moe_experts",
    )(blk_e, xb, row_w.reshape(rows, 1), w1, w3, w2)


def hier_moe_residual(x, norm_g, w_router, w1, w3, w2):
    m, d = x.shape
    h, route = norm_router(x, norm_g, w_router)
    eid = route[:, :MOE_TOPK].astype(jnp.int32)
    ew = route[:, MOE_TOPK:2 * MOE_TOPK]

    n_assign = m * MOE_TOPK
    fe = eid.reshape(-1).astype(jnp.int32)
    onehot = (fe[:, None] == jnp.arange(MOE_EXPERTS, dtype=jnp.int32)[None, :]).astype(jnp.int32)
    csum = jnp.cumsum(onehot, axis=0)
    rank = jnp.sum((csum - onehot) * onehot, axis=1)
    counts = csum[-1]
    padded = (counts + MOE_BLOCK - 1) // MOE_BLOCK * MOE_BLOCK
    ends = jnp.cumsum(padded)
    dest = (ends - padded)[fe] + rank
    n_blk = -(-(n_assign + MOE_EXPERTS * (MOE_BLOCK - 1)) // MOE_BLOCK)
    rows = n_blk * MOE_BLOCK
    tok = jnp.arange(n_assign, dtype=jnp.int32) // MOE_TOPK
    row_tok = jnp.full((rows,), m, jnp.int32).at[dest].set(tok)
    row_w = jnp.zeros((rows,), F32).at[dest].set(ew.reshape(-1))
    blk_e = jnp.minimum(jnp.searchsorted(ends, jnp.arange(n_blk, dtype=jnp.int32) * MOE_BLOCK, side='right'),
                        MOE_EXPERTS - 1).astype(jnp.int32)
    xb = jnp.concatenate([h, jnp.zeros((1, d), h.dtype)], axis=0)[row_tok]
    yb = grouped_experts_blocks(xb, row_w, blk_e, w1, w3, w2)
    slot = dest.reshape(m, MOE_TOPK)
    y = yb[slot[:, 0]]
    for k in range(1, MOE_TOPK):
        y = y + yb[slot[:, k]]
    return x + y


def _rope_tables(pos):
    posf = pos.astype(F32)[:, None]
    inv_m = ROPE_THETA ** (-(jnp.arange(MLA_HALF, dtype=F32) * 2.0 / MLA_ROPE))
    ang_m = posf * inv_m[None, :]
    cos_m = jnp.tile(jnp.cos(ang_m), (1, LANES // MLA_HALF))
    sin_m = jnp.tile(jnp.sin(ang_m), (1, LANES // MLA_HALF))
    half = NSA_ROT // 2
    inv_n = ROPE_THETA ** (-(jnp.arange(half, dtype=F32) * 2.0 / NSA_ROT))
    ang_n = posf * inv_n[None, :]
    cn, sn = jnp.cos(ang_n), jnp.sin(ang_n)
    t = pos.shape[0]
    one = jnp.ones((t, NSA_DH - NSA_ROT), F32)
    zero = jnp.zeros((t, NSA_DH - NSA_ROT), F32)
    z8 = jnp.zeros((t, half), F32)
    c = jnp.concatenate([cn, cn, one], axis=1)
    s1 = jnp.concatenate([-sn, z8, zero], axis=1)
    s2 = jnp.concatenate([z8, sn, zero], axis=1)
    rep = lambda a: jnp.tile(a, (1, LANES // NSA_DH))
    return (cos_m, sin_m), (rep(c), rep(s1), rep(s2))


def _layer_weights(l, w_in, raw):
    w = {}
    wl = w_in[l]
    d = wl.shape[0]
    col = lambda i: wl[:, IN_OFF[i]:IN_OFF[i + 1]]
    zeros = lambda n: jnp.zeros((d, n), F32)
    w['w_gdn'] = jnp.concatenate([col(0), col(1), col(2), col(3), zeros(LANES - 2 * GDN_HEADS)], axis=1).astype(BF16)
    kr = col(6)
    kr_rep = jnp.concatenate([jnp.tile(kr[:, :MLA_HALF], (1, MLA_HEADS)), jnp.tile(kr[:, MLA_HALF:], (1, MLA_HEADS))],
                             axis=1)
    w['w_mla'] = jnp.concatenate([col(4), col(5), kr_rep, zeros(LANES)], axis=1).astype(BF16)
    w['w_nsa'] = jnp.concatenate([col(7), col(8), col(9), col(10), col(11), zeros(LANES - 3 * NSA_HEADS)],
                                 axis=1).astype(BF16)
    w['w_br'] = col(12).astype(BF16)

    uq = raw['mla_w_uq'][l].reshape(MLA_Q_LORA, MLA_HEADS, MLA_NOPE + MLA_ROPE)
    w['w_uq'] = jnp.concatenate([uq[:, :, :MLA_NOPE].reshape(MLA_Q_LORA, -1),
                                 uq[:, :, MLA_NOPE:MLA_NOPE + MLA_HALF].reshape(MLA_Q_LORA, -1),
                                 uq[:, :, MLA_NOPE + MLA_HALF:].reshape(MLA_Q_LORA, -1)], axis=1).astype(BF16)
    uk = jnp.transpose(raw['mla_w_uk'][l], (0, 2, 1))
    zk = jnp.zeros_like(uk)
    even = (jnp.arange(MLA_HEADS) % 2 == 0)[:, None, None]
    w['w_uk'] = jnp.concatenate([jnp.where(even, uk, zk), jnp.where(even, zk, uk)], axis=1).astype(BF16)
    w['w_uv'] = raw['mla_w_uv'][l].astype(BF16)
    w['w_uv_all'] = jnp.transpose(raw['mla_w_uv'][l], (1, 0, 2)).reshape(MLA_KV_LORA, MLA_HEADS * MLA_V).astype(BF16)
    w['cq_g'] = raw['mla_cq_g'][l].reshape(1, -1)
    w['ckv_g'] = raw['mla_ckv_g'][l].reshape(1, -1)
    w['qn_g'] = jnp.tile(raw['mla_qn_g'][l], MLA_HEADS).reshape(1, -1)
    split_tile = lambda g: jnp.stack([jnp.tile(g[:MLA_HALF], MLA_HEADS), jnp.tile(g[MLA_HALF:], MLA_HEADS)])
    w['qr_g'] = split_tile(raw['mla_qr_g'][l])
    w['kr_g'] = split_tile(raw['mla_kr_g'][l])
    w['bd64_512'] = _block_diag_ones(NSA_QW, NSA_DH)
    w['bd64_128'] = _block_diag_ones(LANES, NSA_DH)
    w['bd16'] = _block_diag_ones(LANES, MLA_HALF)

    w['nsa_qn_g'] = jnp.tile(raw['nsa_qn_g'][l], NSA_HEADS).reshape(1, -1)
    w['nsa_kn_g'] = jnp.tile(raw['nsa_kn_g'][l], (1, NSA_GROUPS))
    wpos = raw['nsa_cmp_wpos'][l]
    wrow = jnp.concatenate([jnp.tile(wpos[0], (1, NSA_GROUPS)), jnp.tile(wpos[1], (1, NSA_GROUPS))], axis=1)
    w['cmp_w'] = jnp.tile(wrow, (512 // NSA_CMP_BLOCK, 1))
    proj = raw['nsa_cmp_proj'][l]
    pbd = jnp.zeros((NSA_KV_COLS, NSA_KV_COLS), F32)
    for kv in range(2):
        for g in range(NSA_GROUPS):
            o = (kv * NSA_GROUPS + g) * NSA_DH
            pbd = pbd.at[o:o + NSA_DH, o:o + NSA_DH].set(proj[kv])
    w['cmp_proj'] = pbd.astype(BF16)
    w['cmp_w_t'] = jnp.tile(wrow.T, (1, LANES // NSA_CMP_BLOCK))
    w['cmp_proj_t'] = pbd.T.astype(BF16)
    ge = np.zeros((3, LANES, NSA_QW), np.float32)
    for j in range(3):
        for h in range(NSA_HEADS):
            ge[j, 3 * h + j, h * NSA_DH:(h + 1) * NSA_DH] = 1.0
    w['gate_exp'] = jnp.asarray(ge, BF16)

    w['p_gdn'] = raw['p_gdn'][l].astype(BF16)
    w['p_mla'] = raw['p_mla'][l].astype(BF16)
    w['p_nsa'] = raw['p_nsa'][l].astype(BF16)
    w['w_out'] = raw['w_out'][l].astype(BF16)
    w['w_router'] = jnp.pad(jnp.concatenate([raw['moe_w_group'][l], raw['moe_w_expert'][l]], axis=1),
                            ((0, 0), (0, ROUTER_PAD - MOE_GROUPS - MOE_EXPERTS))).astype(BF16)
    w['w1'] = raw['moe_w1'][l].astype(BF16)
    w['w3'] = raw['moe_w3'][l].astype(BF16)
    w['w2'] = raw['moe_w2'][l].astype(BF16)
    return w


def _pool_matrix(n_in, n_out):
    pm = np.zeros((n_in, n_out), np.float32)
    ratio = NSA_SEL_BLOCK // NSA_CMP_BLOCK
    for c in range(n_in):
        if c // ratio < n_out:
            pm[c, c // ratio] = 1.0
    return jnp.asarray(pm, BF16)


def kernel(x_prompt, x_sample, cache_mla_ckv, cache_mla_krope, cache_nsa_cmp, cache_nsa_sel, cache_nsa_win, state_gdn_ssm, state_gdn_conv, page_table, norm1_g, w_in, gdn_conv_w, gdn_a_log, gdn_dt_bias, gdn_out_g, mla_cq_g, mla_ckv_g, mla_w_uq, mla_qn_g, mla_qr_g, mla_kr_g, mla_w_uk, mla_w_uv, nsa_qn_g, nsa_kn_g, nsa_cmp_wpos, nsa_cmp_proj, p_gdn, p_mla, p_nsa, w_out, norm2_g, moe_w_group, moe_w_expert, moe_w1, moe_w3, moe_w2):
    depth = w_in.shape[0]
    b_p, t_p, d = x_prompt.shape
    b_s, t_s, _ = x_sample.shape
    n_pool, page = cache_mla_ckv.shape[1:3]
    past = page_table.shape[1] * page
    win_s = cache_nsa_win.shape[2]
    assert t_p >= GDN_CONV - 1 and t_s >= GDN_CONV - 1 and t_p % GDN_CHUNK == 0
    tp_s = 8
    assert t_s <= tp_s
    raw = dict(mla_w_uq=mla_w_uq, mla_w_uk=mla_w_uk, mla_w_uv=mla_w_uv, mla_cq_g=mla_cq_g, mla_ckv_g=mla_ckv_g,
               mla_qn_g=mla_qn_g, mla_qr_g=mla_qr_g, mla_kr_g=mla_kr_g, nsa_qn_g=nsa_qn_g, nsa_kn_g=nsa_kn_g,
               nsa_cmp_wpos=nsa_cmp_wpos, nsa_cmp_proj=nsa_cmp_proj, p_gdn=p_gdn, p_mla=p_mla, p_nsa=p_nsa,
               w_out=w_out, moe_w_group=moe_w_group, moe_w_expert=moe_w_expert, moe_w1=moe_w1, moe_w3=moe_w3,
               moe_w2=moe_w2)
    pos_p = jnp.arange(t_p, dtype=jnp.int32)
    pos_s = jnp.tile(past + jnp.arange(t_s, dtype=jnp.int32), b_s)
    mla_tab_p, nsa_tab_p = _rope_tables(pos_p)
    mla_tab_s, nsa_tab_s = _rope_tables(pos_s)
    pool_p = _pool_matrix(LANES, LANES)
    nc_s = past // NSA_CMP_BLOCK
    pool_s = _pool_matrix(nc_s, nc_s)
    feat_major = lambda a: jnp.transpose(a, (0, 1, 3, 4, 5, 2)).reshape(a.shape[0], a.shape[1], NSA_KV_COLS, a.shape[2])
    cache_cmp_t = feat_major(cache_nsa_cmp)
    cache_sel_t = feat_major(cache_nsa_sel)
    cache_win_t = feat_major(cache_nsa_win)
    cache_kr_t = jnp.transpose(cache_mla_krope, (0, 1, 3, 2))

    def project(x2, l, w):
        g = norm1_g[l]
        return (norm_matmul(x2, g, w['w_gdn']), norm_matmul(x2, g, w['w_mla']),
                norm_matmul(x2, g, w['w_nsa']), norm_matmul(x2, g, w['w_br']))

    def finish(x2, o_a, o_b, o_c, br, l, w):
        y = merge(x2, o_a, o_b, o_c, br, w['p_gdn'], w['p_mla'], w['p_nsa'], w['w_out'])
        return hier_moe_residual(y, norm2_g[l], w['w_router'], w['w1'], w['w3'], w['w2'])

    def prompt_layer(x, l, w):
        m = b_p * t_p
        x2 = x.reshape(m, d)
        pg_, pm_, pn_, br = project(x2, l, w)
        pg3 = pg_.reshape(b_p, t_p, GDN_COLS)
        o_a, ssm = gdn_mixer(pg3, jnp.zeros((b_p, 8, GDN_CONV_DIM), F32),
                             jnp.zeros((b_p, GDN_HEADS, GDN_DK, GDN_DV), F32),
                             gdn_conv_w[l], gdn_a_log[l], gdn_dt_bias[l], gdn_out_g[l],
                             chunk=GDN_CHUNK, t_valid=GDN_CHUNK, nb=1, cb=4)
        conv = pg3[:, t_p - (GDN_CONV - 1):, :GDN_CONV_DIM]
        q_cat, k_cat, ckv, krope = mla_prep(pm_, *mla_tab_p, w, tm=256)
        o_b = mla_flash(q_cat, k_cat, w['w_uv'], b_p)
        q8, kv_c, kv_s, kv_w, gates, kvc = nsa_prep(pn_, nsa_tab_p, w, tm=256, compress=True)
        o_c = nsa_prompt(q8, gates, kvc.reshape(b_p, t_p // NSA_CMP_BLOCK, NSA_KV_COLS),
                         kv_s.reshape(b_p, t_p, NSA_KV_COLS), kv_w.reshape(b_p, t_p, NSA_KV_COLS),
                         pool_p, w['gate_exp'], b_p)
        y = finish(x2, o_a.reshape(m, GDN_VW), o_b, o_c, br, l, w).reshape(b_p, t_p, d)
        kv5 = lambda a: a.reshape(b_p, t_p, 2, NSA_GROUPS, NSA_DH)
        kvw5 = kv5(kv_w)
        return y, (ssm, conv, ckv.reshape(b_p, t_p, MLA_KV_LORA), krope.reshape(b_p, t_p, MLA_ROPE),
                   kv5(kv_c), kv5(kv_s), kvw5[:, t_p - min(NSA_WINDOW, t_p):])

    def sample_layer(x, l, w):
        m = b_s * t_s
        x2 = x.reshape(m, d)
        pg_, pm_, pn_, br = project(x2, l, w)
        pad_t = lambda a: jnp.pad(a.reshape(b_s, t_s, -1), ((0, 0), (0, tp_s - t_s), (0, 0)))
        pg3 = pg_.reshape(b_s, t_s, GDN_COLS)
        hist8 = jnp.pad(state_gdn_conv[l], ((0, 0), (8 - (GDN_CONV - 1), 0), (0, 0)))
        o_a, ssm = gdn_mixer(pad_t(pg_), hist8, state_gdn_ssm[l], gdn_conv_w[l], gdn_a_log[l], gdn_dt_bias[l],
                             gdn_out_g[l], chunk=tp_s, t_valid=t_s, nb=4, cb=1)
        o_a = o_a[:, :t_s].reshape(m, GDN_VW)
        conv = pg3[:, t_s - (GDN_CONV - 1):, :GDN_CONV_DIM]

        q_cat, k_cat, ckv, krope = mla_prep(pm_, *mla_tab_s, w, tm=m)
        qc = q_cat.reshape(m, MLA_HEADS, MLA_QK)
        q_lat = qc[:, :, :MLA_KV_LORA]
        hsel = jnp.arange(MLA_HEADS)
        r1 = qc[:, :, MLA_KV_LORA:MLA_KV_LORA + LANES].reshape(m, MLA_HEADS, MLA_HEADS, MLA_HALF)[:, hsel, hsel]
        r2 = qc[:, :, MLA_KV_LORA + LANES:].reshape(m, MLA_HEADS, MLA_HEADS, MLA_HALF)[:, hsel, hsel]
        q_rope = jnp.concatenate([r1, r2], axis=-1)
        rows_of = lambda a: pad_t(a.reshape(m, -1)).reshape(b_s, tp_s * MLA_HEADS, -1)
        o_b = mla_decode(page_table, rows_of(q_lat), rows_of(q_rope), pad_t(ckv).astype(BF16),
                         pad_t(krope).astype(BF16), w['w_uv_all'], cache_mla_ckv, cache_kr_t, l, t_s=t_s)
        o_b = o_b[:, :t_s].reshape(m, MLA_HEADS * MLA_V)

        q8, kv_c, kv_s, kv_w, gates = nsa_prep(pn_, nsa_tab_s, w, tm=m, compress=False)
        ws = dict(w, pool_s=pool_s)
        o_c = nsa_decode(page_table, pad_t(q8), pad_t(gates), pad_t(kv_s), pad_t(kv_w), cache_cmp_t, cache_sel_t,
                         cache_win_t, ws, l, t_s=t_s)
        o_c = o_c[:, :t_s].reshape(m, NSA_QW)
        y = finish(x2, o_a, o_b, o_c, br, l, w).reshape(b_s, t_s, d)
        kv5 = lambda a: a.reshape(b_s, t_s, 2, NSA_GROUPS, NSA_DH)
        win_new = jnp.concatenate([cache_nsa_win[l], kv5(kv_w)], axis=1)[:, t_s:]
        return y, (ssm, conv, ckv.reshape(b_s, t_s, MLA_KV_LORA), krope.reshape(b_s, t_s, MLA_ROPE),
                   kv5(kv_c), kv5(kv_s), win_new)

    xp, xs = x_prompt, x_sample
    p_states, s_states = [], []
    for l in range(depth):
        w = _layer_weights(l, w_in, raw)
        xp, st = prompt_layer(xp, l, w)
        p_states.append(st)
        xs, st = sample_layer(xs, l, w)
        s_states.append(st)
    p_out = [jnp.stack([st[i] for st in p_states], axis=0) for i in range(7)]
    s_out = [jnp.stack([st[i] for st in s_states], axis=0) for i in range(7)]
    return (xp, xs, *p_out, *s_out)
```

```python
import functools
import math

import numpy as np
import jax
import jax.numpy as jnp
from jax import lax
from jax.experimental import pallas as pl
from jax.experimental.pallas import tpu as pltpu

F32 = jnp.float32
BF16 = jnp.bfloat16
HIGHEST = lax.Precision.HIGHEST

D_MODEL = 1024
ROPE_THETA = 500000.0
NORM_EPS = 1e-6
NEG = -1e30
LANES = 128

GDN_HEADS = 4
GDN_DK = 128
GDN_DV = 128
GDN_CONV = 4
GDN_CHUNK = 64
GDN_QK = GDN_HEADS * GDN_DK
GDN_VW = GDN_HEADS * GDN_DV
GDN_CONV_DIM = 2 * GDN_QK + GDN_VW

MLA_HEADS = 8
MLA_Q_LORA = 384
MLA_KV_LORA = 256
MLA_NOPE = 64
MLA_ROPE = 32
MLA_HALF = MLA_ROPE // 2
MLA_V = 64
MLA_SCALE = (MLA_NOPE + MLA_ROPE) ** -0.5
MLA_QK = MLA_KV_LORA + 2 * LANES
FLASH_HEADS_PER_BLOCK = 2
DECODE_PAGE_GROUPS = 4

NSA_HEADS = 8
NSA_GROUPS = 2
NSA_HPG = NSA_HEADS // NSA_GROUPS
NSA_DH = 64
NSA_ROT = NSA_DH // 4
NSA_CMP_BLOCK = 32
NSA_SEL_BLOCK = 64
NSA_TOPK = 16
NSA_WINDOW = 512
NSA_FORCE = 1000.0
NSA_SCALE = NSA_DH ** -0.5
NSA_KV_COLS = 2 * NSA_GROUPS * NSA_DH
NSA_QW = NSA_HEADS * NSA_DH

MOE_GROUPS = 4
MOE_EPG = 8
MOE_EXPERTS = MOE_GROUPS * MOE_EPG
MOE_TOPK = 2
MOE_DFF = 512
MOE_BLOCK = 128
ROUTER_PAD = LANES

IN_SPLITS = (GDN_CONV_DIM, GDN_VW, GDN_HEADS, GDN_HEADS,
             MLA_Q_LORA, MLA_KV_LORA, MLA_ROPE,
             NSA_HEADS * NSA_DH, NSA_KV_COLS, NSA_KV_COLS, NSA_KV_COLS, 3 * NSA_HEADS,
             3 * D_MODEL)
IN_OFF = [0]
for _w in IN_SPLITS:
    IN_OFF.append(IN_OFF[-1] + _w)

GDN_COLS = GDN_CONV_DIM + GDN_VW + LANES
MLA_COLS = MLA_Q_LORA + MLA_KV_LORA + 2 * LANES + LANES
NSA_COLS = NSA_QW + 3 * NSA_KV_COLS + LANES

VMEM_LIMIT = 52 * 1024 * 1024

_NT = (((1,), (1,)), ((), ()))
_TN = (((0,), (0,)), ((), ()))


def _cparams(sem):
    return pltpu.CompilerParams(dimension_semantics=sem, vmem_limit_bytes=VMEM_LIMIT)


def _block_diag_ones(n, grp):
    i = np.arange(n)
    return jnp.asarray((i[:, None] // grp) == (i[None, :] // grp), BF16)


def _group_sum(sq, bd):
    hi = sq.astype(BF16)
    lo = (sq - hi.astype(F32)).astype(BF16)
    return jnp.dot(hi, bd, preferred_element_type=F32) + jnp.dot(lo, bd, preferred_element_type=F32)


def _softmax_rows(s, mask):
    s = jnp.where(mask, s, NEG)
    m = jnp.max(s, axis=-1, keepdims=True)
    p = jnp.where(mask, jnp.exp(s - m), 0.0)
    return p / jnp.maximum(jnp.sum(p, axis=-1, keepdims=True), 1e-30)


def _silu(x):
    return x * jax.nn.sigmoid(x)


def _norm_matmul_kernel(x_ref, g_ref, w_ref, o_ref, h_ref):
    @pl.when(pl.program_id(1) == 0)
    def _():
        x = x_ref[...]
        ms = jnp.mean(x * x, axis=-1, keepdims=True)
        h_ref[...] = (x * lax.rsqrt(ms + NORM_EPS) * g_ref[...]).astype(BF16)

    o_ref[...] = jnp.dot(h_ref[...], w_ref[...], preferred_element_type=F32)


def norm_matmul(x, g, w, *, tm=512, tn=None):
    m, k = x.shape
    n = w.shape[1]
    tm = min(tm, m)
    tn = n if tn is None else tn
    assert m % tm == 0 and n % tn == 0
    return pl.pallas_call(
        _norm_matmul_kernel,
        grid=(m // tm, n // tn),
        in_specs=[pl.BlockSpec((tm, k), lambda i, j: (i, 0)),
                  pl.BlockSpec((1, k), lambda i, j: (0, 0)),
                  pl.BlockSpec((k, tn), lambda i, j: (0, j))],
        out_specs=pl.BlockSpec((tm, tn), lambda i, j: (i, j)),
        out_shape=jax.ShapeDtypeStruct((m, n), F32),
        scratch_shapes=[pltpu.VMEM((tm, k), BF16)],
        compiler_params=_cparams(("parallel", "arbitrary")),
        name="norm_matmul",
    )(x, g.reshape(1, k), w)


def _gdn_kernel(p_ref, hist_ref, s0_ref, cw_ref, alog_ref, dtb_ref, og_ref, o_ref, sfin_ref,
                prev_ref, s_ref, *, chunk, t_valid, nb, cb):
    c = pl.program_id(1)
    n_steps = pl.num_programs(1)
    qk = GDN_QK
    span = cb * chunk
    dot32 = functools.partial(jnp.dot, preferred_element_type=F32, precision=HIGHEST)
    dotb = lambda a, b: jnp.dot(a.astype(BF16), b.astype(BF16), preferred_element_type=F32)

    @pl.when(c == 0)
    def _():
        prev_ref[...] = hist_ref[...]
        s_ref[...] = s0_ref[...]

    row = lax.broadcasted_iota(jnp.int32, (chunk, chunk), 0)
    col = lax.broadcasted_iota(jnp.int32, (chunk, chunk), 1)
    incl = row >= col
    strict = row > col
    eye_f = (row == col).astype(F32)
    incl_f = incl.astype(F32)
    live = (lax.broadcasted_iota(jnp.int32, (chunk, 1), 0) < t_valid) if t_valid < chunk else None
    n_fac = int(math.log2(chunk)) - 1

    chains = []
    raws, qkvs, betas, gcs, gcts = [], [], {}, {}, {}
    for s in range(nb):
        raw = p_ref[s]
        xcat = jnp.concatenate([prev_ref[s], raw[:, :GDN_CONV_DIM]], axis=0)
        prev_ref[s] = xcat[span:span + 8]
        acc = xcat[8:8 + span] * cw_ref[GDN_CONV - 1:GDN_CONV, :]
        for i in range(GDN_CONV - 1):
            sh = pltpu.roll(xcat, GDN_CONV - 1 - i, 0)
            acc = acc + sh[8:8 + span] * cw_ref[i:i + 1, :]
        raws.append(raw)
        qkvs.append(_silu(acc))
        ba = raw[:, GDN_CONV_DIM + GDN_VW:]
        xa = ba + dtb_ref[...]
        softplus = jnp.maximum(xa, 0.0) + jnp.log(1.0 + jnp.exp(-jnp.abs(xa)))
        beta_all = jax.nn.sigmoid(ba)
        g_all = -jnp.exp(alog_ref[...]) * softplus
        for ci in range(cb):
            bt = beta_all[ci * chunk:(ci + 1) * chunk]
            gt = g_all[ci * chunk:(ci + 1) * chunk]
            if live is not None:
                bt = jnp.where(live, bt, 0.0)
                gt = jnp.where(live, gt, 0.0)
            betas[(s, ci)] = bt
            gcs[(s, ci)] = dot32(incl_f, gt)
            gcts[(s, ci)] = lax.dot_general(gt, incl_f, (((0,), (1,)), ((), ())), preferred_element_type=F32,
                                            precision=HIGHEST)
    for ci in range(cb):
        for s in range(nb):
            for h in range(GDN_HEADS):
                chains.append((s, ci, h))
    nch = len(chains)

    def tile(s, ci, off):
        return qkvs[s][ci * chunk:(ci + 1) * chunk, off:off + GDN_DK]

    qs = [tile(s, ci, h * GDN_DK) for s, ci, h in chains]
    ks = [tile(s, ci, qk + h * GDN_DK) for s, ci, h in chains]
    vs = [tile(s, ci, 2 * qk + h * GDN_DV) for s, ci, h in chains]
    qs = [q * lax.rsqrt(jnp.sum(q * q, axis=-1, keepdims=True) + NORM_EPS) * (GDN_DK ** -0.5) for q in qs]
    ks = [k * lax.rsqrt(jnp.sum(k * k, axis=-1, keepdims=True) + NORM_EPS) for k in ks]
    beta = [betas[(s, ci)][:, h:h + 1] for s, ci, h in chains]
    gcol = [gcs[(s, ci)][:, GDN_HEADS + h:GDN_HEADS + h + 1] for s, ci, h in chains]
    grow = [gcts[(s, ci)][GDN_HEADS + h:GDN_HEADS + h + 1, :] for s, ci, h in chains]
    decay = [jnp.where(incl, jnp.exp(jnp.where(incl, gcol[i] - grow[i], 0.0)), 0.0) for i in range(nch)]
    kb = [ks[i] * beta[i] for i in range(nch)]
    kbf = [k.astype(BF16) for k in ks]
    a_kk = [jnp.where(strict, lax.dot_general(kb[i].astype(BF16), kbf[i], _NT, preferred_element_type=F32) * decay[i],
                      0.0) for i in range(nch)]
    inv = [eye_f - a for a in a_kk]
    pw = a_kk
    for _ in range(n_fac):
        pw = [dot32(p, p) for p in pw]
        inv = [inv[i] + dot32(inv[i], pw[i]) for i in range(nch)]
    egc = [jnp.exp(g) for g in gcol]
    sol = [dot32(inv[i], jnp.concatenate([vs[i] * beta[i], kb[i] * egc[i]], axis=1)) for i in range(nch)]
    a_qk = [lax.dot_general(qs[i].astype(BF16), kbf[i], _NT, preferred_element_type=F32) * decay[i]
            for i in range(nch)]
    qe = [(qs[i] * egc[i]).astype(BF16) for i in range(nch)]
    g_last = [g[chunk - 1:chunk, :] for g in gcol]
    kdec = [(ks[i] * jnp.exp(g_last[i] - gcol[i])).astype(BF16) for i in range(nch)]
    e_last = [jnp.exp(g) for g in g_last]

    per = nb * GDN_HEADS
    state = {(s, h): s_ref[s, h] for s in range(nb) for h in range(GDN_HEADS)}
    for ci in range(cb):
        idx = range(ci * per, (ci + 1) * per)
        sb = {i: state[(chains[i][0], chains[i][2])].astype(BF16) for i in idx}
        v_new = {i: sol[i][:, :GDN_DV] - jnp.dot(sol[i][:, GDN_DV:].astype(BF16), sb[i], preferred_element_type=F32)
                 for i in idx}
        o = {i: jnp.dot(qe[i], sb[i], preferred_element_type=F32) + dotb(a_qk[i], v_new[i]) for i in idx}
        for i in idx:
            s, _, h = chains[i]
            state[(s, h)] = state[(s, h)] * e_last[i] + lax.dot_general(kdec[i], v_new[i].astype(BF16), _TN,
                                                                        preferred_element_type=F32)
            on = o[i] * lax.rsqrt(jnp.mean(o[i] * o[i], axis=-1, keepdims=True) + NORM_EPS) * og_ref[...]
            z = raws[s][ci * chunk:(ci + 1) * chunk, GDN_CONV_DIM + h * GDN_DV:GDN_CONV_DIM + (h + 1) * GDN_DV]
            o_ref[s, ci * chunk:(ci + 1) * chunk, h * GDN_DV:(h + 1) * GDN_DV] = on * _silu(z)
    for (s, h), val in state.items():
        s_ref[s, h] = val

    @pl.when(c == n_steps - 1)
    def _():
        sfin_ref[...] = s_ref[...]


def gdn_mixer(p_gdn, hist8, s0, conv_w, a_log, dt_bias, out_g, *, chunk, t_valid, nb, cb):
    n, tp, _ = p_gdn.shape
    span = cb * chunk
    assert n % nb == 0 and tp % span == 0
    pad4 = jnp.zeros((GDN_HEADS,), F32)
    lane_row = lambda a: jnp.pad(jnp.concatenate([pad4, a]), (0, LANES - 2 * GDN_HEADS)).reshape(1, LANES)
    return pl.pallas_call(
        functools.partial(_gdn_kernel, chunk=chunk, t_valid=t_valid, nb=nb, cb=cb),
        grid=(n // nb, tp // span),
        in_specs=[pl.BlockSpec((nb, span, GDN_COLS), lambda b, c: (b, c, 0)),
                  pl.BlockSpec((nb, 8, GDN_CONV_DIM), lambda b, c: (b, 0, 0)),
                  pl.BlockSpec((nb, GDN_HEADS, GDN_DK, GDN_DV), lambda b, c: (b, 0, 0, 0)),
                  pl.BlockSpec((GDN_CONV, GDN_CONV_DIM), lambda b, c: (0, 0)),
                  pl.BlockSpec((1, LANES), lambda b, c: (0, 0)),
                  pl.BlockSpec((1, LANES), lambda b, c: (0, 0)),
                  pl.BlockSpec((1, GDN_DV), lambda b, c: (0, 0))],
        out_specs=[pl.BlockSpec((nb, span, GDN_VW), lambda b, c: (b, c, 0)),
                   pl.BlockSpec((nb, GDN_HEADS, GDN_DK, GDN_DV), lambda b, c: (b, 0, 0, 0))],
        out_shape=[jax.ShapeDtypeStruct((n, tp, GDN_VW), F32),
                   jax.ShapeDtypeStruct((n, GDN_HEADS, GDN_DK, GDN_DV), F32)],
        scratch_shapes=[pltpu.VMEM((nb, 8, GDN_CONV_DIM), F32), pltpu.VMEM((nb, GDN_HEADS, GDN_DK, GDN_DV), F32)],
        compiler_params=_cparams(("parallel", "arbitrary")),
        name="gdn_mixer",
    )(p_gdn, hist8, s0, conv_w, lane_row(a_log), lane_row(dt_bias), out_g.reshape(1, GDN_DV))


def _mla_prep_kernel(p_ref, cos_ref, sin_ref, cqg_ref, ckvg_ref, wuq_ref, qng_ref, qrg_ref, krg_ref, wuk_ref,
                     bd64_ref, bd16_ref, qcat_ref, kcat_ref, ckv_ref, krope_ref):
    p = p_ref[...]
    cos = cos_ref[...]
    sin = sin_ref[...]
    lane = lax.broadcasted_iota(jnp.int32, (1, LANES), 1)

    cq = p[:, :MLA_Q_LORA]
    cq = cq * lax.rsqrt(jnp.mean(cq * cq, axis=-1, keepdims=True) + NORM_EPS) * cqg_ref[...]
    q = jnp.dot(cq.astype(BF16), wuq_ref[...], preferred_element_type=F32)
    nw = MLA_HEADS * MLA_NOPE
    nope = q[:, :nw]
    nope = nope * lax.rsqrt(_group_sum(nope * nope, bd64_ref[...]) * (1.0 / MLA_NOPE) + NORM_EPS) * qng_ref[...]
    x1 = q[:, nw:nw + LANES]
    x2 = q[:, nw + LANES:]
    r = lax.rsqrt(_group_sum(x1 * x1 + x2 * x2, bd16_ref[...]) * (1.0 / MLA_ROPE) + NORM_EPS)
    x1 = x1 * r * qrg_ref[0:1, :]
    x2 = x2 * r * qrg_ref[1:2, :]
    r1 = x1 * cos - x2 * sin
    r2 = x2 * cos + x1 * sin
    for h in range(MLA_HEADS):
        tile = nope[:, (h // 2) * LANES:(h // 2 + 1) * LANES].astype(BF16)
        q_lat = jnp.dot(tile, wuk_ref[h], preferred_element_type=F32)
        mine = (lane // MLA_HALF) == h
        base = h * MLA_QK
        qcat_ref[:, base:base + MLA_KV_LORA] = q_lat.astype(BF16)
        qcat_ref[:, base + MLA_KV_LORA:base + MLA_KV_LORA + LANES] = jnp.where(mine, r1, 0.0).astype(BF16)
        qcat_ref[:, base + MLA_KV_LORA + LANES:base + MLA_QK] = jnp.where(mine, r2, 0.0).astype(BF16)

    ckv = p[:, MLA_Q_LORA:MLA_Q_LORA + MLA_KV_LORA]
    ckv = ckv * lax.rsqrt(jnp.mean(ckv * ckv, axis=-1, keepdims=True) + NORM_EPS) * ckvg_ref[...]
    ckv_ref[...] = ckv
    kr = p[:, MLA_Q_LORA + MLA_KV_LORA:MLA_Q_LORA + MLA_KV_LORA + 2 * LANES]
    ms = jnp.sum(kr * kr, axis=-1, keepdims=True) * (1.0 / (MLA_HEADS * MLA_ROPE))
    rk = lax.rsqrt(ms + NORM_EPS)
    k1 = kr[:, :LANES] * rk * krg_ref[0:1, :]
    k2 = kr[:, LANES:] * rk * krg_ref[1:2, :]
    k1r = k1 * cos - k2 * sin
    k2r = k2 * cos + k1 * sin
    kcat_ref[:, :MLA_KV_LORA] = ckv.astype(BF16)
    kcat_ref[:, MLA_KV_LORA:MLA_KV_LORA + LANES] = k1r.astype(BF16)
    kcat_ref[:, MLA_KV_LORA + LANES:] = k2r.astype(BF16)
    std = jnp.where(lane < MLA_HALF, k1r, pltpu.roll(k2r, MLA_HALF, 1))
    krope_ref[...] = std[:, :MLA_ROPE]


def mla_prep(p_mla, cos, sin, wts, *, tm):
    m = p_mla.shape[0]
    nt = cos.shape[0] // tm
    row = lambda w: pl.BlockSpec((tm, w), lambda i: (i, 0))
    res = lambda a: pl.BlockSpec(a.shape, lambda i: (0,) * a.ndim)
    tab = pl.BlockSpec((tm, LANES), lambda i: (i % nt, 0))
    consts = [wts['cq_g'], wts['ckv_g'], wts['w_uq'], wts['qn_g'], wts['qr_g'], wts['kr_g'], wts['w_uk'],
              wts['bd64_512'], wts['bd16']]
    return pl.pallas_call(
        _mla_prep_kernel,
        grid=(m // tm,),
        in_specs=[row(MLA_COLS), tab, tab] + [res(a) for a in consts],
        out_specs=[row(MLA_HEADS * MLA_QK), row(MLA_QK), row(MLA_KV_LORA), row(MLA_ROPE)],
        out_shape=[jax.ShapeDtypeStruct((m, MLA_HEADS * MLA_QK), BF16), jax.ShapeDtypeStruct((m, MLA_QK), BF16),
                   jax.ShapeDtypeStruct((m, MLA_KV_LORA), F32), jax.ShapeDtypeStruct((m, MLA_ROPE), F32)],
        compiler_params=_cparams(("parallel",)),
        name="mla_prep",
    )(p_mla, cos, sin, *consts)


def _mla_flash_kernel(qi_ref, kj_ref, q_ref, k_ref, wuv_ref, o_ref, m_ref, l_ref, acc_ref, *, tq, tk):
    p = pl.program_id(1)
    qi = qi_ref[p]
    kj = kj_ref[p]
    rows = MLA_HEADS * tq

    @pl.when(kj == 0)
    def _():
        m_ref[...] = jnp.full(m_ref.shape, NEG, F32)
        l_ref[...] = jnp.zeros(l_ref.shape, F32)
        acc_ref[...] = jnp.zeros(acc_ref.shape, F32)

    k = k_ref[...]
    kv = k[:, :MLA_KV_LORA]
    kpos = kj * tk + lax.broadcasted_iota(jnp.int32, (1, tk), 1)
    for h0 in range(0, MLA_HEADS, FLASH_HEADS_PER_BLOCK):
        sub = FLASH_HEADS_PER_BLOCK * tq
        rs = slice(h0 * tq, h0 * tq + sub)
        q = jnp.concatenate([q_ref[:, h * MLA_QK:(h + 1) * MLA_QK] for h in range(h0, h0 + FLASH_HEADS_PER_BLOCK)],
                            axis=0)
        s = lax.dot_general(q, k, _NT, preferred_element_type=F32) * MLA_SCALE
        qpos = qi * tq + (lax.broadcasted_iota(jnp.int32, (sub, 1), 0) & (tq - 1))
        s = jnp.where(kpos <= qpos, s, NEG)
        m_prev = m_ref[rs]
        m_new = jnp.maximum(m_prev, jnp.max(s, axis=1, keepdims=True))
        alpha = jnp.exp(m_prev - m_new)
        pexp = jnp.exp(s - m_new)
        l_ref[rs] = alpha * l_ref[rs] + jnp.sum(pexp, axis=1, keepdims=True)
        acc_ref[rs] = alpha * acc_ref[rs] + jnp.dot(pexp.astype(BF16), kv, preferred_element_type=F32)
        m_ref[rs] = m_new

    @pl.when(kj == (qi * tq + tq - 1) // tk)
    def _():
        lat = (acc_ref[...] / l_ref[...]).astype(BF16)
        outs = [jnp.dot(lat[h * tq:(h + 1) * tq], wuv_ref[h], preferred_element_type=F32)
                for h in range(MLA_HEADS)]
        o_ref[...] = jnp.concatenate(outs, axis=-1)


def mla_flash(q_cat, k_cat, w_uv, n, *, tq=128, tk=1024):
    t = q_cat.shape[0] // n
    assert t % tq == 0 and t % tk == 0 and (tq & (tq - 1)) == 0
    pairs = [(i, j) for i in range(t // tq) for j in range((i * tq + tq - 1) // tk + 1)]
    qi_tab = jnp.asarray([a for a, _ in pairs], jnp.int32)
    kj_tab = jnp.asarray([b for _, b in pairs], jnp.int32)
    rows = MLA_HEADS * tq
    nq, nk = t // tq, t // tk
    grid_spec = pltpu.PrefetchScalarGridSpec(
        num_scalar_prefetch=2,
        grid=(n, len(pairs)),
        in_specs=[pl.BlockSpec((tq, MLA_HEADS * MLA_QK), lambda b, p, qi, kj: (b * nq + qi[p], 0)),
                  pl.BlockSpec((tk, MLA_QK), lambda b, p, qi, kj: (b * nk + kj[p], 0)),
                  pl.BlockSpec((MLA_HEADS, MLA_KV_LORA, MLA_V), lambda b, p, qi, kj: (0, 0, 0))],
        out_specs=pl.BlockSpec((tq, MLA_HEADS * MLA_V), lambda b, p, qi, kj: (b * nq + qi[p], 0)),
        scratch_shapes=[pltpu.VMEM((rows, 1), F32), pltpu.VMEM((rows, 1), F32),
                        pltpu.VMEM((rows, MLA_KV_LORA), F32)])
    return pl.pallas_call(
        functools.partial(_mla_flash_kernel, tq=tq, tk=tk),
        grid_spec=grid_spec,
        out_shape=jax.ShapeDtypeStruct((n * t, MLA_HEADS * MLA_V), F32),
        compiler_params=_cparams(("parallel", "arbitrary")),
        name="mla_flash",
    )(qi_tab, kj_tab, q_cat, k_cat, w_uv)


def _mla_decode_kernel(pt_ref, ql_ref, qr_ref, cnew_ref, rnew_ref, wuv_ref, *refs, pg, t_s):
    ck_refs = refs[:pg]
    kr_refs = refs[pg:2 * pg]
    o_ref, m_ref, l_ref, acc_ref = refs[2 * pg:]
    j = pl.program_id(1)
    rows = ql_ref.shape[1]
    tp = rows // MLA_HEADS

    @pl.when(j == 0)
    def _():
        m_ref[...] = jnp.full(m_ref.shape, NEG, F32)
        l_ref[...] = jnp.zeros(l_ref.shape, F32)
        acc_ref[...] = jnp.zeros(acc_ref.shape, F32)

    ql = ql_ref[0]
    qr = qr_ref[0]
    cks, ss = [], []
    for i in range(pg):
        ck = ck_refs[i][0, 0].astype(BF16)
        kr_t = kr_refs[i][0, 0].astype(BF16)
        cks.append(ck)
        ss.append((lax.dot_general(ql, ck, _NT, preferred_element_type=F32)
                   + jnp.dot(qr, kr_t, preferred_element_type=F32)) * MLA_SCALE)
    gsz = pg // DECODE_PAGE_GROUPS
    parts = []
    for g in range(DECODE_PAGE_GROUPS):
        sg = jnp.concatenate(ss[g * gsz:(g + 1) * gsz], axis=1)
        mg = jnp.max(sg, axis=1, keepdims=True)
        pg_exp = jnp.exp(sg - mg)
        page = sg.shape[1] // gsz
        pv = None
        for i in range(gsz):
            term = jnp.dot(pg_exp[:, i * page:(i + 1) * page].astype(BF16), cks[g * gsz + i],
                           preferred_element_type=F32)
            pv = term if pv is None else pv + term
        parts.append((mg, jnp.sum(pg_exp, axis=1, keepdims=True), pv))
    m_prev = m_ref[...]
    m_new = m_prev
    for mg, _, _ in parts:
        m_new = jnp.maximum(m_new, mg)
    alpha = jnp.exp(m_prev - m_new)
    l_new = alpha * l_ref[...]
    acc_new = alpha * acc_ref[...]
    for mg, lg, pv in parts:
        w = jnp.exp(mg - m_new)
        l_new = l_new + w * lg
        acc_new = acc_new + w * pv
    l_ref[...] = l_new
    acc_ref[...] = acc_new
    m_ref[...] = m_new

    @pl.when(j == pl.num_programs(1) - 1)
    def _():
        cn = cnew_ref[0]
        rn = rnew_ref[0]
        sn = (lax.dot_general(ql, cn, _NT, preferred_element_type=F32)
              + lax.dot_general(qr, rn, _NT, preferred_element_type=F32)) * MLA_SCALE
        tq = lax.broadcasted_iota(jnp.int32, (rows, 1), 0) // MLA_HEADS
        tk = lax.broadcasted_iota(jnp.int32, (1, tp), 1)
        sn = jnp.where((tk <= tq) & (tk < t_s), sn, NEG)
        m2 = jnp.maximum(m_ref[...], jnp.max(sn, axis=1, keepdims=True))
        a2 = jnp.exp(m_ref[...] - m2)
        pn = jnp.exp(sn - m2)
        l2 = a2 * l_ref[...] + jnp.sum(pn, axis=1, keepdims=True)
        acc2 = a2 * acc_ref[...] + jnp.dot(pn.astype(BF16), cn, preferred_element_type=F32)
        lat = (acc2 / l2).astype(BF16)
        full = jnp.dot(lat, wuv_ref[...], preferred_element_type=F32)
        hrow = lax.broadcasted_iota(jnp.int32, (rows, 1), 0) % MLA_HEADS
        hcol = lax.broadcasted_iota(jnp.int32, (1, MLA_HEADS * MLA_V), 1) // MLA_V
        full = jnp.where(hrow == hcol, full, 0.0)
        o_ref[0] = jnp.sum(full.reshape(tp, MLA_HEADS, MLA_HEADS * MLA_V), axis=1)


def mla_decode(page_table, q_lat, q_rope, ckv_new, kr_new, w_uv_all, cache_ckv, cache_kr_t, l, *, t_s, pg=32):
    b, rows, _ = q_lat.shape
    tp = rows // MLA_HEADS
    n_pages = page_table.shape[1]
    page = cache_ckv.shape[2]
    assert n_pages % pg == 0
    pt = page_table.reshape(-1)

    def page_spec(shape, i):
        return pl.BlockSpec((1, 1) + shape, lambda s, j, pt_ref: (l, pt_ref[s * n_pages + j * pg + i], 0, 0))

    seq = lambda a: pl.BlockSpec((1,) + a.shape[1:], lambda s, j, pt_ref: (s, 0, 0))
    grid_spec = pltpu.PrefetchScalarGridSpec(
        num_scalar_prefetch=1,
        grid=(b, n_pages // pg),
        in_specs=[seq(q_lat), seq(q_rope), seq(ckv_new), seq(kr_new),
                  pl.BlockSpec(w_uv_all.shape, lambda s, j, pt_ref: (0, 0))]
                 + [page_spec((page, MLA_KV_LORA), i) for i in range(pg)]
                 + [page_spec((MLA_ROPE, page), i) for i in range(pg)],
        out_specs=pl.BlockSpec((1, tp, MLA_HEADS * MLA_V), lambda s, j, pt_ref: (s, 0, 0)),
        scratch_shapes=[pltpu.VMEM((rows, 1), F32), pltpu.VMEM((rows, 1), F32),
                        pltpu.VMEM((rows, MLA_KV_LORA), F32)])
    return pl.pallas_call(
        functools.partial(_mla_decode_kernel, pg=pg, t_s=t_s),
        grid_spec=grid_spec,
        out_shape=jax.ShapeDtypeStruct((b, tp, MLA_HEADS * MLA_V), F32),
        compiler_params=_cparams(("parallel", "arbitrary")),
        name="mla_decode",
    )(pt, q_lat, q_rope, ckv_new, kr_new, w_uv_all, *([cache_ckv] * pg), *([cache_kr_t] * pg))


def _rope_lanes(x, c, s1, s2):
    n = x.shape[1]
    half = NSA_ROT // 2
    reps = n // LANES
    tile = (lambda a: jnp.concatenate([a] * reps, axis=1)) if reps > 1 else (lambda a: a)
    return x * tile(c) + pltpu.roll(x, n - half, 1) * tile(s1) + pltpu.roll(x, half, 1) * tile(s2)


def _nsa_prep_kernel(p_ref, c_ref, s1_ref, s2_ref, qng_ref, kng_ref, bdq_ref, bdk_ref, wt_ref, pbd_ref,
                     q_ref, kvc_ref, kvs_ref, kvw_ref, gate_ref, *cmp_out, compress):
    p = p_ref[...]
    c, s1, s2 = c_ref[...], s1_ref[...], s2_ref[...]
    q = p[:, :NSA_QW]
    q = q * lax.rsqrt(_group_sum(q * q, bdq_ref[...]) * (1.0 / NSA_DH) + NORM_EPS) * qng_ref[...]
    q_ref[...] = (_rope_lanes(q, c, s1, s2) * NSA_SCALE).astype(BF16)
    outs = (kvc_ref, kvs_ref, kvw_ref)
    kc = None
    for j in range(3):
        base = NSA_QW + j * NSA_KV_COLS
        k = p[:, base:base + LANES]
        k = k * lax.rsqrt(_group_sum(k * k, bdk_ref[...]) * (1.0 / NSA_DH) + NORM_EPS) * kng_ref[j:j + 1, :]
        k = _rope_lanes(k, c, s1, s2)
        outs[j][:, :LANES] = k
        outs[j][:, LANES:] = p[:, base + LANES:base + NSA_KV_COLS]
        if j == 0:
            kc = k
    gate_ref[...] = jax.nn.sigmoid(p[:, NSA_QW + 3 * NSA_KV_COLS:])
    if compress:
        tm = p.shape[0]
        rows = jnp.concatenate([kc, p[:, NSA_QW + LANES:NSA_QW + NSA_KV_COLS]], axis=1)
        pooled = jnp.sum((rows * wt_ref[...]).reshape(tm // NSA_CMP_BLOCK, NSA_CMP_BLOCK, NSA_KV_COLS), axis=1)
        cmp_out[0][...] = jnp.dot(pooled.astype(BF16), pbd_ref[...], preferred_element_type=F32)


def nsa_prep(p_nsa, tabs, wts, *, tm, compress):
    m = p_nsa.shape[0]
    nt = tabs[0].shape[0] // tm
    row = lambda w: pl.BlockSpec((tm, w), lambda i: (i, 0))
    res = lambda a: pl.BlockSpec(a.shape, lambda i: (0,) * a.ndim)
    tab = pl.BlockSpec((tm, LANES), lambda i: (i % nt, 0))
    consts = [wts['nsa_qn_g'], wts['nsa_kn_g'], wts['bd64_512'], wts['bd64_128'], wts['cmp_w'][:tm], wts['cmp_proj']]
    out_specs = [row(NSA_QW), row(NSA_KV_COLS), row(NSA_KV_COLS), row(NSA_KV_COLS), row(LANES)]
    out_shape = [jax.ShapeDtypeStruct((m, NSA_QW), BF16)] + [jax.ShapeDtypeStruct((m, NSA_KV_COLS), F32)] * 3 \
        + [jax.ShapeDtypeStruct((m, LANES), F32)]
    if compress:
        out_specs.append(pl.BlockSpec((tm // NSA_CMP_BLOCK, NSA_KV_COLS), lambda i: (i, 0)))
        out_shape.append(jax.ShapeDtypeStruct((m // NSA_CMP_BLOCK, NSA_KV_COLS), F32))
    return pl.pallas_call(
        functools.partial(_nsa_prep_kernel, compress=compress),
        grid=(m // tm,),
        in_specs=[row(NSA_COLS), tab, tab, tab] + [res(a) for a in consts],
        out_specs=out_specs,
        out_shape=out_shape,
        compiler_params=_cparams(("parallel",)),
        name="nsa_prep",
    )(p_nsa, *tabs, *consts)


def _group_rows(q, g):
    lane = lax.broadcasted_iota(jnp.int32, (1, LANES), 1)
    in_half = (lane // NSA_DH) == g
    zero = jnp.zeros((), q.dtype)
    parts = []
    for pidx in range(NSA_HPG):
        h = g * NSA_HPG + pidx
        tile = q[:, (h // 2) * LANES:(h // 2 + 1) * LANES]
        if h % 2 != g:
            tile = jnp.concatenate([tile[:, NSA_DH:], tile[:, :NSA_DH]], axis=1)
        parts.append(jnp.where(in_half, tile, zero))
    return jnp.concatenate(parts, axis=0)


def _place_heads(o_groups, t):
    lane = lax.broadcasted_iota(jnp.int32, (1, LANES), 1)
    low = lane < NSA_DH
    tiles = []
    for j in range(NSA_HEADS // 2):
        pair = []
        for h in (2 * j, 2 * j + 1):
            g, pidx = divmod(h, NSA_HPG)
            x = o_groups[g][pidx * t:(pidx + 1) * t]
            if h % 2 != g:
                x = jnp.concatenate([x[:, NSA_DH:], x[:, :NSA_DH]], axis=1)
            pair.append(x)
        tiles.append(jnp.where(low, pair[0], pair[1]))
    return jnp.concatenate(tiles, axis=1)


def _expand_gates(gs, exp_ref, j):
    hi = gs.astype(BF16)
    lo = (gs - hi.astype(F32)).astype(BF16)
    e = exp_ref[j]
    return jnp.dot(hi, e, preferred_element_type=F32) + jnp.dot(lo, e, preferred_element_type=F32)


def _top_blocks_t(score_t, blk_row_f, n_rows):
    sel_t = jnp.zeros(score_t.shape, F32)
    for _ in range(NSA_TOPK):
        mx = jnp.max(score_t, axis=0, keepdims=True)
        first = jnp.min(jnp.where(score_t == mx, blk_row_f, float(n_rows)), axis=0, keepdims=True)
        pick = blk_row_f == first
        sel_t = jnp.where(pick, 1.0, sel_t)
        score_t = jnp.where(pick, -3e38, score_t)
    return sel_t


def _nsa_prompt_kernel(q_ref, gate_ref, kvc_ref, kvs_ref, kvw_ref, pool_ref, exp_ref, o_ref, *, tq, tk, nc, n_sel):
    start = pl.program_id(1) * tq
    grows = NSA_HPG * tq
    q = q_ref[...]
    tokpos = start + lax.broadcasted_iota(jnp.int32, (tq, 1), 0)
    qpos = start + (lax.broadcasted_iota(jnp.int32, (grows, 1), 0) & (tq - 1))
    c_end = lax.broadcasted_iota(jnp.int32, (1, nc), 1) * NSA_CMP_BLOCK + (NSA_CMP_BLOCK - 1)
    blk_lane = lax.broadcasted_iota(jnp.int32, (1, LANES), 1)
    blk_row_f = lax.broadcasted_iota(jnp.int32, (LANES, tq), 0).astype(F32)
    cur = tokpos // NSA_SEL_BLOCK
    n_kt = (start + tq - 1) // tk + 1
    lo = jnp.maximum(start - NSA_WINDOW, 0)
    wlen = NSA_WINDOW + tq
    kvc = kvc_ref[0]
    kc = kvc[:, :LANES].astype(BF16)
    vc = kvc[:, LANES:].astype(BF16)
    o_cmp, o_sel, o_win = [], [], []

    for g in range(NSA_GROUPS):
        qg = _group_rows(q, g)

        sc = lax.dot_general(qg, kc, _NT, preferred_element_type=F32)
        pc = _softmax_rows(sc, c_end <= qpos)
        o_cmp.append(jnp.dot(pc.astype(BF16), vc, preferred_element_type=F32))

        psum = pc[0:tq]
        for h in range(1, NSA_HPG):
            psum = psum + pc[h * tq:(h + 1) * tq]
        imp = _group_sum(psum, pool_ref[...])
        forced = (blk_lane == 0) | (blk_lane == cur) | (blk_lane == cur - 1)
        score = jnp.where(blk_lane > cur, -1e9, imp + NSA_FORCE * forced.astype(F32))
        score = jnp.where(blk_lane < n_sel, score, -3e38)
        sel = _top_blocks_t(score.T, blk_row_f, LANES).T.astype(BF16)

        def sel_step(kt, carry):
            m_prev, l_prev, acc = carry
            off = pl.multiple_of(kt * tk, tk)
            kv = kvs_ref[0, pl.ds(off, tk), :]
            k = kv[:, :LANES].astype(BF16)
            v = kv[:, LANES:].astype(BF16)
            s = lax.dot_general(qg, k, _NT, preferred_element_type=F32)
            keyi = kt * tk + lax.broadcasted_iota(jnp.int32, (1, tk), 1)
            expand = (lax.broadcasted_iota(jnp.int32, (LANES, tk), 0) == keyi // NSA_SEL_BLOCK).astype(BF16)
            selm = jnp.dot(sel, expand, preferred_element_type=F32)
            valid = (selm > 0.5) & (keyi <= tokpos)
            s3 = jnp.where(valid[None], s.reshape(NSA_HPG, tq, tk), NEG)
            m_new = jnp.maximum(m_prev, jnp.max(s3, axis=-1, keepdims=True))
            alpha = jnp.exp(m_prev - m_new)
            p3 = jnp.exp(s3 - m_new)
            l_new = alpha * l_prev + jnp.sum(p3, axis=-1, keepdims=True)
            pv = jnp.dot(p3.reshape(grows, tk).astype(BF16), v, preferred_element_type=F32)
            return m_new, l_new, alpha * acc + pv.reshape(NSA_HPG, tq, LANES)

        init = (jnp.full((NSA_HPG, tq, 1), NEG, F32), jnp.zeros((NSA_HPG, tq, 1), F32),
                jnp.zeros((NSA_HPG, tq, LANES), F32))
        _, l_fin, acc = lax.fori_loop(0, n_kt, sel_step, init)
        o_sel.append((acc / l_fin).reshape(grows, LANES))

        woff = pl.multiple_of(lo, tq)
        kvw = kvw_ref[0, pl.ds(woff, wlen), :]
        sw = lax.dot_general(qg, kvw[:, :LANES].astype(BF16), _NT, preferred_element_type=F32)
        dpos = tokpos - (lo + lax.broadcasted_iota(jnp.int32, (1, wlen), 1))
        validw = (dpos >= 0) & (dpos < NSA_WINDOW)
        sw3 = sw.reshape(NSA_HPG, tq, wlen)
        pw = _softmax_rows(sw3, jnp.broadcast_to(validw[None], sw3.shape))
        o_win.append(jnp.dot(pw.reshape(grows, wlen).astype(BF16), kvw[:, LANES:].astype(BF16),
                             preferred_element_type=F32))

    gs = gate_ref[...]
    o_ref[...] = (_expand_gates(gs, exp_ref, 0) * _place_heads(o_cmp, tq)
                  + _expand_gates(gs, exp_ref, 1) * _place_heads(o_sel, tq)
                  + _expand_gates(gs, exp_ref, 2) * _place_heads(o_win, tq))


def nsa_prompt(q8, gates, kvc, kv_s, kv_w, pool, gate_exp, n, *, tq=128, tk=1024):
    t = q8.shape[0] // n
    nc = kvc.shape[1]
    n_sel = t // NSA_SEL_BLOCK
    assert nc == LANES and 2 * n_sel == nc and t % tk == 0 and t >= NSA_WINDOW + tq
    nq = t // tq
    full = lambda a: pl.BlockSpec((1,) + a.shape[1:], lambda b, i: (b, 0, 0))
    res = lambda a: pl.BlockSpec(a.shape, lambda b, i: (0,) * a.ndim)
    return pl.pallas_call(
        functools.partial(_nsa_prompt_kernel, tq=tq, tk=tk, nc=nc, n_sel=n_sel),
        grid=(n, nq),
        in_specs=[pl.BlockSpec((tq, NSA_QW), lambda b, i: (b * nq + i, 0)),
                  pl.BlockSpec((tq, LANES), lambda b, i: (b * nq + i, 0)),
                  full(kvc), full(kv_s), full(kv_w), res(pool), res(gate_exp)],
        out_specs=pl.BlockSpec((tq, NSA_QW), lambda b, i: (b * nq + i, 0)),
        out_shape=jax.ShapeDtypeStruct((n * t, NSA_QW), F32),
        compiler_params=_cparams(("parallel", "arbitrary")),
        name="nsa_prompt",
    )(q8, gates, kvc, kv_s, kv_w, pool, gate_exp)


def _nsa_decode_kernel(pt_ref, q_ref, gate_ref, snew_ref, wnew_ref, win_ref, wt_ref, pbd_ref, pool_ref, exp_ref,
                       e0_ref, *refs, pg, n_steps, t_s, past):
    cmp_refs = refs[:pg]
    sel_refs = refs[pg:2 * pg]
    o_ref = refs[2 * pg]
    pooled_ref, selstep_ref, ocmp_ref, m_ref, l_ref, acc_ref = refs[2 * pg + 1:]
    j = pl.program_id(1)
    tp = q_ref.shape[1]
    grows = NSA_HPG * tp
    rows = NSA_GROUPS * grows
    page = cmp_refs[0].shape[3]
    nc = past // NSA_CMP_BLOCK
    nblk = pool_ref.shape[1]
    bps = pg * page // NSA_SEL_BLOCK
    q = q_ref[0]
    qs = jnp.concatenate([_group_rows(q, g) for g in range(NSA_GROUPS)], axis=0)
    tokpos = past + lax.broadcasted_iota(jnp.int32, (tp, 1), 0)
    rowtok = lax.broadcasted_iota(jnp.int32, (rows, 1), 0) & (tp - 1)

    cps = pg * page // NSA_CMP_BLOCK

    @pl.when(j < n_steps)
    def _():
        xw = jnp.concatenate([r[0, 0] * wt_ref[...] for r in cmp_refs], axis=1)
        pooled_ref[jnp.minimum(j, n_steps - 1)] = _group_sum(xw, e0_ref[...])

    @pl.when(j == n_steps - 1)
    def _():
        pooled = jnp.concatenate([pooled_ref[s][:, :cps] for s in range(n_steps)], axis=1)
        kvc_t = jnp.dot(pbd_ref[...], pooled.astype(BF16), preferred_element_type=F32)
        kc_t = kvc_t[:LANES].astype(BF16)
        vc_t = kvc_t[LANES:].astype(BF16)
        c_end = lax.broadcasted_iota(jnp.int32, (1, nc), 1) * NSA_CMP_BLOCK + (NSA_CMP_BLOCK - 1)
        sc = jnp.dot(qs, kc_t, preferred_element_type=F32)
        pc = _softmax_rows(sc, c_end <= past + rowtok)
        ocmp_ref[...] = lax.dot_general(pc.astype(BF16), vc_t, _NT, preferred_element_type=F32)
        blk_lane = lax.broadcasted_iota(jnp.int32, (1, nblk), 1)
        blk_lane_f = blk_lane.astype(F32)
        cur = tokpos // NSA_SEL_BLOCK
        n_sel = -(-(past + t_s) // NSA_SEL_BLOCK)
        forced = (blk_lane == 0) | (blk_lane == cur) | (blk_lane == cur - 1)
        sels = []
        for g in range(NSA_GROUPS):
            psum = pc[g * grows:g * grows + tp]
            for h in range(1, NSA_HPG):
                psum = psum + pc[g * grows + h * tp:g * grows + (h + 1) * tp]
            imp = _group_sum(psum, pool_ref[...])
            score = jnp.where(blk_lane > cur, -1e9, imp + NSA_FORCE * forced.astype(F32))
            score = jnp.where(blk_lane < n_sel, score, -3e38)
            sel = jnp.zeros((tp, nblk), F32)
            for _ in range(NSA_TOPK):
                mx = jnp.max(score, axis=1, keepdims=True)
                first = jnp.min(jnp.where(score == mx, blk_lane_f, float(nblk)), axis=1, keepdims=True)
                pick = blk_lane_f == first
                sel = jnp.where(pick, 1.0, sel)
                score = jnp.where(pick, -3e38, score)
            sels.append(jnp.concatenate([sel] * NSA_HPG, axis=0))
        sel_rows = jnp.concatenate(sels, axis=0)
        zpad = jnp.zeros((rows, LANES - bps), F32)
        for s in range(nblk // bps):
            selstep_ref[s] = jnp.concatenate([sel_rows[:, s * bps:(s + 1) * bps], zpad], axis=1)
        m_ref[...] = jnp.full(m_ref.shape, NEG, F32)
        l_ref[...] = jnp.zeros(l_ref.shape, F32)
        acc_ref[...] = jnp.zeros(acc_ref.shape, F32)

    def online(s, valid):
        s = jnp.where(valid, s, NEG)
        m_prev = m_ref[...]
        m_new = jnp.maximum(m_prev, jnp.max(s, axis=-1, keepdims=True))
        alpha = jnp.exp(m_prev - m_new)
        p = jnp.where(valid, jnp.exp(s - m_new), 0.0)
        l_ref[...] = alpha * l_ref[...] + jnp.sum(p, axis=-1, keepdims=True)
        m_ref[...] = m_new
        return alpha, p.astype(BF16)

    @pl.when(j >= n_steps)
    def _():
        jj = j - n_steps
        tiles = [r[0, 0] for r in sel_refs]
        s = jnp.concatenate([jnp.dot(qs, x[:LANES].astype(BF16), preferred_element_type=F32) for x in tiles], axis=1)
        tile = selstep_ref[jj]
        lane = lax.broadcasted_iota(jnp.int32, (1, page), 1)
        bpp = page // NSA_SEL_BLOCK
        masks = []
        for i in range(pg):
            m_i = tile[:, i * bpp:i * bpp + 1]
            for b in range(1, bpp):
                m_i = jnp.where(lane >= b * NSA_SEL_BLOCK, tile[:, i * bpp + b:i * bpp + b + 1], m_i)
            masks.append(jnp.broadcast_to(m_i, (rows, page)))
        valid = jnp.concatenate(masks, axis=1) > 0.5
        alpha, p = online(s, valid)
        pv = None
        for i, x in enumerate(tiles):
            term = lax.dot_general(p[:, i * page:(i + 1) * page], x[LANES:].astype(BF16), _NT,
                                   preferred_element_type=F32)
            pv = term if pv is None else pv + term
        acc_ref[...] = alpha * acc_ref[...] + pv

    @pl.when(j == 2 * n_steps - 1)
    def _():
        snew = snew_ref[0]
        wnew = wnew_ref[0]
        tk = lax.broadcasted_iota(jnp.int32, (1, tp), 1)
        causal_new = (tk <= rowtok) & (tk < t_s)
        new_blk = past // NSA_SEL_BLOCK
        lane = lax.broadcasted_iota(jnp.int32, (1, LANES), 1)
        last = selstep_ref[new_blk // bps]
        picked = jnp.sum(jnp.where(lane == new_blk % bps, last, 0.0), axis=1, keepdims=True) > 0.5
        s = lax.dot_general(qs, snew[:, :LANES].astype(BF16), _NT, preferred_element_type=F32)
        alpha, p = online(s, causal_new & picked)
        acc = alpha * acc_ref[...] + jnp.dot(p, snew[:, LANES:].astype(BF16), preferred_element_type=F32)
        o_sel = acc / jnp.maximum(l_ref[...], 1e-30)

        win = win_ref[0, 0]
        win_s = win.shape[1]
        wi = lax.broadcasted_iota(jnp.int32, (1, win_s), 1)
        dpos_c = (past + rowtok) - (past - win_s + wi)
        valid_c = (dpos_c >= 0) & (dpos_c < NSA_WINDOW)
        sc = jnp.dot(qs, win[:LANES].astype(BF16), preferred_element_type=F32)
        sn = lax.dot_general(qs, wnew[:, :LANES].astype(BF16), _NT, preferred_element_type=F32)
        sc = jnp.where(valid_c, sc, NEG)
        sn = jnp.where(causal_new, sn, NEG)
        mw = jnp.maximum(jnp.max(sc, axis=-1, keepdims=True), jnp.max(sn, axis=-1, keepdims=True))
        pcw = jnp.where(valid_c, jnp.exp(sc - mw), 0.0)
        pnw = jnp.where(causal_new, jnp.exp(sn - mw), 0.0)
        den = jnp.maximum(jnp.sum(pcw, axis=-1, keepdims=True) + jnp.sum(pnw, axis=-1, keepdims=True), 1e-30)
        o_win = (lax.dot_general(pcw.astype(BF16), win[LANES:].astype(BF16), _NT, preferred_element_type=F32)
                 + jnp.dot(pnw.astype(BF16), wnew[:, LANES:].astype(BF16), preferred_element_type=F32)) / den
        split = lambda a: [a[g * grows:(g + 1) * grows] for g in range(NSA_GROUPS)]
        gs = gate_ref[0]
        o_ref[0] = (_expand_gates(gs, exp_ref, 0) * _place_heads(split(ocmp_ref[...]), tp)
                    + _expand_gates(gs, exp_ref, 1) * _place_heads(split(o_sel), tp)
                    + _expand_gates(gs, exp_ref, 2) * _place_heads(split(o_win), tp))


def nsa_decode(page_table, q8, gates, kvs_new, kvw_new, cache_cmp_t, cache_sel_t, cache_win_t, wts, l, *, t_s, pg=32):
    b, tp, _ = q8.shape
    n_pages = page_table.shape[1]
    page = cache_cmp_t.shape[3]
    past = n_pages * page
    n_steps = n_pages // pg
    nc = past // NSA_CMP_BLOCK
    assert (past + t_s) // NSA_CMP_BLOCK == nc
    nblk = wts['pool_s'].shape[1]
    bps = pg * page // NSA_SEL_BLOCK
    assert nblk % bps == 0 and bps <= LANES and (past // NSA_SEL_BLOCK) // bps < nblk // bps
    rows = NSA_HEADS * tp
    pt = page_table.reshape(-1)
    assert pg * page // NSA_CMP_BLOCK <= LANES
    e0 = jnp.asarray((np.arange(pg * page)[:, None] // NSA_CMP_BLOCK) == np.arange(LANES)[None, :], BF16)

    def cmp_spec(i):
        return pl.BlockSpec((1, 1, NSA_KV_COLS, page),
                            lambda s, j, pt_ref: (l, pt_ref[s * n_pages + jnp.minimum(j, n_steps - 1) * pg + i], 0, 0))

    def sel_spec(i):
        return pl.BlockSpec((1, 1, NSA_KV_COLS, page),
                            lambda s, j, pt_ref: (l, pt_ref[s * n_pages + jnp.maximum(j - n_steps, 0) * pg + i], 0, 0))

    seq = lambda a: pl.BlockSpec((1,) + a.shape[1:], lambda s, j, pt_ref: (s, 0, 0))
    res = lambda a: pl.BlockSpec(a.shape, lambda s, j, pt_ref: (0,) * a.ndim)
    consts = [wts['cmp_w_t'], wts['cmp_proj_t'], wts['pool_s'], wts['gate_exp'], e0]
    grid_spec = pltpu.PrefetchScalarGridSpec(
        num_scalar_prefetch=1,
        grid=(b, 2 * n_steps),
        in_specs=[seq(q8), seq(gates), seq(kvs_new), seq(kvw_new),
                  pl.BlockSpec((1, 1) + cache_win_t.shape[2:], lambda s, j, pt_ref: (l, s, 0, 0))]
                 + [res(a) for a in consts]
                 + [cmp_spec(i) for i in range(pg)] + [sel_spec(i) for i in range(pg)],
        out_specs=pl.BlockSpec((1, tp, NSA_QW), lambda s, j, pt_ref: (s, 0, 0)),
        scratch_shapes=[pltpu.VMEM((n_steps, NSA_KV_COLS, LANES), F32),
                        pltpu.VMEM((nblk // bps, rows, LANES), F32),
                        pltpu.VMEM((rows, LANES), F32),
                        pltpu.VMEM((rows, 1), F32),
                        pltpu.VMEM((rows, 1), F32),
                        pltpu.VMEM((rows, LANES), F32)])
    return pl.pallas_call(
        functools.partial(_nsa_decode_kernel, pg=pg, n_steps=n_steps, t_s=t_s, past=past),
        grid_spec=grid_spec,
        out_shape=jax.ShapeDtypeStruct((b, tp, NSA_QW), F32),
        compiler_params=_cparams(("parallel", "arbitrary")),
        name="nsa_decode",
    )(pt, q8, gates, kvs_new, kvw_new, cache_win_t, *consts, *([cache_cmp_t] * pg), *([cache_sel_t] * pg))


def _merge_kernel(x_ref, oa_ref, ob_ref, oc_ref, br_ref, pa_ref, pb_ref, pc_ref, wo_ref, y_ref):
    d = D_MODEL
    br = br_ref[...]
    u = (jax.nn.sigmoid(br[:, 0:d]) * jnp.dot(oa_ref[...].astype(BF16), pa_ref[...], preferred_element_type=F32)
         + jax.nn.sigmoid(br[:, d:2 * d]) * jnp.dot(ob_ref[...].astype(BF16), pb_ref[...], preferred_element_type=F32)
         + jax.nn.sigmoid(br[:, 2 * d:3 * d]) * jnp.dot(oc_ref[...].astype(BF16), pc_ref[...], preferred_element_type=F32))
    y_ref[...] = x_ref[...] + jnp.dot(u.astype(BF16), wo_ref[...], preferred_element_type=F32)


def merge(x, o_a, o_b, o_c, br, p_a, p_b, p_c, w_out, *, tm=256):
    m, d = x.shape
    tm = min(tm, m)
    bw = o_a.shape[1]
    row = lambda w: pl.BlockSpec((tm, w), lambda i: (i, 0))
    res = lambda a: pl.BlockSpec(a.shape, lambda i: (0, 0))
    return pl.pallas_call(
        _merge_kernel,
        grid=(m // tm,),
        in_specs=[row(d), row(bw), row(bw), row(bw), row(3 * d), res(p_a), res(p_b), res(p_c), res(w_out)],
        out_specs=row(d),
        out_shape=jax.ShapeDtypeStruct((m, d), F32),
        compiler_params=_cparams(("parallel",)),
        name="merge",
    )(x, o_a, o_b, o_c, br, p_a, p_b, p_c, w_out)


def _router_kernel(x_ref, g_ref, wr_ref, h_ref, route_ref):
    x = x_ref[...]
    ms = jnp.mean(x * x, axis=-1, keepdims=True)
    hb = (x * lax.rsqrt(ms + NORM_EPS) * g_ref[...]).astype(BF16)
    h_ref[...] = hb
    logits = jnp.dot(hb, wr_ref[...], preferred_element_type=F32)
    lane = lax.broadcasted_iota(jnp.int32, (1, ROUTER_PAD), 1)
    lane_f = lane.astype(F32)
    big = float(ROUTER_PAD)

    def top1(vals):
        mx = jnp.max(vals, axis=-1, keepdims=True)
        return mx, jnp.min(jnp.where(vals == mx, lane_f, big), axis=-1, keepdims=True)

    gl = jnp.where(lane < MOE_GROUPS, logits, NEG)
    gmax, gidx = top1(gl)
    g_w = 1.0 / jnp.sum(jnp.where(lane < MOE_GROUPS, jnp.exp(gl - gmax), 0.0), axis=-1, keepdims=True)
    lo = MOE_GROUPS + MOE_EPG * gidx
    el = jnp.where((lane_f >= lo) & (lane_f < lo + MOE_EPG), logits, NEG)
    e1, i1 = top1(el)
    e2, i2 = top1(jnp.where(lane_f == i1, NEG, el))
    z = jnp.exp(e2 - e1)
    w1 = g_w / (1.0 + z)
    w2 = g_w * z / (1.0 + z)
    route_ref[...] = jnp.where(lane == 0, i1 - MOE_GROUPS,
                               jnp.where(lane == 1, i2 - MOE_GROUPS,
                                         jnp.where(lane == 2, w1, jnp.where(lane == 3, w2, 0.0))))


def norm_router(x, g, w_router, *, tm=512):
    m, d = x.shape
    tm = min(tm, m)
    return pl.pallas_call(
        _router_kernel,
        grid=(m // tm,),
        in_specs=[pl.BlockSpec((tm, d), lambda i: (i, 0)),
                  pl.BlockSpec((1, d), lambda i: (0, 0)),
                  pl.BlockSpec((d, ROUTER_PAD), lambda i: (0, 0))],
        out_specs=[pl.BlockSpec((tm, d), lambda i: (i, 0)),
                   pl.BlockSpec((tm, ROUTER_PAD), lambda i: (i, 0))],
        out_shape=[jax.ShapeDtypeStruct((m, d), BF16), jax.ShapeDtypeStruct((m, ROUTER_PAD), F32)],
        compiler_params=_cparams(("parallel",)),
        name="norm_router",
    )(x, g.reshape(1, d), w_router)


def _expert_kernel(be_ref, x_ref, rw_ref, w1_ref, w3_ref, w2_ref, o_ref):
    x = x_ref[...]
    a = jnp.dot(x, w1_ref[0], preferred_element_type=F32)
    b = jnp.dot(x, w3_ref[0], preferred_element_type=F32)
    hmid = _silu(a) * b
    y = jnp.dot(hmid.astype(BF16), w2_ref[0], preferred_element_type=F32)
    o_ref[...] = y * rw_ref[...]


def grouped_experts_blocks(xb, row_w, blk_e, w1, w3, w2):
    rows, d = xb.shape
    n_blk = rows // MOE_BLOCK
    grid_spec = pltpu.PrefetchScalarGridSpec(
        num_scalar_prefetch=1,
        grid=(n_blk,),
        in_specs=[pl.BlockSpec((MOE_BLOCK, d), lambda b, be: (b, 0)),
                  pl.BlockSpec((MOE_BLOCK, 1), lambda b, be: (b, 0)),
                  pl.BlockSpec((1, d, MOE_DFF), lambda b, be: (be[b], 0, 0)),
                  pl.BlockSpec((1, d, MOE_DFF), lambda b, be: (be[b], 0, 0)),
                  pl.BlockSpec((1, MOE_DFF, d), lambda b, be: (be[b], 0, 0))],
        out_specs=pl.BlockSpec((MOE_BLOCK, d), lambda b, be: (b, 0)))
    return pl.pallas_call(
        _expert_kernel,
        grid_spec=grid_spec,
        out_shape=jax.ShapeDtypeStruct((rows, d), F32),
        compiler_params=_cparams(("arbitrary",)),
        name="moe_experts",
    )(blk_e, xb, row_w.reshape(rows, 1), w1, w3, w2)


def hier_moe_residual(x, norm_g, w_router, w1, w3, w2):
    m, d = x.shape
    h, route = norm_router(x, norm_g, w_router)
    eid = route[:, :MOE_TOPK].astype(jnp.int32)
    ew = route[:, MOE_TOPK:2 * MOE_TOPK]

    n_assign = m * MOE_TOPK
    fe = eid.reshape(-1).astype(jnp.int32)
    onehot = (fe[:, None] == jnp.arange(MOE_EXPERTS, dtype=jnp.int32)[None, :]).astype(jnp.int32)
    csum = jnp.cumsum(onehot, axis=0)
    rank = jnp.sum((csum - onehot) * onehot, axis=1)
    counts = csum[-1]
    padded = (counts + MOE_BLOCK - 1) // MOE_BLOCK * MOE_BLOCK
    ends = jnp.cumsum(padded)
    dest = (ends - padded)[fe] + rank
    n_blk = -(-(n_assign + MOE_EXPERTS * (MOE_BLOCK - 1)) // MOE_BLOCK)
    rows = n_blk * MOE_BLOCK
    tok = jnp.arange(n_assign, dtype=jnp.int32) // MOE_TOPK
    row_tok = jnp.full((rows,), m, jnp.int32).at[dest].set(tok)
    row_w = jnp.zeros((rows,), F32).at[dest].set(ew.reshape(-1))
    blk_e = jnp.minimum(jnp.searchsorted(ends, jnp.arange(n_blk, dtype=jnp.int32) * MOE_BLOCK, side='right'),
                        MOE_EXPERTS - 1).astype(jnp.int32)
    xb = jnp.concatenate([h, jnp.zeros((1, d), h.dtype)], axis=0)[row_tok]
    yb = grouped_experts_blocks(xb, row_w, blk_e, w1, w3, w2)
    slot = dest.reshape(m, MOE_TOPK)
    y = yb[slot[:, 0]]
    for k in range(1, MOE_TOPK):
        y = y + yb[slot[:, k]]
    return x + y


def _rope_tables(pos):
    posf = pos.astype(F32)[:, None]
    inv_m = ROPE_THETA ** (-(jnp.arange(MLA_HALF, dtype=F32) * 2.0 / MLA_ROPE))
    ang_m = posf * inv_m[None, :]
    cos_m = jnp.tile(jnp.cos(ang_m), (1, LANES // MLA_HALF))
    sin_m = jnp.tile(jnp.sin(ang_m), (1, LANES // MLA_HALF))
    half = NSA_ROT // 2
    inv_n = ROPE_THETA ** (-(jnp.arange(half, dtype=F32) * 2.0 / NSA_ROT))
    ang_n = posf * inv_n[None, :]
    cn, sn = jnp.cos(ang_n), jnp.sin(ang_n)
    t = pos.shape[0]
    one = jnp.ones((t, NSA_DH - NSA_ROT), F32)
    zero = jnp.zeros((t, NSA_DH - NSA_ROT), F32)
    z8 = jnp.zeros((t, half), F32)
    c = jnp.concatenate([cn, cn, one], axis=1)
    s1 = jnp.concatenate([-sn, z8, zero], axis=1)
    s2 = jnp.concatenate([z8, sn, zero], axis=1)
    rep = lambda a: jnp.tile(a, (1, LANES // NSA_DH))
    return (cos_m, sin_m), (rep(c), rep(s1), rep(s2))


def _layer_weights(l, w_in, raw):
    w = {}
    wl = w_in[l]
    d = wl.shape[0]
    col = lambda i: wl[:, IN_OFF[i]:IN_OFF[i + 1]]
    zeros = lambda n: jnp.zeros((d, n), F32)
    w['w_gdn'] = jnp.concatenate([col(0), col(1), col(2), col(3), zeros(LANES - 2 * GDN_HEADS)], axis=1).astype(BF16)
    kr = col(6)
    kr_rep = jnp.concatenate([jnp.tile(kr[:, :MLA_HALF], (1, MLA_HEADS)), jnp.tile(kr[:, MLA_HALF:], (1, MLA_HEADS))],
                             axis=1)
    w['w_mla'] = jnp.concatenate([col(4), col(5), kr_rep, zeros(LANES)], axis=1).astype(BF16)
    w['w_nsa'] = jnp.concatenate([col(7), col(8), col(9), col(10), col(11), zeros(LANES - 3 * NSA_HEADS)],
                                 axis=1).astype(BF16)
    w['w_br'] = col(12).astype(BF16)

    uq = raw['mla_w_uq'][l].reshape(MLA_Q_LORA, MLA_HEADS, MLA_NOPE + MLA_ROPE)
    w['w_uq'] = jnp.concatenate([uq[:, :, :MLA_NOPE].reshape(MLA_Q_LORA, -1),
                                 uq[:, :, MLA_NOPE:MLA_NOPE + MLA_HALF].reshape(MLA_Q_LORA, -1),
                                 uq[:, :, MLA_NOPE + MLA_HALF:].reshape(MLA_Q_LORA, -1)], axis=1).astype(BF16)
    uk = jnp.transpose(raw['mla_w_uk'][l], (0, 2, 1))
    zk = jnp.zeros_like(uk)
    even = (jnp.arange(MLA_HEADS) % 2 == 0)[:, None, None]
    w['w_uk'] = jnp.concatenate([jnp.where(even, uk, zk), jnp.where(even, zk, uk)], axis=1).astype(BF16)
    w['w_uv'] = raw['mla_w_uv'][l].astype(BF16)
    w['w_uv_all'] = jnp.transpose(raw['mla_w_uv'][l], (1, 0, 2)).reshape(MLA_KV_LORA, MLA_HEADS * MLA_V).astype(BF16)
    w['cq_g'] = raw['mla_cq_g'][l].reshape(1, -1)
    w['ckv_g'] = raw['mla_ckv_g'][l].reshape(1, -1)
    w['qn_g'] = jnp.tile(raw['mla_qn_g'][l], MLA_HEADS).reshape(1, -1)
    split_tile = lambda g: jnp.stack([jnp.tile(g[:MLA_HALF], MLA_HEADS), jnp.tile(g[MLA_HALF:], MLA_HEADS)])
    w['qr_g'] = split_tile(raw['mla_qr_g'][l])
    w['kr_g'] = split_tile(raw['mla_kr_g'][l])
    w['bd64_512'] = _block_diag_ones(NSA_QW, NSA_DH)
    w['bd64_128'] = _block_diag_ones(LANES, NSA_DH)
    w['bd16'] = _block_diag_ones(LANES, MLA_HALF)

    w['nsa_qn_g'] = jnp.tile(raw['nsa_qn_g'][l], NSA_HEADS).reshape(1, -1)
    w['nsa_kn_g'] = jnp.tile(raw['nsa_kn_g'][l], (1, NSA_GROUPS))
    wpos = raw['nsa_cmp_wpos'][l]
    wrow = jnp.concatenate([jnp.tile(wpos[0], (1, NSA_GROUPS)), jnp.tile(wpos[1], (1, NSA_GROUPS))], axis=1)
    w['cmp_w'] = jnp.tile(wrow, (512 // NSA_CMP_BLOCK, 1))
    proj = raw['nsa_cmp_proj'][l]
    pbd = jnp.zeros((NSA_KV_COLS, NSA_KV_COLS), F32)
    for kv in range(2):
        for g in range(NSA_GROUPS):
            o = (kv * NSA_GROUPS + g) * NSA_DH
            pbd = pbd.at[o:o + NSA_DH, o:o + NSA_DH].set(proj[kv])
    w['cmp_proj'] = pbd.astype(BF16)
    w['cmp_w_t'] = jnp.tile(wrow.T, (1, LANES // NSA_CMP_BLOCK))
    w['cmp_proj_t'] = pbd.T.astype(BF16)
    ge = np.zeros((3, LANES, NSA_QW), np.float32)
    for j in range(3):
        for h in range(NSA_HEADS):
            ge[j, 3 * h + j, h * NSA_DH:(h + 1) * NSA_DH] = 1.0
    w['gate_exp'] = jnp.asarray(ge, BF16)

    w['p_gdn'] = raw['p_gdn'][l].astype(BF16)
    w['p_mla'] = raw['p_mla'][l].astype(BF16)
    w['p_nsa'] = raw['p_nsa'][l].astype(BF16)
    w['w_out'] = raw['w_out'][l].astype(BF16)
    w['w_router'] = jnp.pad(jnp.concatenate([raw['moe_w_group'][l], raw['moe_w_expert'][l]], axis=1),
                            ((0, 0), (0, ROUTER_PAD - MOE_GROUPS - MOE_EXPERTS))).astype(BF16)
    w['w1'] = raw['moe_w1'][l].astype(BF16)
    w['w3'] = raw['moe_w3'][l].astype(BF16)
    w['w2'] = raw['moe_w2'][l].astype(BF16)
    return w


def _pool_matrix(n_in, n_out):
    pm = np.zeros((n_in, n_out), np.float32)
    ratio = NSA_SEL_BLOCK // NSA_CMP_BLOCK
    for c in range(n_in):
        if c // ratio < n_out:
            pm[c, c // ratio] = 1.0
    return jnp.asarray(pm, BF16)


def kernel(x_prompt, x_sample, cache_mla_ckv, cache_mla_krope, cache_nsa_cmp, cache_nsa_sel, cache_nsa_win, state_gdn_ssm, state_gdn_conv, page_table, norm1_g, w_in, gdn_conv_w, gdn_a_log, gdn_dt_bias, gdn_out_g, mla_cq_g, mla_ckv_g, mla_w_uq, mla_qn_g, mla_qr_g, mla_kr_g, mla_w_uk, mla_w_uv, nsa_qn_g, nsa_kn_g, nsa_cmp_wpos, nsa_cmp_proj, p_gdn, p_mla, p_nsa, w_out, norm2_g, moe_w_group, moe_w_expert, moe_w1, moe_w3, moe_w2):
    depth = w_in.shape[0]
    b_p, t_p, d = x_prompt.shape
    b_s, t_s, _ = x_sample.shape
    n_pool, page = cache_mla_ckv.shape[1:3]
    past = page_table.shape[1] * page
    win_s = cache_nsa_win.shape[2]
    assert t_p >= GDN_CONV - 1 and t_s >= GDN_CONV - 1 and t_p % GDN_CHUNK == 0
    tp_s = 8
    assert t_s <= tp_s
    raw = dict(mla_w_uq=mla_w_uq, mla_w_uk=mla_w_uk, mla_w_uv=mla_w_uv, mla_cq_g=mla_cq_g, mla_ckv_g=mla_ckv_g,
               mla_qn_g=mla_qn_g, mla_qr_g=mla_qr_g, mla_kr_g=mla_kr_g, nsa_qn_g=nsa_qn_g, nsa_kn_g=nsa_kn_g,
               nsa_cmp_wpos=nsa_cmp_wpos, nsa_cmp_proj=nsa_cmp_proj, p_gdn=p_gdn, p_mla=p_mla, p_nsa=p_nsa,
               w_out=w_out, moe_w_group=moe_w_group, moe_w_expert=moe_w_expert, moe_w1=moe_w1, moe_w3=moe_w3,
               moe_w2=moe_w2)
    pos_p = jnp.arange(t_p, dtype=jnp.int32)
    pos_s = jnp.tile(past + jnp.arange(t_s, dtype=jnp.int32), b_s)
    mla_tab_p, nsa_tab_p = _rope_tables(pos_p)
    mla_tab_s, nsa_tab_s = _rope_tables(pos_s)
    pool_p = _pool_matrix(LANES, LANES)
    nc_s = past // NSA_CMP_BLOCK
    pool_s = _pool_matrix(nc_s, nc_s)
    feat_major = lambda a: jnp.transpose(a, (0, 1, 3, 4, 5, 2)).reshape(a.shape[0], a.shape[1], NSA_KV_COLS, a.shape[2])
    cache_cmp_t = feat_major(cache_nsa_cmp)
    cache_sel_t = feat_major(cache_nsa_sel)
    cache_win_t = feat_major(cache_nsa_win)
    cache_kr_t = jnp.transpose(cache_mla_krope, (0, 1, 3, 2))

    def project(x2, l, w):
        g = norm1_g[l]
        return (norm_matmul(x2, g, w['w_gdn']), norm_matmul(x2, g, w['w_mla']),
                norm_matmul(x2, g, w['w_nsa']), norm_matmul(x2, g, w['w_br']))

    def finish(x2, o_a, o_b, o_c, br, l, w):
        y = merge(x2, o_a, o_b, o_c, br, w['p_gdn'], w['p_mla'], w['p_nsa'], w['w_out'])
        return hier_moe_residual(y, norm2_g[l], w['w_router'], w['w1'], w['w3'], w['w2'])

    def prompt_layer(x, l, w):
        m = b_p * t_p
        x2 = x.reshape(m, d)
        pg_, pm_, pn_, br = project(x2, l, w)
        pg3 = pg_.reshape(b_p, t_p, GDN_COLS)
        o_a, ssm = gdn_mixer(pg3, jnp.zeros((b_p, 8, GDN_CONV_DIM), F32),
                             jnp.zeros((b_p, GDN_HEADS, GDN_DK, GDN_DV), F32),
                             gdn_conv_w[l], gdn_a_log[l], gdn_dt_bias[l], gdn_out_g[l],
                             chunk=GDN_CHUNK, t_valid=GDN_CHUNK, nb=1, cb=4)
        conv = pg3[:, t_p - (GDN_CONV - 1):, :GDN_CONV_DIM]
        q_cat, k_cat, ckv, krope = mla_prep(pm_, *mla_tab_p, w, tm=256)
        o_b = mla_flash(q_cat, k_cat, w['w_uv'], b_p)
        q8, kv_c, kv_s, kv_w, gates, kvc = nsa_prep(pn_, nsa_tab_p, w, tm=256, compress=True)
        o_c = nsa_prompt(q8, gates, kvc.reshape(b_p, t_p // NSA_CMP_BLOCK, NSA_KV_COLS),
                         kv_s.reshape(b_p, t_p, NSA_KV_COLS), kv_w.reshape(b_p, t_p, NSA_KV_COLS),
                         pool_p, w['gate_exp'], b_p)
        y = finish(x2, o_a.reshape(m, GDN_VW), o_b, o_c, br, l, w).reshape(b_p, t_p, d)
        kv5 = lambda a: a.reshape(b_p, t_p, 2, NSA_GROUPS, NSA_DH)
        kvw5 = kv5(kv_w)
        return y, (ssm, conv, ckv.reshape(b_p, t_p, MLA_KV_LORA), krope.reshape(b_p, t_p, MLA_ROPE),
                   kv5(kv_c), kv5(kv_s), kvw5[:, t_p - min(NSA_WINDOW, t_p):])

    def sample_layer(x, l, w):
        m = b_s * t_s
        x2 = x.reshape(m, d)
        pg_, pm_, pn_, br = project(x2, l, w)
        pad_t = lambda a: jnp.pad(a.reshape(b_s, t_s, -1), ((0, 0), (0, tp_s - t_s), (0, 0)))
        pg3 = pg_.reshape(b_s, t_s, GDN_COLS)
        hist8 = jnp.pad(state_gdn_conv[l], ((0, 0), (8 - (GDN_CONV - 1), 0), (0, 0)))
        o_a, ssm = gdn_mixer(pad_t(pg_), hist8, state_gdn_ssm[l], gdn_conv_w[l], gdn_a_log[l], gdn_dt_bias[l],
                             gdn_out_g[l], chunk=tp_s, t_valid=t_s, nb=4, cb=1)
        o_a = o_a[:, :t_s].reshape(m, GDN_VW)
        conv = pg3[:, t_s - (GDN_CONV - 1):, :GDN_CONV_DIM]

        q_cat, k_cat, ckv, krope = mla_prep(pm_, *mla_tab_s, w, tm=m)
        qc = q_cat.reshape(m, MLA_HEADS, MLA_QK)
        q_lat = qc[:, :, :MLA_KV_LORA]
        hsel = jnp.arange(MLA_HEADS)
        r1 = qc[:, :, MLA_KV_LORA:MLA_KV_LORA + LANES].reshape(m, MLA_HEADS, MLA_HEADS, MLA_HALF)[:, hsel, hsel]
        r2 = qc[:, :, MLA_KV_LORA + LANES:].reshape(m, MLA_HEADS, MLA_HEADS, MLA_HALF)[:, hsel, hsel]
        q_rope = jnp.concatenate([r1, r2], axis=-1)
        rows_of = lambda a: pad_t(a.reshape(m, -1)).reshape(b_s, tp_s * MLA_HEADS, -1)
        o_b = mla_decode(page_table, rows_of(q_lat), rows_of(q_rope), pad_t(ckv).astype(BF16),
                         pad_t(krope).astype(BF16), w['w_uv_all'], cache_mla_ckv, cache_kr_t, l, t_s=t_s)
        o_b = o_b[:, :t_s].reshape(m, MLA_HEADS * MLA_V)

        q8, kv_c, kv_s, kv_w, gates = nsa_prep(pn_, nsa_tab_s, w, tm=m, compress=False)
        ws = dict(w, pool_s=pool_s)
        o_c = nsa_decode(page_table, pad_t(q8), pad_t(gates), pad_t(kv_s), pad_t(kv_w), cache_cmp_t, cache_sel_t,
                         cache_win_t, ws, l, t_s=t_s)
        o_c = o_c[:, :t_s].reshape(m, NSA_QW)
        y = finish(x2, o_a, o_b, o_c, br, l, w).reshape(b_s, t_s, d)
        kv5 = lambda a: a.reshape(b_s, t_s, 2, NSA_GROUPS, NSA_DH)
        win_new = jnp.concatenate([cache_nsa_win[l], kv5(kv_w)], axis=1)[:, t_s:]
        return y, (ssm, conv, ckv.reshape(b_s, t_s, MLA_KV_LORA), krope.reshape(b_s, t_s, MLA_ROPE),
                   kv5(kv_c), kv5(kv_s), win_new)

    xp, xs = x_prompt, x_sample
    p_states, s_states = [], []
    for l in range(depth):
        w = _layer_weights(l, w_in, raw)
        xp, st = prompt_layer(xp, l, w)
        p_states.append(st)
        xs, st = sample_layer(xs, l, w)
        s_states.append(st)
    p_out = [jnp.stack([st[i] for st in p_states], axis=0) for i in range(7)]
    s_out = [jnp.stack([st[i] for st in s_states], axis=0) for i in range(7)]
    return (xp, xs, *p_out, *s_out)
```
